```python
import jax
import jax.numpy as jnp
from jax import lax
import numpy as np

D_MODEL = 1024
BATCH = 8
SEQ = 2048
DEPTH = 1

N_META = 16
GLA_WIDTH = D_MODEL // 2
GLA_HEADS = 4
GLA_DV = GLA_WIDTH // GLA_HEADS
GLA_DK = GLA_DV // 2
GLA_QK = GLA_HEADS * GLA_DK
GATE_RANK = 16
GATE_NORM = 16.0
CHUNK = 64
SUB = 16
CONV_CH = D_MODEL - GLA_WIDTH
CONV_K = 3
D_FF = 4 * D_MODEL
EPS = 1e-6
PROJ_SIZES = (GLA_QK, GLA_QK, GLA_WIDTH, GLA_WIDTH, GATE_RANK, CONV_CH, CONV_CH, CONV_CH)
PROJ_WIDTH = 2 * GLA_QK + 2 * GLA_WIDTH + GATE_RANK + 3 * CONV_CH

kernel_name = "hymba_gla_shortconv_block"


def rms_norm(x, w):
    xf = x.astype(jnp.float32)
    y = xf * lax.rsqrt(jnp.mean(xf * xf, axis=-1, keepdims=True) + EPS)
    return (y * w.astype(jnp.float32)).astype(x.dtype)


def split_cols(p):
    outs = []
    start = 0
    for s in PROJ_SIZES:
        outs.append(p[..., start:start + s])
        start += s
    return outs


def gla_chunked(q, k, v, gk):
    Bsz, Lp, H, dk = q.shape
    dv = v.shape[-1]
    N = Lp // CHUNK
    S = CHUNK // SUB

    def blk(t):
        return t.reshape(Bsz, N, S, SUB, H, t.shape[-1]).transpose(0, 4, 1, 2, 3, 5)

    q, k, v, gk = blk(q), blk(k), blk(v), blk(gk)
    b = jnp.cumsum(gk.reshape(Bsz, H, N, CHUNK, dk), axis=3).reshape(Bsz, H, N, S, SUB, dk)

    b_end = b[..., -1, :]
    r = jnp.concatenate([jnp.zeros_like(b_end[:, :, :, :1]), b_end[:, :, :, :-1]], axis=3)
    q_r = q * jnp.exp(b - r[..., None, :])
    lower = jnp.arange(S)[:, None] > jnp.arange(S)[None, :]
    expo = r[:, :, :, :, None, None, :] - b[:, :, :, None, :, :, :]
    expo = jnp.where(lower[:, :, None, None], expo, -jnp.inf)
    k_rel = k[:, :, :, None] * jnp.exp(expo)
    a_off = jnp.einsum('bhnsid,bhnstjd->bhnsitj', q_r, k_rel)

    causal = jnp.tril(jnp.ones((SUB, SUB), dtype=bool))
    pair = b[..., :, None, :] - b[..., None, :, :]
    pair = jnp.where(causal[:, :, None], pair, -jnp.inf)
    a_diag = jnp.einsum('bhnsid,bhnsjd,bhnsijd->bhnsij', q, k, jnp.exp(pair))
    eye = jnp.eye(S, dtype=a_off.dtype)
    a = a_off + a_diag[..., None, :] * eye[:, None, :, None]
    a = a.reshape(Bsz, H, N, CHUNK, CHUNK)

    qc = q.reshape(Bsz, H, N, CHUNK, dk)
    kc = k.reshape(Bsz, H, N, CHUNK, dk)
    vc = v.reshape(Bsz, H, N, CHUNK, dv)
    bc = b.reshape(Bsz, H, N, CHUNK, dk)
    o_intra = jnp.einsum('bhnij,bhnjv->bhniv', a, vc)

    b_last = bc[..., -1, :]
    kv = jnp.einsum('bhncd,bhncv->bhndv', kc * jnp.exp(b_last[..., None, :] - bc), vc)

    def step(h, inp):
        decay, kv_n = inp
        return h * decay[..., None] + kv_n, h

    h0 = jnp.zeros((Bsz, H, dk, dv), dtype=q.dtype)
    _, h_prev = lax.scan(step, h0, (jnp.moveaxis(jnp.exp(b_last), 2, 0), jnp.moveaxis(kv, 2, 0)))
    h_prev = jnp.moveaxis(h_prev, 0, 2)
    o_inter = jnp.einsum('bhncd,bhndv->bhncv', qc * jnp.exp(bc), h_prev)

    o = o_intra + o_inter
    return o.transpose(0, 2, 3, 1, 4).reshape(Bsz, Lp, H, dv)


def gla_branch(q, k, v, g, gr, w_gate_up, b_gate, norm_w):
    Bsz, L, _ = q.shape
    f32 = jnp.float32
    qh = q.astype(f32).reshape(Bsz, L, GLA_HEADS, GLA_DK) * (GLA_DK ** -0.5)
    kh = k.astype(f32).reshape(Bsz, L, GLA_HEADS, GLA_DK)
    vh = v.astype(f32).reshape(Bsz, L, GLA_HEADS, GLA_DV)
    gk = jax.nn.log_sigmoid(gr.astype(f32) @ w_gate_up.astype(f32) + b_gate.astype(f32)) / GATE_NORM
    gk = gk.reshape(Bsz, L, GLA_HEADS, GLA_DK)
    front = (-N_META) % CHUNK
    back = (-(front + L)) % CHUNK

    def padf(t):
        return jnp.pad(t, ((0, 0), (front, back), (0, 0), (0, 0)))

    o = gla_chunked(padf(qh), padf(kh), padf(vh), padf(gk))[:, front:front + L]
    o = rms_norm(o, norm_w)
    o = o.reshape(Bsz, L, GLA_WIDTH) * jax.nn.silu(g.astype(f32))
    return o.astype(q.dtype)


def short_conv_branch(cb, cc, cx, conv_w):
    u = cc * cx
    y = lax.conv_general_dilated(
        u, conv_w[:, None, :].astype(u.dtype), window_strides=(1,),
        padding=[(CONV_K - 1, 0)], dimension_numbers=('NWC', 'WIO', 'NWC'),
        feature_group_count=CONV_CH)
    return cb * y


def setup_inputs(seed: int = 0) -> dict:
    key = jax.random.key(seed)
    ks = jax.random.split(key, 14)
    n = jax.random.normal
    return {
        'x': n(ks[0], (BATCH, SEQ, D_MODEL), jnp.float32),
        'meta_tokens': n(ks[1], (N_META, D_MODEL), jnp.float32),
        'norm_mix_w': 1.0 + 0.01 * n(ks[2], (DEPTH, D_MODEL), jnp.float32),
        'w_in': n(ks[3], (DEPTH, D_MODEL, PROJ_WIDTH), jnp.float32) * D_MODEL ** -0.5,
        'w_gate_up': n(ks[4], (DEPTH, GATE_RANK, GLA_QK), jnp.float32) * GATE_RANK ** -0.5,
        'b_gate': 0.1 * n(ks[5], (DEPTH, GLA_QK), jnp.float32),
        'gla_norm_w': 1.0 + 0.01 * n(ks[6], (DEPTH, GLA_DV), jnp.float32),
        'conv_w': n(ks[7], (DEPTH, CONV_K, CONV_CH), jnp.float32) * CONV_K ** -0.5,
        'w_out': n(ks[8], (DEPTH, D_MODEL, D_MODEL), jnp.float32) * D_MODEL ** -0.5,
        'norm_mlp_w': 1.0 + 0.01 * n(ks[9], (DEPTH, D_MODEL), jnp.float32),
        'w_up': n(ks[10], (DEPTH, D_MODEL, D_FF), jnp.float32) * D_MODEL ** -0.5,
        'w_down': n(ks[11], (DEPTH, D_FF, D_MODEL), jnp.float32) * D_FF ** -0.5,
        'norm_final_w': 1.0 + 0.01 * n(ks[12], (D_MODEL,), jnp.float32),
    }


def reference(x, meta_tokens, norm_mix_w, w_in, w_gate_up, b_gate, gla_norm_w, conv_w,
              w_out, norm_mlp_w, w_up, w_down, norm_final_w):
    Bsz = x.shape[0]
    meta = jnp.broadcast_to(meta_tokens.astype(x.dtype)[None], (Bsz, N_META, D_MODEL))
    h = jnp.concatenate([meta, x], axis=1)
    for layer in range(DEPTH):
        hn = rms_norm(h, norm_mix_w[layer])
        proj = hn @ w_in[layer]
        q, k, v, g, gr, cb, cc, cx = split_cols(proj)
        y_gla = gla_branch(q, k, v, g, gr, w_gate_up[layer], b_gate[layer], gla_norm_w[layer])
        y_conv = short_conv_branch(cb, cc, cx, conv_w[layer]).astype(h.dtype)
        mixed = jnp.concatenate([y_gla.astype(h.dtype), y_conv], axis=-1) @ w_out[layer]
        h = h + mixed
        hn = rms_norm(h, norm_mlp_w[layer])
        h = h + jnp.square(jax.nn.relu(hn @ w_up[layer])) @ w_down[layer]
    out = rms_norm(h, norm_final_w)
    return out[:, N_META:]
```

```python
import functools

import numpy as np
import jax
import jax.numpy as jnp
from jax import lax
from jax.experimental import pallas as pl
from jax.experimental.pallas import tpu as pltpu

D_MODEL = 1024
N_META = 16
HEADS = 4
DK = 64
DV = 128
QK = HEADS * DK
GW = HEADS * DV
RANK = 16
GATE_NORM = 16.0
CHUNK = 64
CONV_CH = 512
D_FF = 4096
EPS = 1e-6

LANES = 128
SUBLANES = 8
HALF_WIDTHS = (32, 16, 8, 4, 2, 1)
N_LEVELS = len(HALF_WIDTHS)
VMEM_LIMIT = 56 * 1024 * 1024

MIX_TILE = 512
MLP_TILE = 512
FF_CHUNK = 1024

BF16 = jnp.bfloat16
F32 = jnp.float32


def _rms(x, w):
    ms = jnp.mean(x * x, axis=-1, keepdims=True)
    return x * lax.rsqrt(ms + EPS) * w


def _log_sigmoid(z):
    return -(jnp.maximum(-z, 0.0) + jnp.log1p(jnp.exp(-jnp.abs(z))))


def _dot(a, b):
    return jnp.dot(a, b, preferred_element_type=F32)


def _dot_nt(a, b):
    return lax.dot_general(a, b, (((1,), (1,)), ((), ())), preferred_element_type=F32)


def _split3(x):
    hi = x.astype(BF16)
    r1 = x - hi.astype(F32)
    mid = r1.astype(BF16)
    lo = (r1 - mid.astype(F32)).astype(BF16)
    return hi, mid, lo


def _stack_heads(x):
    lane_head = lax.broadcasted_iota(jnp.int32, x.shape, 1) >> (DK.bit_length() - 1)
    return jnp.concatenate([jnp.where(lane_head == h, x, 0.0) for h in range(HEADS)], axis=0)


def _gate_log_decay(gr, wgu, bg):
    z = jnp.dot(gr, wgu, precision=lax.Precision.HIGHEST, preferred_element_type=F32) + bg
    return _log_sigmoid(z) / GATE_NORM


def _state_update(st, k, v, b):
    rows = k.shape[0]
    b_last = b[rows - 1:rows, :]
    kdec = k * jnp.exp(b_last - b)
    v_rows = jnp.concatenate([v[:, h * DV:(h + 1) * DV] for h in range(HEADS)], axis=0)
    kv = _dot(v_rows.T.astype(BF16), _stack_heads(kdec).astype(BF16))
    return st * jnp.exp(b_last) + kv


def _meta_kernel(meta_ref, nw_ref, wmain_ref, wgr_ref, wgu_ref, bg_ref, tri_ref, st_ref, tail_ref):
    hn = _rms(meta_ref[...], nw_ref[...]).astype(BF16)
    k = _dot(hn, wmain_ref[:, QK:2 * QK])
    v = _dot(hn, wmain_ref[:, 2 * QK:2 * QK + GW])
    cc = _dot(hn, wmain_ref[:, 2 * QK + 2 * GW + CONV_CH:2 * QK + 2 * GW + 2 * CONV_CH])
    cx = _dot(hn, wmain_ref[:, 2 * QK + 2 * GW + 2 * CONV_CH:2 * QK + 2 * GW + 3 * CONV_CH])
    gr = _dot(hn, wgr_ref[...])
    gk = _gate_log_decay(gr, wgu_ref[...], bg_ref[...])
    b = _dot(tri_ref[...], jnp.concatenate(_split3(gk), axis=0))
    st_ref[...] = _state_update(jnp.zeros((DV, QK), F32), k, v, b)
    u = cc * cx
    tail_ref[...] = u[N_META - SUBLANES:N_META, :]


def _mixer_kernel(x_ref, nw_ref, wmain_ref, wgr_ref, wgu_ref, bg_ref, gnw_ref, cw_ref, wout_ref,
                  cmat_ref, lvl_ref, st0_ref, tail0_ref, h1_ref,
                  st_ref, tail_ref, q_s, k_s, v_s, gk_s, o_s, u_s):
    t = pl.program_id(1)

    @pl.when(t == 0)
    def _():
        st_ref[...] = st0_ref[...]
        tail_ref[...] = tail0_ref[...]

    x = x_ref[0]
    hn = _rms(x, nw_ref[...]).astype(BF16)
    q_s[...] = _dot(hn, wmain_ref[:, 0:QK]) * (DK ** -0.5)
    k_s[...] = _dot(hn, wmain_ref[:, QK:2 * QK])
    v_s[...] = _dot(hn, wmain_ref[:, 2 * QK:2 * QK + GW])
    gr = _dot(hn, wgr_ref[...])
    gk_s[...] = _gate_log_decay(gr, wgu_ref[...], bg_ref[...])

    lvl = lvl_ref[...]

    def chunk_body(c, st):
        r0 = pl.multiple_of(c * CHUNK, CHUNK)
        q = q_s[pl.ds(r0, CHUNK), :]
        k = k_s[pl.ds(r0, CHUNK), :]
        v = v_s[pl.ds(r0, CHUNK), :]
        gk = gk_s[pl.ds(r0, CHUNK), :]
        br = _dot(cmat_ref[...], jnp.concatenate(_split3(gk), axis=0))
        b = br[0:CHUNK]
        row = lax.broadcasted_iota(jnp.int32, (CHUNK, QK), 0)
        a = jnp.zeros((HEADS * CHUNK, CHUNK), F32)
        for l, w in enumerate(HALF_WIDTHS):
            ref = br[(l + 1) * CHUNK:(l + 2) * CHUNK]
            e = jnp.exp(-jnp.abs(b - ref))
            second = (row & w) != 0
            ql = jnp.where(second, q * e, 0.0)
            kl = jnp.where(second, 0.0, k * e)
            g = _dot_nt(_stack_heads(ql).astype(BF16), kl.astype(BF16))
            a = jnp.where(lvl == l, g, a)
        g = _dot_nt(_stack_heads(q).astype(BF16), k.astype(BF16))
        a = jnp.where(lvl == N_LEVELS, g, a)
        a = a.astype(BF16)
        o_inter = _dot_nt(_stack_heads(q * jnp.exp(b)).astype(BF16), st.astype(BF16))
        for h in range(HEADS):
            o_h = _dot(a[h * CHUNK:(h + 1) * CHUNK], v[:, h * DV:(h + 1) * DV].astype(BF16))
            o_s[pl.ds(r0, CHUNK), h * DV:(h + 1) * DV] = o_h + o_inter[h * CHUNK:(h + 1) * CHUNK]
        return _state_update(st, k, v, b)

    st_ref[...] = lax.fori_loop(0, MIX_TILE // CHUNK, chunk_body, st_ref[...])

    g_gate = _dot(hn, wmain_ref[:, 2 * QK + GW:2 * QK + 2 * GW])
    gate = g_gate / (1.0 + jnp.exp(-g_gate))
    gnw = gnw_ref[...]
    y_gla = jnp.concatenate(
        [_rms(o_s[:, h * DV:(h + 1) * DV], gnw) for h in range(HEADS)], axis=1) * gate

    c0 = 2 * QK + 2 * GW
    cb = _dot(hn, wmain_ref[:, c0:c0 + CONV_CH])
    cc = _dot(hn, wmain_ref[:, c0 + CONV_CH:c0 + 2 * CONV_CH])
    cx = _dot(hn, wmain_ref[:, c0 + 2 * CONV_CH:c0 + 3 * CONV_CH])
    u = cc * cx
    u_s[0:SUBLANES, :] = tail_ref[...]
    u_s[SUBLANES:SUBLANES + MIX_TILE, :] = u
    tail_ref[...] = u[MIX_TILE - SUBLANES:MIX_TILE, :]
    u1 = u_s[SUBLANES - 1:SUBLANES - 1 + MIX_TILE, :]
    u2 = u_s[SUBLANES - 2:SUBLANES - 2 + MIX_TILE, :]
    y_conv = cb * (cw_ref[0:1, :] * u2 + cw_ref[1:2, :] * u1 + cw_ref[2:3, :] * u)

    mixed = (_dot(y_gla.astype(BF16), wout_ref[0:GW, :])
             + _dot(y_conv.astype(BF16), wout_ref[GW:GW + CONV_CH, :]))
    h1_ref[0] = x + mixed


def _mlp_kernel(h_ref, nw_ref, wup_ref, wdown_ref, fw_ref, out_ref):
    h = h_ref[...]
    hn = _rms(h, nw_ref[...]).astype(BF16)
    acc = jnp.zeros(h.shape, F32)
    for f in range(0, D_FF, FF_CHUNK):
        up = _dot(hn, wup_ref[:, f:f + FF_CHUNK])
        act = jnp.square(jnp.maximum(up, 0.0)).astype(BF16)
        acc = acc + _dot(act, wdown_ref[f:f + FF_CHUNK, :])
    out_ref[...] = _rms(h + acc, fw_ref[...])


def _cumsum_and_ref_matrix():
    tri = np.tril(np.ones((CHUNK, CHUNK), np.float32))
    blocks = [tri]
    idx = np.arange(CHUNK)
    for w in HALF_WIDTHS:
        ref_row = (idx // (2 * w)) * (2 * w) + w - 1
        blocks.append(tri[ref_row])
    return np.concatenate(blocks, axis=0)


def _level_matrix():
    i = np.arange(CHUNK)[:, None]
    j = np.arange(CHUNK)[None, :]
    lvl = np.full((CHUNK, CHUNK), -1, np.int32)
    for l, w in enumerate(HALF_WIDTHS):
        same = (i // (2 * w)) == (j // (2 * w))
        lvl[same & ((i & w) != 0) & ((j & w) == 0)] = l
    lvl[i == j] = N_LEVELS
    return np.tile(lvl, (HEADS, 1))


def _const_spec(shape):
    return pl.BlockSpec(shape, lambda *_: (0,) * len(shape))


def kernel(x, meta_tokens, norm_mix_w, w_in, w_gate_up, b_gate, gla_norm_w, conv_w, w_out,
           norm_mlp_w, w_up, w_down, norm_final_w):
    batch, seq, _ = x.shape
    assert seq % MIX_TILE == 0 and (batch * seq) % MLP_TILE == 0
    assert norm_mix_w.shape[0] == 1, "single layer"

    w_in0 = w_in[0]
    gr0 = 2 * QK + 2 * GW
    w_main = jnp.concatenate([w_in0[:, :gr0], w_in0[:, gr0 + RANK:]], axis=1).astype(BF16)
    w_gr = jnp.pad(w_in0[:, gr0:gr0 + RANK], ((0, 0), (0, LANES - RANK))).astype(BF16)
    wgu = jnp.pad(w_gate_up[0], ((0, LANES - RANK), (0, 0)))
    bg = b_gate[0].reshape(1, QK)
    nw_mix = norm_mix_w[0].reshape(1, D_MODEL)
    gnw = gla_norm_w[0].reshape(1, DV)
    cw = conv_w[0]
    wout = w_out[0].astype(BF16)
    nw_mlp = norm_mlp_w[0].reshape(1, D_MODEL)
    wup = w_up[0].astype(BF16)
    wdown = w_down[0].astype(BF16)
    fw = norm_final_w.reshape(1, D_MODEL)

    cmat = _cumsum_and_ref_matrix()
    cmat3 = jnp.asarray(np.concatenate([cmat, cmat, cmat], axis=1), BF16)
    tri16 = np.tril(np.ones((N_META, N_META), np.float32))
    tri3 = jnp.asarray(np.concatenate([tri16, tri16, tri16], axis=1), BF16)
    lvl = jnp.asarray(_level_matrix())

    params = pltpu.CompilerParams(vmem_limit_bytes=VMEM_LIMIT)

    st0, tail0 = pl.pallas_call(
        _meta_kernel,
        out_shape=(jax.ShapeDtypeStruct((DV, QK), F32), jax.ShapeDtypeStruct((SUBLANES, CONV_CH), F32)),
        compiler_params=params,
        name="gla_meta_state",
    )(meta_tokens, nw_mix, w_main, w_gr, wgu, bg, tri3)

    n_tiles = seq // MIX_TILE
    h1 = pl.pallas_call(
        _mixer_kernel,
        grid=(batch, n_tiles),
        in_specs=[
            pl.BlockSpec((1, MIX_TILE, D_MODEL), lambda b, t: (b, t, 0)),
            _const_spec((1, D_MODEL)),
            _const_spec(w_main.shape),
            _const_spec(w_gr.shape),
            _const_spec(wgu.shape),
            _const_spec((1, QK)),
            _const_spec((1, DV)),
            _const_spec(cw.shape),
            _const_spec(wout.shape),
            _const_spec(cmat3.shape),
            _const_spec(lvl.shape),
            _const_spec((DV, QK)),
            _const_spec((SUBLANES, CONV_CH)),
        ],
        out_specs=pl.BlockSpec((1, MIX_TILE, D_MODEL), lambda b, t: (b, t, 0)),
        out_shape=jax.ShapeDtypeStruct((batch, seq, D_MODEL), F32),
        scratch_shapes=[
            pltpu.VMEM((DV, QK), F32),
            pltpu.VMEM((SUBLANES, CONV_CH), F32),
            pltpu.VMEM((MIX_TILE, QK), F32),
            pltpu.VMEM((MIX_TILE, QK), F32),
            pltpu.VMEM((MIX_TILE, GW), F32),
            pltpu.VMEM((MIX_TILE, QK), F32),
            pltpu.VMEM((MIX_TILE, GW), F32),
            pltpu.VMEM((MIX_TILE + SUBLANES, CONV_CH), F32),
        ],
        compiler_params=pltpu.CompilerParams(
            vmem_limit_bytes=VMEM_LIMIT, dimension_semantics=("arbitrary", "arbitrary")),
        name="gla_conv_mixer",
    )(x, nw_mix, w_main, w_gr, wgu, bg, gnw, cw, wout, cmat3, lvl, st0, tail0)

    rows = batch * seq
    out = pl.pallas_call(
        _mlp_kernel,
        grid=(rows // MLP_TILE,),
        in_specs=[
            pl.BlockSpec((MLP_TILE, D_MODEL), lambda i: (i, 0)),
            _const_spec((1, D_MODEL)),
            _const_spec(wup.shape),
            _const_spec(wdown.shape),
            _const_spec((1, D_MODEL)),
        ],
        out_specs=pl.BlockSpec((MLP_TILE, D_MODEL), lambda i: (i, 0)),
        out_shape=jax.ShapeDtypeStruct((rows, D_MODEL), F32),
        compiler_params=pltpu.CompilerParams(
            vmem_limit_bytes=VMEM_LIMIT, dimension_semantics=("arbitrary",)),
        name="relu2_mlp_final_norm",
    )(h1.reshape(rows, D_MODEL), nw_mlp, wup, wdown, fw)
    return out.reshape(batch, seq, D_MODEL)
```

```python
import numpy as np
import jax
import jax.numpy as jnp
from jax import lax
from jax.experimental import pallas as pl
from jax.experimental.pallas import tpu as pltpu

D_MODEL = 1024
N_META = 16
HEADS = 4
DK = 64
DV = 128
QK = HEADS * DK
GW = HEADS * DV
RANK = 16
GATE_NORM = 16.0
CONV_CH = 512
D_FF = 4096
EPS = 1e-6

LANES = 128
SUBLANES = 8
VMEM_LIMIT = 56 * 1024 * 1024

MIX_TILE = 512
WIDE_CHUNK = 256
CHUNK = 64
HALF_WIDTHS = (32, 16, 8, 4, 2, 1)
N_LEVELS = len(HALF_WIDTHS)
WIDE_KEY_BOUND = 1e18
GATE_TERMS = 6
MLP_TILE = 512
FF_CHUNK = 1024

BF16 = jnp.bfloat16
F32 = jnp.float32


def _rms(x, w):
    ms = jnp.mean(x * x, axis=-1, keepdims=True)
    return x * lax.rsqrt(ms + EPS) * w


def _log_sigmoid(z):
    return -(jnp.maximum(-z, 0.0) + jnp.log1p(jnp.exp(-jnp.abs(z))))


def _silu(g):
    return g * (0.5 * (1.0 + jnp.tanh(0.5 * g)))


def _dot(a, b):
    return jnp.dot(a, b, preferred_element_type=F32)


def _dot_nt(a, b):
    return lax.dot_general(a, b, (((1,), (1,)), ((), ())), preferred_element_type=F32)


def _split3(x):
    hi = x.astype(BF16)
    r1 = x - hi.astype(F32)
    mid = r1.astype(BF16)
    lo = (r1 - mid.astype(F32)).astype(BF16)
    return hi, mid, lo


def _stack_heads(x):
    lane_head = lax.broadcasted_iota(jnp.int32, x.shape, 1) >> (DK.bit_length() - 1)
    return jnp.concatenate([jnp.where(lane_head == h, x, 0.0) for h in range(HEADS)], axis=0)


def _rows_by_head(v):
    return jnp.concatenate([v[:, h * DV:(h + 1) * DV] for h in range(HEADS)], axis=0)


def _gate_log_decay(gr6, wgu6, bg):
    hi, mid, lo = _split3(gr6)
    group = lax.broadcasted_iota(jnp.int32, gr6.shape, 1) >> (RANK.bit_length() - 1)
    lhs = jnp.where(group < 3, hi, jnp.where(group < 5, mid, lo))
    z = _dot(lhs, wgu6) + bg
    return _log_sigmoid(z) / GATE_NORM


def _cumsum_rows(tri3, gk):
    return _dot(tri3, jnp.concatenate(_split3(gk), axis=0))


def _state_update(st, kdec, v, decay_last):
    kv = _dot(_rows_by_head(v).T.astype(BF16), _stack_heads(kdec).astype(BF16))
    return st * decay_last + kv


def _meta_kernel(meta_ref, nw_ref, wmain_ref, wgr_ref, wgu_ref, bg_ref, tri_ref, st_ref, tail_ref):
    hn = _rms(meta_ref[...], nw_ref[...]).astype(BF16)
    c0 = 2 * QK + 2 * GW
    k = _dot(hn, wmain_ref[:, QK:2 * QK])
    v = _dot(hn, wmain_ref[:, 2 * QK:2 * QK + GW])
    cc = _dot(hn, wmain_ref[:, c0 + CONV_CH:c0 + 2 * CONV_CH])
    cx = _dot(hn, wmain_ref[:, c0 + 2 * CONV_CH:c0 + 3 * CONV_CH])
    gk = _gate_log_decay(_dot(hn, wgr_ref[...]), wgu_ref[...], bg_ref[...])
    b = _cumsum_rows(tri_ref[...], gk)
    b_last = b[N_META - 1:N_META, :]
    st_ref[...] = _state_update(jnp.zeros((DV, QK), F32), k * jnp.exp(b_last - b), v, jnp.exp(b_last))
    u = cc * cx
    tail_ref[...] = u[N_META - SUBLANES:N_META, :]


def _gla_wide(q_s, v_s, b_s, qp_s, kp_s, o_s, st_ref):
    row = lax.broadcasted_iota(jnp.int32, (HEADS * WIDE_CHUNK, WIDE_CHUNK), 0) & (WIDE_CHUNK - 1)
    col = lax.broadcasted_iota(jnp.int32, (HEADS * WIDE_CHUNK, WIDE_CHUNK), 1)
    causal = row >= col
    st = st_ref[...]
    for c in range(MIX_TILE // WIDE_CHUNK):
        rows = slice(c * WIDE_CHUNK, (c + 1) * WIDE_CHUNK)
        kp = kp_s[rows, :]
        v = v_s[rows, :]
        qp_heads = _stack_heads(qp_s[rows, :]).astype(BF16)
        sc = _dot_nt(qp_heads, jnp.concatenate([kp.astype(BF16), st.astype(BF16)], axis=0))
        a = jnp.where(causal, sc[:, 0:WIDE_CHUNK], 0.0).astype(BF16)
        for h in range(HEADS):
            hr = slice(h * WIDE_CHUNK, (h + 1) * WIDE_CHUNK)
            o_h = _dot(a[hr], v[:, h * DV:(h + 1) * DV].astype(BF16))
            o_s[rows, h * DV:(h + 1) * DV] = o_h + sc[hr, WIDE_CHUNK:WIDE_CHUNK + DV]
        decay_last = jnp.exp(b_s[(c + 1) * WIDE_CHUNK - 1:(c + 1) * WIDE_CHUNK, :])
        st = _state_update(st, kp * decay_last, v, decay_last)
    st_ref[...] = st


def _gla_stable(q_s, k_s, v_s, gk_s, o_s, st_ref, cmat_ref, lvl_ref):
    lvl = lvl_ref[...]

    def chunk_body(c, st):
        r0 = pl.multiple_of(c * CHUNK, CHUNK)
        q = q_s[pl.ds(r0, CHUNK), :]
        k = k_s[pl.ds(r0, CHUNK), :]
        v = v_s[pl.ds(r0, CHUNK), :]
        gk = gk_s[pl.ds(r0, CHUNK), :]
        br = _cumsum_rows(cmat_ref[...], gk)
        b = br[0:CHUNK]
        row = lax.broadcasted_iota(jnp.int32, (CHUNK, QK), 0)
        a = jnp.zeros((HEADS * CHUNK, CHUNK), F32)
        for l, w in enumerate(HALF_WIDTHS):
            ref = br[(l + 1) * CHUNK:(l + 2) * CHUNK]
            e = jnp.exp(-jnp.abs(b - ref))
            second = (row & w) != 0
            ql = jnp.where(second, q * e, 0.0)
            kl = jnp.where(second, 0.0, k * e)
            g = _dot_nt(_stack_heads(ql).astype(BF16), kl.astype(BF16))
            a = jnp.where(lvl == l, g, a)
        g = _dot_nt(_stack_heads(q).astype(BF16), k.astype(BF16))
        a = jnp.where(lvl == N_LEVELS, g, a)
        a = a.astype(BF16)
        o_inter = _dot_nt(_stack_heads(q * jnp.exp(b)).astype(BF16), st.astype(BF16))
        for h in range(HEADS):
            o_h = _dot(a[h * CHUNK:(h + 1) * CHUNK], v[:, h * DV:(h + 1) * DV].astype(BF16))
            o_s[pl.ds(r0, CHUNK), h * DV:(h + 1) * DV] = o_h + o_inter[h * CHUNK:(h + 1) * CHUNK]
        b_last = b[CHUNK - 1:CHUNK, :]
        return _state_update(st, k * jnp.exp(b_last - b), v, jnp.exp(b_last))

    st_ref[...] = lax.fori_loop(0, MIX_TILE // CHUNK, chunk_body, st_ref[...])


def _mixer_kernel(x_ref, nw_ref, wmain_ref, wgr_ref, wgu_ref, bg_ref, gnw_ref, cw_ref, wout_ref,
                  tri_ref, cmat_ref, lvl_ref, st0_ref, tail0_ref, h1_ref,
                  st_ref, tail_ref, q_s, k_s, v_s, gk_s, b_s, qp_s, kp_s, o_s, u_s):
    t = pl.program_id(1)

    @pl.when(t == 0)
    def _():
        st_ref[...] = st0_ref[...]
        tail_ref[...] = tail0_ref[...]

    x = x_ref[0]
    hn = _rms(x, nw_ref[...]).astype(BF16)
    q = _dot(hn, wmain_ref[:, 0:QK]) * (DK ** -0.5)
    k = _dot(hn, wmain_ref[:, QK:2 * QK])
    v_s[...] = _dot(hn, wmain_ref[:, 2 * QK:2 * QK + GW])
    gk = _gate_log_decay(_dot(hn, wgr_ref[...]), wgu_ref[...], bg_ref[...])
    q_s[...] = q
    k_s[...] = k
    gk_s[...] = gk

    b = jnp.concatenate(
        [_cumsum_rows(tri_ref[...], gk[c * WIDE_CHUNK:(c + 1) * WIDE_CHUNK])
         for c in range(MIX_TILE // WIDE_CHUNK)], axis=0)
    kp = k * jnp.exp(-b)
    b_s[...] = b
    qp_s[...] = q * jnp.exp(b)
    kp_s[...] = kp
    n_bad = jnp.sum(jnp.where(jnp.abs(kp) <= WIDE_KEY_BOUND, 0.0, 1.0))
    wide_ok = n_bad == 0.0

    @pl.when(wide_ok)
    def _():
        _gla_wide(q_s, v_s, b_s, qp_s, kp_s, o_s, st_ref)

    @pl.when(jnp.logical_not(wide_ok))
    def _():
        _gla_stable(q_s, k_s, v_s, gk_s, o_s, st_ref, cmat_ref, lvl_ref)

    gate = _silu(_dot(hn, wmain_ref[:, 2 * QK + GW:2 * QK + 2 * GW]))
    gnw = gnw_ref[...]
    y_gla = jnp.concatenate(
        [_rms(o_s[:, h * DV:(h + 1) * DV], gnw) for h in range(HEADS)], axis=1) * gate

    c0 = 2 * QK + 2 * GW
    cb = _dot(hn, wmain_ref[:, c0:c0 + CONV_CH])
    cc = _dot(hn, wmain_ref[:, c0 + CONV_CH:c0 + 2 * CONV_CH])
    cx = _dot(hn, wmain_ref[:, c0 + 2 * CONV_CH:c0 + 3 * CONV_CH])
    u = cc * cx
    u_s[0:SUBLANES, :] = tail_ref[...]
    u_s[SUBLANES:SUBLANES + MIX_TILE, :] = u
    tail_ref[...] = u[MIX_TILE - SUBLANES:MIX_TILE, :]
    u1 = u_s[SUBLANES - 1:SUBLANES - 1 + MIX_TILE, :]
    u2 = u_s[SUBLANES - 2:SUBLANES - 2 + MIX_TILE, :]
    y_conv = cb * (cw_ref[0:1, :] * u2 + cw_ref[1:2, :] * u1 + cw_ref[2:3, :] * u)

    mixed = (_dot(y_gla.astype(BF16), wout_ref[0:GW, :])
             + _dot(y_conv.astype(BF16), wout_ref[GW:GW + CONV_CH, :]))
    h1_ref[0] = x + mixed


def _mlp_kernel(h_ref, nw_ref, wup_ref, wdown_ref, fw_ref, out_ref):
    h = h_ref[...]
    hn = _rms(h, nw_ref[...]).astype(BF16)
    acc = jnp.zeros(h.shape, F32)
    for f in range(0, D_FF, FF_CHUNK):
        up = _dot(hn, wup_ref[:, f:f + FF_CHUNK])
        act = jnp.square(jnp.maximum(up, 0.0)).astype(BF16)
        acc = acc + _dot(act, wdown_ref[f:f + FF_CHUNK, :])
    out_ref[...] = _rms(h + acc, fw_ref[...])


def _tri3(n):
    tri = np.tril(np.ones((n, n), np.float32))
    return np.concatenate([tri, tri, tri], axis=1)


def _cumsum_and_ref_matrix():
    tri = np.tril(np.ones((CHUNK, CHUNK), np.float32))
    blocks = [tri]
    idx = np.arange(CHUNK)
    for w in HALF_WIDTHS:
        ref_row = (idx // (2 * w)) * (2 * w) + w - 1
        blocks.append(tri[ref_row])
    return np.concatenate(blocks, axis=0)


def _level_matrix():
    i = np.arange(CHUNK)[:, None]
    j = np.arange(CHUNK)[None, :]
    lvl = np.full((CHUNK, CHUNK), -1, np.int32)
    for l, w in enumerate(HALF_WIDTHS):
        same = (i // (2 * w)) == (j // (2 * w))
        lvl[same & ((i & w) != 0) & ((j & w) == 0)] = l
    lvl[i == j] = N_LEVELS
    return np.tile(lvl, (HEADS, 1))


def _const_spec(shape):
    return pl.BlockSpec(shape, lambda *_: (0,) * len(shape))


def kernel(x, meta_tokens, norm_mix_w, w_in, w_gate_up, b_gate, gla_norm_w, conv_w, w_out,
           norm_mlp_w, w_up, w_down, norm_final_w):
    batch, seq, _ = x.shape
    assert seq % MIX_TILE == 0 and (batch * seq) % MLP_TILE == 0
    assert norm_mix_w.shape[0] == 1, "single layer"

    w_in0 = w_in[0]
    gr0 = 2 * QK + 2 * GW
    w_main = jnp.concatenate([w_in0[:, :gr0], w_in0[:, gr0 + RANK:]], axis=1).astype(BF16)
    w_gr = w_in0[:, gr0:gr0 + RANK]
    w_gr6 = jnp.pad(jnp.tile(w_gr, (1, GATE_TERMS)), ((0, 0), (0, LANES - GATE_TERMS * RANK))).astype(BF16)
    g_hi, g_mid, g_lo = _split3(w_gate_up[0])
    wgu6 = jnp.pad(jnp.concatenate([g_hi, g_mid, g_lo, g_hi, g_mid, g_hi], axis=0),
                   ((0, LANES - GATE_TERMS * RANK), (0, 0)))
    bg = b_gate[0].reshape(1, QK)
    nw_mix = norm_mix_w[0].reshape(1, D_MODEL)
    gnw = gla_norm_w[0].reshape(1, DV)
    cw = conv_w[0]
    wout = w_out[0].astype(BF16)
    nw_mlp = norm_mlp_w[0].reshape(1, D_MODEL)
    wup = w_up[0].astype(BF16)
    wdown = w_down[0].astype(BF16)
    fw = norm_final_w.reshape(1, D_MODEL)

    cmat = _cumsum_and_ref_matrix()
    cmat3 = jnp.asarray(np.concatenate([cmat, cmat, cmat], axis=1), BF16)
    tri_meta = jnp.asarray(_tri3(N_META), BF16)
    tri_wide = jnp.asarray(_tri3(WIDE_CHUNK), BF16)
    lvl = jnp.asarray(_level_matrix())

    st0, tail0 = pl.pallas_call(
        _meta_kernel,
        out_shape=(jax.ShapeDtypeStruct((DV, QK), F32), jax.ShapeDtypeStruct((SUBLANES, CONV_CH), F32)),
        compiler_params=pltpu.CompilerParams(vmem_limit_bytes=VMEM_LIMIT),
        name="gla_meta_state",
    )(meta_tokens, nw_mix, w_main, w_gr6, wgu6, bg, tri_meta)

    n_tiles = seq // MIX_TILE
    h1 = pl.pallas_call(
        _mixer_kernel,
        grid=(batch, n_tiles),
        in_specs=[
            pl.BlockSpec((1, MIX_TILE, D_MODEL), lambda b, t: (b, t, 0)),
            _const_spec((1, D_MODEL)),
            _const_spec(w_main.shape),
            _const_spec(w_gr6.shape),
            _const_spec(wgu6.shape),
            _const_spec((1, QK)),
            _const_spec((1, DV)),
            _const_spec(cw.shape),
            _const_spec(wout.shape),
            _const_spec(tri_wide.shape),
            _const_spec(cmat3.shape),
            _const_spec(lvl.shape),
            _const_spec((DV, QK)),
            _const_spec((SUBLANES, CONV_CH)),
        ],
        out_specs=pl.BlockSpec((1, MIX_TILE, D_MODEL), lambda b, t: (b, t, 0)),
        out_shape=jax.ShapeDtypeStruct((batch, seq, D_MODEL), F32),
        scratch_shapes=[
            pltpu.VMEM((DV, QK), F32),
            pltpu.VMEM((SUBLANES, CONV_CH), F32),
            pltpu.VMEM((MIX_TILE, QK), F32),
            pltpu.VMEM((MIX_TILE, QK), F32),
            pltpu.VMEM((MIX_TILE, GW), F32),
            pltpu.VMEM((MIX_TILE, QK), F32),
            pltpu.VMEM((MIX_TILE, QK), F32),
            pltpu.VMEM((MIX_TILE, QK), F32),
            pltpu.VMEM((MIX_TILE, QK), F32),
            pltpu.VMEM((MIX_TILE, GW), F32),
            pltpu.VMEM((MIX_TILE + SUBLANES, CONV_CH), F32),
        ],
        compiler_params=pltpu.CompilerParams(
            vmem_limit_bytes=VMEM_LIMIT, dimension_semantics=("arbitrary", "arbitrary")),
        name="gla_conv_mixer",
    )(x, nw_mix, w_main, w_gr6, wgu6, bg, gnw, cw, wout, tri_wide, cmat3, lvl, st0, tail0)

    rows = batch * seq
    out = pl.pallas_call(
        _mlp_kernel,
        grid=(rows // MLP_TILE,),
        in_specs=[
            pl.BlockSpec((MLP_TILE, D_MODEL), lambda i: (i, 0)),
            _const_spec((1, D_MODEL)),
            _const_spec(wup.shape),
            _const_spec(wdown.shape),
            _const_spec((1, D_MODEL)),
        ],
        out_specs=pl.BlockSpec((MLP_TILE, D_MODEL), lambda i: (i, 0)),
        out_shape=jax.ShapeDtypeStruct((rows, D_MODEL), F32),
        compiler_params=pltpu.CompilerParams(
            vmem_limit_bytes=VMEM_LIMIT, dimension_semantics=("arbitrary",)),
        name="relu2_mlp_final_norm",
    )(h1.reshape(rows, D_MODEL), nw_mlp, wup, wdown, fw)
    return out.reshape(batch, seq, D_MODEL)
```

```python
import numpy as np
import jax
import jax.numpy as jnp
from jax import lax
from jax.experimental import pallas as pl
from jax.experimental.pallas import tpu as pltpu

D_MODEL = 1024
N_META = 16
HEADS = 4
DK = 64
DV = 128
QK = HEADS * DK
GW = HEADS * DV
RANK = 16
GATE_NORM = 16.0
CONV_CH = 512
D_FF = 4096
EPS = 1e-6

LANES = 128
SUBLANES = 8
VMEM_LIMIT = 56 * 1024 * 1024

MIX_TILE = 512
WIDE_CHUNK = 256
CHUNK = 64
HALF_WIDTHS = (32, 16, 8, 4, 2, 1)
N_LEVELS = len(HALF_WIDTHS)
WIDE_KEY_BOUND = 1e18
GATE_TERMS = 6
MLP_TILE = 512
FF_CHUNK = 1024

BF16 = jnp.bfloat16
F32 = jnp.float32


def _rms(x, w):
    ms = jnp.mean(x * x, axis=-1, keepdims=True)
    return x * lax.rsqrt(ms + EPS) * w


def _log_sigmoid(z):
    return -(jnp.maximum(-z, 0.0) + jnp.log1p(jnp.exp(-jnp.abs(z))))


def _silu(g):
    return g * (0.5 * (1.0 + jnp.tanh(0.5 * g)))


def _dot(a, b):
    return jnp.dot(a, b, preferred_element_type=F32)


def _dot_nt(a, b):
    return lax.dot_general(a, b, (((1,), (1,)), ((), ())), preferred_element_type=F32)


def _split3(x):
    hi = x.astype(BF16)
    r1 = x - hi.astype(F32)
    mid = r1.astype(BF16)
    lo = (r1 - mid.astype(F32)).astype(BF16)
    return hi, mid, lo


def _stack_heads(x):
    lane_head = lax.broadcasted_iota(jnp.int32, x.shape, 1) >> (DK.bit_length() - 1)
    return jnp.concatenate([jnp.where(lane_head == h, x, 0.0) for h in range(HEADS)], axis=0)


def _rows_by_head(v):
    return jnp.concatenate([v[:, h * DV:(h + 1) * DV] for h in range(HEADS)], axis=0)


def _gate_log_decay(gr6, wgu6, bg):
    hi, mid, lo = _split3(gr6)
    group = lax.broadcasted_iota(jnp.int32, gr6.shape, 1) >> (RANK.bit_length() - 1)
    lhs = jnp.where(group < 3, hi, jnp.where(group < 5, mid, lo))
    z = _dot(lhs, wgu6) + bg
    return _log_sigmoid(z) / GATE_NORM


def _cumsum_rows(tri3, gk):
    return _dot(tri3, jnp.concatenate(_split3(gk), axis=0))


def _state_update(st, kdec, v, decay_last):
    kv = _dot(_rows_by_head(v).T.astype(BF16), _stack_heads(kdec).astype(BF16))
    return st * decay_last + kv


def _meta_kernel(meta_ref, nw_ref, wmain_ref, wgr_ref, wgu_ref, bg_ref, tri_ref, st_ref, tail_ref):
    hn = _rms(meta_ref[...], nw_ref[...]).astype(BF16)
    c0 = 2 * QK + 2 * GW
    k = _dot(hn, wmain_ref[:, QK:2 * QK])
    v = _dot(hn, wmain_ref[:, 2 * QK:2 * QK + GW])
    cc = _dot(hn, wmain_ref[:, c0 + CONV_CH:c0 + 2 * CONV_CH])
    cx = _dot(hn, wmain_ref[:, c0 + 2 * CONV_CH:c0 + 3 * CONV_CH])
    gk = _gate_log_decay(_dot(hn, wgr_ref[...]), wgu_ref[...], bg_ref[...])
    b = _cumsum_rows(tri_ref[...], gk)
    b_last = b[N_META - 1:N_META, :]
    st_ref[...] = _state_update(jnp.zeros((DV, QK), F32), k * jnp.exp(b_last - b), v, jnp.exp(b_last))
    u = cc * cx
    tail_ref[...] = u[N_META - SUBLANES:N_META, :]


def _gla_wide(q_s, v_s, b_s, qp_s, kp_s, o_s, st_ref):
    row = lax.broadcasted_iota(jnp.int32, (HEADS * WIDE_CHUNK, WIDE_CHUNK), 0) & (WIDE_CHUNK - 1)
    col = lax.broadcasted_iota(jnp.int32, (HEADS * WIDE_CHUNK, WIDE_CHUNK), 1)
    causal = row >= col
    st = st_ref[...]
    for c in range(MIX_TILE // WIDE_CHUNK):
        rows = slice(c * WIDE_CHUNK, (c + 1) * WIDE_CHUNK)
        kp = kp_s[rows, :]
        v = v_s[rows, :]
        qp_heads = _stack_heads(qp_s[rows, :]).astype(BF16)
        sc = _dot_nt(qp_heads, jnp.concatenate([kp.astype(BF16), st.astype(BF16)], axis=0))
        a = jnp.where(causal, sc[:, 0:WIDE_CHUNK], 0.0).astype(BF16)
        for h in range(HEADS):
            hr = slice(h * WIDE_CHUNK, (h + 1) * WIDE_CHUNK)
            o_h = _dot(a[hr], v[:, h * DV:(h + 1) * DV].astype(BF16))
            o_s[rows, h * DV:(h + 1) * DV] = o_h + sc[hr, WIDE_CHUNK:WIDE_CHUNK + DV]
        decay_last = jnp.exp(b_s[(c + 1) * WIDE_CHUNK - 1:(c + 1) * WIDE_CHUNK, :])
        st = _state_update(st, kp * decay_last, v, decay_last)
    st_ref[...] = st


def _gla_stable(q_s, k_s, v_s, gk_s, o_s, st_ref, cmat_ref, lvl_ref):
    lvl = lvl_ref[...]

    def chunk_body(c, st):
        r0 = pl.multiple_of(c * CHUNK, CHUNK)
        q = q_s[pl.ds(r0, CHUNK), :]
        k = k_s[pl.ds(r0, CHUNK), :]
        v = v_s[pl.ds(r0, CHUNK), :]
        gk = gk_s[pl.ds(r0, CHUNK), :]
        br = _cumsum_rows(cmat_ref[...], gk)
        b = br[0:CHUNK]
        row = lax.broadcasted_iota(jnp.int32, (CHUNK, QK), 0)
        a = jnp.zeros((HEADS * CHUNK, CHUNK), F32)
        for l, w in enumerate(HALF_WIDTHS):
            ref = br[(l + 1) * CHUNK:(l + 2) * CHUNK]
            e = jnp.exp(-jnp.abs(b - ref))
            second = (row & w) != 0
            ql = jnp.where(second, q * e, 0.0)
            kl = jnp.where(second, 0.0, k * e)
            g = _dot_nt(_stack_heads(ql).astype(BF16), kl.astype(BF16))
            a = jnp.where(lvl == l, g, a)
        g = _dot_nt(_stack_heads(q).astype(BF16), k.astype(BF16))
        a = jnp.where(lvl == N_LEVELS, g, a)
        a = a.astype(BF16)
        o_inter = _dot_nt(_stack_heads(q * jnp.exp(b)).astype(BF16), st.astype(BF16))
        for h in range(HEADS):
            o_h = _dot(a[h * CHUNK:(h + 1) * CHUNK], v[:, h * DV:(h + 1) * DV].astype(BF16))
            o_s[pl.ds(r0, CHUNK), h * DV:(h + 1) * DV] = o_h + o_inter[h * CHUNK:(h + 1) * CHUNK]
        b_last = b[CHUNK - 1:CHUNK, :]
        return _state_update(st, k * jnp.exp(b_last - b), v, jnp.exp(b_last))

    st_ref[...] = lax.fori_loop(0, MIX_TILE // CHUNK, chunk_body, st_ref[...])


def _mixer_kernel(x_ref, nw_ref, wmain_ref, wgr_ref, wgu_ref, bg_ref, gnw_ref, cw_ref, wout_ref,
                  tri_ref, cmat_ref, lvl_ref, st0_ref, tail0_ref, h1_ref,
                  st_ref, tail_ref, q_s, k_s, v_s, gk_s, b_s, qp_s, kp_s, o_s, u_s, gate_s, yconv_s):
    t = pl.program_id(1)

    @pl.when(t == 0)
    def _():
        st_ref[...] = st0_ref[...]
        tail_ref[...] = tail0_ref[...]

    hn = _rms(x_ref[0], nw_ref[...]).astype(BF16)
    gk = _gate_log_decay(_dot(hn, wgr_ref[...]), wgu_ref[...], bg_ref[...])
    k = _dot(hn, wmain_ref[:, QK:2 * QK])
    q = _dot(hn, wmain_ref[:, 0:QK]) * (DK ** -0.5)
    q_s[...] = q
    k_s[...] = k
    gk_s[...] = gk
    b = jnp.concatenate(
        [_cumsum_rows(tri_ref[...], gk[c * WIDE_CHUNK:(c + 1) * WIDE_CHUNK])
         for c in range(MIX_TILE // WIDE_CHUNK)], axis=0)
    kp = k * jnp.exp(-b)
    b_s[...] = b
    qp_s[...] = q * jnp.exp(b)
    kp_s[...] = kp
    n_bad = jnp.sum(jnp.where(jnp.abs(kp) <= WIDE_KEY_BOUND, 0.0, 1.0))
    wide_ok = n_bad == 0.0

    c0 = 2 * QK + 2 * GW
    cc = _dot(hn, wmain_ref[:, c0 + CONV_CH:c0 + 2 * CONV_CH])
    cx = _dot(hn, wmain_ref[:, c0 + 2 * CONV_CH:c0 + 3 * CONV_CH])
    cb = _dot(hn, wmain_ref[:, c0:c0 + CONV_CH])
    u = cc * cx
    u_s[0:SUBLANES, :] = tail_ref[...]
    u_s[SUBLANES:SUBLANES + MIX_TILE, :] = u
    tail_ref[...] = u[MIX_TILE - SUBLANES:MIX_TILE, :]
    u1 = u_s[SUBLANES - 1:SUBLANES - 1 + MIX_TILE, :]
    u2 = u_s[SUBLANES - 2:SUBLANES - 2 + MIX_TILE, :]
    yconv_s[...] = (cb * (cw_ref[0:1, :] * u2 + cw_ref[1:2, :] * u1 + cw_ref[2:3, :] * u)).astype(BF16)
    gate_s[...] = _silu(_dot(hn, wmain_ref[:, 2 * QK + GW:2 * QK + 2 * GW]))
    v_s[...] = _dot(hn, wmain_ref[:, 2 * QK:2 * QK + GW])

    @pl.when(wide_ok)
    def _():
        _gla_wide(q_s, v_s, b_s, qp_s, kp_s, o_s, st_ref)

    @pl.when(jnp.logical_not(wide_ok))
    def _():
        _gla_stable(q_s, k_s, v_s, gk_s, o_s, st_ref, cmat_ref, lvl_ref)

    gnw = gnw_ref[...]
    y_gla = jnp.concatenate(
        [_rms(o_s[:, h * DV:(h + 1) * DV], gnw) for h in range(HEADS)], axis=1) * gate_s[...]
    mixed = (_dot(yconv_s[...], wout_ref[GW:GW + CONV_CH, :])
             + _dot(y_gla.astype(BF16), wout_ref[0:GW, :]))
    h1_ref[0] = x_ref[0] + mixed


def _mlp_kernel(h_ref, nw_ref, wup_ref, wdown_ref, fw_ref, out_ref):
    h = h_ref[...]
    hn = _rms(h, nw_ref[...]).astype(BF16)
    acc = jnp.zeros(h.shape, F32)
    for f in range(0, D_FF, FF_CHUNK):
        up = _dot(hn, wup_ref[:, f:f + FF_CHUNK])
        act = jnp.square(jnp.maximum(up, 0.0)).astype(BF16)
        acc = acc + _dot(act, wdown_ref[f:f + FF_CHUNK, :])
    out_ref[...] = _rms(h + acc, fw_ref[...])


def _tri3(n):
    tri = np.tril(np.ones((n, n), np.float32))
    return np.concatenate([tri, tri, tri], axis=1)


def _cumsum_and_ref_matrix():
    tri = np.tril(np.ones((CHUNK, CHUNK), np.float32))
    blocks = [tri]
    idx = np.arange(CHUNK)
    for w in HALF_WIDTHS:
        ref_row = (idx // (2 * w)) * (2 * w) + w - 1
        blocks.append(tri[ref_row])
    return np.concatenate(blocks, axis=0)


def _level_matrix():
    i = np.arange(CHUNK)[:, None]
    j = np.arange(CHUNK)[None, :]
    lvl = np.full((CHUNK, CHUNK), -1, np.int32)
    for l, w in enumerate(HALF_WIDTHS):
        same = (i // (2 * w)) == (j // (2 * w))
        lvl[same & ((i & w) != 0) & ((j & w) == 0)] = l
    lvl[i == j] = N_LEVELS
    return np.tile(lvl, (HEADS, 1))


def _const_spec(shape):
    return pl.BlockSpec(shape, lambda *_: (0,) * len(shape))


def kernel(x, meta_tokens, norm_mix_w, w_in, w_gate_up, b_gate, gla_norm_w, conv_w, w_out,
           norm_mlp_w, w_up, w_down, norm_final_w):
    batch, seq, _ = x.shape
    assert seq % MIX_TILE == 0 and (batch * seq) % MLP_TILE == 0
    assert norm_mix_w.shape[0] == 1, "single layer"

    w_in0 = w_in[0]
    gr0 = 2 * QK + 2 * GW
    w_main = jnp.concatenate([w_in0[:, :gr0], w_in0[:, gr0 + RANK:]], axis=1).astype(BF16)
    w_gr = w_in0[:, gr0:gr0 + RANK]
    w_gr6 = jnp.pad(jnp.tile(w_gr, (1, GATE_TERMS)), ((0, 0), (0, LANES - GATE_TERMS * RANK))).astype(BF16)
    g_hi, g_mid, g_lo = _split3(w_gate_up[0])
    wgu6 = jnp.pad(jnp.concatenate([g_hi, g_mid, g_lo, g_hi, g_mid, g_hi], axis=0),
                   ((0, LANES - GATE_TERMS * RANK), (0, 0)))
    bg = b_gate[0].reshape(1, QK)
    nw_mix = norm_mix_w[0].reshape(1, D_MODEL)
    gnw = gla_norm_w[0].reshape(1, DV)
    cw = conv_w[0]
    wout = w_out[0].astype(BF16)
    nw_mlp = norm_mlp_w[0].reshape(1, D_MODEL)
    wup = w_up[0].astype(BF16)
    wdown = w_down[0].astype(BF16)
    fw = norm_final_w.reshape(1, D_MODEL)

    cmat = _cumsum_and_ref_matrix()
    cmat3 = jnp.asarray(np.concatenate([cmat, cmat, cmat], axis=1), BF16)
    tri_meta = jnp.asarray(_tri3(N_META), BF16)
    tri_wide = jnp.asarray(_tri3(WIDE_CHUNK), BF16)
    lvl = jnp.asarray(_level_matrix())

    st0, tail0 = pl.pallas_call(
        _meta_kernel,
        out_shape=(jax.ShapeDtypeStruct((DV, QK), F32), jax.ShapeDtypeStruct((SUBLANES, CONV_CH), F32)),
        compiler_params=pltpu.CompilerParams(vmem_limit_bytes=VMEM_LIMIT),
        name="gla_meta_state",
    )(meta_tokens, nw_mix, w_main, w_gr6, wgu6, bg, tri_meta)

    n_tiles = seq // MIX_TILE
    h1 = pl.pallas_call(
        _mixer_kernel,
        grid=(batch, n_tiles),
        in_specs=[
            pl.BlockSpec((1, MIX_TILE, D_MODEL), lambda b, t: (b, t, 0)),
            _const_spec((1, D_MODEL)),
            _const_spec(w_main.shape),
            _const_spec(w_gr6.shape),
            _const_spec(wgu6.shape),
            _const_spec((1, QK)),
            _const_spec((1, DV)),
            _const_spec(cw.shape),
            _const_spec(wout.shape),
            _const_spec(tri_wide.shape),
            _const_spec(cmat3.shape),
            _const_spec(lvl.shape),
            _const_spec((DV, QK)),
            _const_spec((SUBLANES, CONV_CH)),
        ],
        out_specs=pl.BlockSpec((1, MIX_TILE, D_MODEL), lambda b, t: (b, t, 0)),
        out_shape=jax.ShapeDtypeStruct((batch, seq, D_MODEL), F32),
        scratch_shapes=[
            pltpu.VMEM((DV, QK), F32),
            pltpu.VMEM((SUBLANES, CONV_CH), F32),
            pltpu.VMEM((MIX_TILE, QK), F32),
            pltpu.VMEM((MIX_TILE, QK), F32),
            pltpu.VMEM((MIX_TILE, GW), F32),
            pltpu.VMEM((MIX_TILE, QK), F32),
            pltpu.VMEM((MIX_TILE, QK), F32),
            pltpu.VMEM((MIX_TILE, QK), F32),
            pltpu.VMEM((MIX_TILE, QK), F32),
            pltpu.VMEM((MIX_TILE, GW), F32),
            pltpu.VMEM((MIX_TILE + SUBLANES, CONV_CH), F32),
            pltpu.VMEM((MIX_TILE, GW), F32),
            pltpu.VMEM((MIX_TILE, CONV_CH), BF16),
        ],
        compiler_params=pltpu.CompilerParams(
            vmem_limit_bytes=VMEM_LIMIT, dimension_semantics=("arbitrary", "arbitrary")),
        name="gla_conv_mixer",
    )(x, nw_mix, w_main, w_gr6, wgu6, bg, gnw, cw, wout, tri_wide, cmat3, lvl, st0, tail0)

    rows = batch * seq
    out = pl.pallas_call(
        _mlp_kernel,
        grid=(rows // MLP_TILE,),
        in_specs=[
            pl.BlockSpec((MLP_TILE, D_MODEL), lambda i: (i, 0)),
            _const_spec((1, D_MODEL)),
            _const_spec(wup.shape),
            _const_spec(wdown.shape),
            _const_spec((1, D_MODEL)),
        ],
        out_specs=pl.BlockSpec((MLP_TILE, D_MODEL), lambda i: (i, 0)),
        out_shape=jax.ShapeDtypeStruct((rows, D_MODEL), F32),
        compiler_params=pltpu.CompilerParams(
            vmem_limit_bytes=VMEM_LIMIT, dimension_semantics=("arbitrary",)),
        name="relu2_mlp_final_norm",
    )(h1.reshape(rows, D_MODEL), nw_mlp, wup, wdown, fw)
    return out.reshape(batch, seq, D_MODEL)
```

```python
import numpy as np
import jax
import jax.numpy as jnp
from jax import lax
from jax.experimental import pallas as pl
from jax.experimental.pallas import tpu as pltpu

D_MODEL = 1024
N_META = 16
HEADS = 4
DK = 64
DV = 128
QK = HEADS * DK
GW = HEADS * DV
RANK = 16
GATE_NORM = 16.0
CONV_CH = 512
D_FF = 4096
EPS = 1e-6

LANES = 128
SUBLANES = 8
VMEM_LIMIT = 56 * 1024 * 1024

MIX_TILE = 512
WIDE_CHUNK = 256
CHUNK = 64
HALF_WIDTHS = (32, 16, 8, 4, 2, 1)
N_LEVELS = len(HALF_WIDTHS)
WIDE_KEY_BOUND = 1e18
GATE_TERMS = 6
MLP_TILE = 1024
MLP_ROWS = 512
FF_CHUNK = 1024

BF16 = jnp.bfloat16
F32 = jnp.float32


def _rms(x, w):
    ms = jnp.mean(x * x, axis=-1, keepdims=True)
    return x * lax.rsqrt(ms + EPS) * w


def _log_sigmoid(z):
    return -(jnp.maximum(-z, 0.0) + jnp.log1p(jnp.exp(-jnp.abs(z))))


def _silu(g):
    return g * (0.5 * (1.0 + jnp.tanh(0.5 * g)))


def _dot(a, b):
    return jnp.dot(a, b, preferred_element_type=F32)


def _dot_nt(a, b):
    return lax.dot_general(a, b, (((1,), (1,)), ((), ())), preferred_element_type=F32)


def _split3(x):
    hi = x.astype(BF16)
    r1 = x - hi.astype(F32)
    mid = r1.astype(BF16)
    lo = (r1 - mid.astype(F32)).astype(BF16)
    return hi, mid, lo


def _stack_heads(x):
    lane_head = lax.broadcasted_iota(jnp.int32, x.shape, 1) >> (DK.bit_length() - 1)
    return jnp.concatenate([jnp.where(lane_head == h, x, 0.0) for h in range(HEADS)], axis=0)


def _rows_by_head(v):
    return jnp.concatenate([v[:, h * DV:(h + 1) * DV] for h in range(HEADS)], axis=0)


def _gate_log_decay(gr6, wgu6, bg):
    hi, mid, lo = _split3(gr6)
    group = lax.broadcasted_iota(jnp.int32, gr6.shape, 1) >> (RANK.bit_length() - 1)
    lhs = jnp.where(group < 3, hi, jnp.where(group < 5, mid, lo))
    z = _dot(lhs, wgu6) + bg
    return _log_sigmoid(z) / GATE_NORM


def _cumsum_rows(tri3, gk):
    return _dot(tri3, jnp.concatenate(_split3(gk), axis=0))


def _state_update(st, kdec, v, decay_last):
    kv = _dot(_rows_by_head(v).T.astype(BF16), _stack_heads(kdec).astype(BF16))
    return st * decay_last + kv


def _meta_kernel(meta_ref, nw_ref, wmain_ref, wgr_ref, wgu_ref, bg_ref, tri_ref, st_ref, tail_ref):
    hn = _rms(meta_ref[...], nw_ref[...]).astype(BF16)
    c0 = 2 * QK + 2 * GW
    k = _dot(hn, wmain_ref[:, QK:2 * QK])
    v = _dot(hn, wmain_ref[:, 2 * QK:2 * QK + GW])
    cc = _dot(hn, wmain_ref[:, c0 + CONV_CH:c0 + 2 * CONV_CH])
    cx = _dot(hn, wmain_ref[:, c0 + 2 * CONV_CH:c0 + 3 * CONV_CH])
    gk = _gate_log_decay(_dot(hn, wgr_ref[...]), wgu_ref[...], bg_ref[...])
    b = _cumsum_rows(tri_ref[...], gk)
    b_last = b[N_META - 1:N_META, :]
    st_ref[...] = _state_update(jnp.zeros((DV, QK), F32), k * jnp.exp(b_last - b), v, jnp.exp(b_last))
    u = cc * cx
    tail_ref[...] = u[N_META - SUBLANES:N_META, :]


def _gla_wide(q_s, v_s, b_s, qp_s, kp_s, o_s, st_ref):
    row = lax.broadcasted_iota(jnp.int32, (HEADS * WIDE_CHUNK, WIDE_CHUNK), 0) & (WIDE_CHUNK - 1)
    col = lax.broadcasted_iota(jnp.int32, (HEADS * WIDE_CHUNK, WIDE_CHUNK), 1)
    causal = row >= col
    st = st_ref[...]
    for c in range(MIX_TILE // WIDE_CHUNK):
        rows = slice(c * WIDE_CHUNK, (c + 1) * WIDE_CHUNK)
        kp = kp_s[rows, :]
        v = v_s[rows, :]
        qp_heads = _stack_heads(qp_s[rows, :]).astype(BF16)
        sc = _dot_nt(qp_heads, jnp.concatenate([kp.astype(BF16), st.astype(BF16)], axis=0))
        a = jnp.where(causal, sc[:, 0:WIDE_CHUNK], 0.0).astype(BF16)
        for h in range(HEADS):
            hr = slice(h * WIDE_CHUNK, (h + 1) * WIDE_CHUNK)
            o_h = _dot(a[hr], v[:, h * DV:(h + 1) * DV].astype(BF16))
            o_s[rows, h * DV:(h + 1) * DV] = o_h + sc[hr, WIDE_CHUNK:WIDE_CHUNK + DV]
        decay_last = jnp.exp(b_s[(c + 1) * WIDE_CHUNK - 1:(c + 1) * WIDE_CHUNK, :])
        st = _state_update(st, kp * decay_last, v, decay_last)
    st_ref[...] = st


def _gla_stable(q_s, k_s, v_s, gk_s, o_s, st_ref, cmat_ref, lvl_ref):
    lvl = lvl_ref[...]

    def chunk_body(c, st):
        r0 = pl.multiple_of(c * CHUNK, CHUNK)
        q = q_s[pl.ds(r0, CHUNK), :]
        k = k_s[pl.ds(r0, CHUNK), :]
        v = v_s[pl.ds(r0, CHUNK), :]
        gk = gk_s[pl.ds(r0, CHUNK), :]
        br = _cumsum_rows(cmat_ref[...], gk)
        b = br[0:CHUNK]
        row = lax.broadcasted_iota(jnp.int32, (CHUNK, QK), 0)
        a = jnp.zeros((HEADS * CHUNK, CHUNK), F32)
        for l, w in enumerate(HALF_WIDTHS):
            ref = br[(l + 1) * CHUNK:(l + 2) * CHUNK]
            e = jnp.exp(-jnp.abs(b - ref))
            second = (row & w) != 0
            ql = jnp.where(second, q * e, 0.0)
            kl = jnp.where(second, 0.0, k * e)
            g = _dot_nt(_stack_heads(ql).astype(BF16), kl.astype(BF16))
            a = jnp.where(lvl == l, g, a)
        g = _dot_nt(_stack_heads(q).astype(BF16), k.astype(BF16))
        a = jnp.where(lvl == N_LEVELS, g, a)
        a = a.astype(BF16)
        o_inter = _dot_nt(_stack_heads(q * jnp.exp(b)).astype(BF16), st.astype(BF16))
        for h in range(HEADS):
            o_h = _dot(a[h * CHUNK:(h + 1) * CHUNK], v[:, h * DV:(h + 1) * DV].astype(BF16))
            o_s[pl.ds(r0, CHUNK), h * DV:(h + 1) * DV] = o_h + o_inter[h * CHUNK:(h + 1) * CHUNK]
        b_last = b[CHUNK - 1:CHUNK, :]
        return _state_update(st, k * jnp.exp(b_last - b), v, jnp.exp(b_last))

    st_ref[...] = lax.fori_loop(0, MIX_TILE // CHUNK, chunk_body, st_ref[...])


def _mixer_kernel(x_ref, xnext_ref, nw_ref, wmain_ref, wgr_ref, wgu_ref, bg_ref, gnw_ref, cw_ref, wout_ref,
                  tri_ref, cmat_ref, lvl_ref, st0_ref, tail0_ref, h1_ref,
                  st_ref, tail_ref, hn_s, hnext_s, grnext_s, q_s, k_s, v_s, gk_s, b_s, qp_s, kp_s, o_s, u_s,
                  gate_s, yconv_s):
    t = pl.program_id(1)

    @pl.when(t == 0)
    def _():
        st_ref[...] = st0_ref[...]
        tail_ref[...] = tail0_ref[...]

    @pl.when(pl.program_id(0) + t == 0)
    def _():
        hnext_s[...] = _rms(x_ref[0], nw_ref[...]).astype(BF16)
        grnext_s[...] = _dot(hnext_s[...], wgr_ref[...])

    hn_s[...] = hnext_s[...]
    gk = _gate_log_decay(grnext_s[...], wgu_ref[...], bg_ref[...])
    qk = _dot(hn_s[...], wmain_ref[:, 0:2 * QK])
    q = qk[:, 0:QK] * (DK ** -0.5)
    k = qk[:, QK:2 * QK]
    q_s[...] = q
    k_s[...] = k
    gk_s[...] = gk
    n_wide = MIX_TILE // WIDE_CHUNK
    gk_cols = jnp.concatenate([gk[c * WIDE_CHUNK:(c + 1) * WIDE_CHUNK] for c in range(n_wide)], axis=1)
    b_cols = _cumsum_rows(tri_ref[...], gk_cols)
    b = jnp.concatenate([b_cols[:, c * QK:(c + 1) * QK] for c in range(n_wide)], axis=0)
    kp = k * jnp.exp(-b)
    b_s[...] = b
    qp_s[...] = q * jnp.exp(b)
    kp_s[...] = kp
    n_bad = jnp.sum(jnp.where(jnp.abs(kp) <= WIDE_KEY_BOUND, 0.0, 1.0))
    wide_ok = n_bad == 0.0

    c0 = 2 * QK + 2 * GW
    cc = _dot(hn_s[...], wmain_ref[:, c0 + CONV_CH:c0 + 2 * CONV_CH])
    cx = _dot(hn_s[...], wmain_ref[:, c0 + 2 * CONV_CH:c0 + 3 * CONV_CH])
    hnext = _rms(xnext_ref[0], nw_ref[...]).astype(BF16)
    hnext_s[...] = hnext
    grnext_s[...] = _dot(hnext, wgr_ref[...])
    cb = _dot(hn_s[...], wmain_ref[:, c0:c0 + CONV_CH])
    u = cc * cx
    u_s[0:SUBLANES, :] = tail_ref[...]
    u_s[SUBLANES:SUBLANES + MIX_TILE, :] = u
    tail_ref[...] = u[MIX_TILE - SUBLANES:MIX_TILE, :]
    u1 = u_s[SUBLANES - 1:SUBLANES - 1 + MIX_TILE, :]
    u2 = u_s[SUBLANES - 2:SUBLANES - 2 + MIX_TILE, :]
    yconv_s[...] = (cb * (cw_ref[0:1, :] * u2 + cw_ref[1:2, :] * u1 + cw_ref[2:3, :] * u)).astype(BF16)
    gate_s[...] = _silu(_dot(hn_s[...], wmain_ref[:, 2 * QK + GW:2 * QK + 2 * GW]))
    v_s[...] = _dot(hn_s[...], wmain_ref[:, 2 * QK:2 * QK + GW])

    @pl.when(wide_ok)
    def _():
        _gla_wide(q_s, v_s, b_s, qp_s, kp_s, o_s, st_ref)

    @pl.when(jnp.logical_not(wide_ok))
    def _():
        _gla_stable(q_s, k_s, v_s, gk_s, o_s, st_ref, cmat_ref, lvl_ref)

    gnw = gnw_ref[...]
    y_gla = jnp.concatenate(
        [_rms(o_s[:, h * DV:(h + 1) * DV], gnw) for h in range(HEADS)], axis=1) * gate_s[...]
    mixed = (_dot(yconv_s[...], wout_ref[GW:GW + CONV_CH, :])
             + _dot(y_gla.astype(BF16), wout_ref[0:GW, :]))
    h1_ref[0] = x_ref[0] + mixed


def _mlp_kernel(h_ref, nw_ref, wup_ref, wdown_ref, fw_ref, out_ref):
    blocks = [slice(r, r + MLP_ROWS) for r in range(0, MLP_TILE, MLP_ROWS)]
    hn = [_rms(h_ref[rows, :], nw_ref[...]).astype(BF16) for rows in blocks]
    acc = [jnp.zeros((MLP_ROWS, D_MODEL), F32) for _ in blocks]
    for f in range(0, D_FF, FF_CHUNK):
        for i in range(len(blocks)):
            up = _dot(hn[i], wup_ref[:, f:f + FF_CHUNK])
            act = jnp.square(jnp.maximum(up, 0.0)).astype(BF16)
            acc[i] = acc[i] + _dot(act, wdown_ref[f:f + FF_CHUNK, :])
    for i, rows in enumerate(blocks):
        out_ref[rows, :] = _rms(h_ref[rows, :] + acc[i], fw_ref[...])


def _tri3(n):
    tri = np.tril(np.ones((n, n), np.float32))
    return np.concatenate([tri, tri, tri], axis=1)


def _cumsum_and_ref_matrix():
    tri = np.tril(np.ones((CHUNK, CHUNK), np.float32))
    blocks = [tri]
    idx = np.arange(CHUNK)
    for w in HALF_WIDTHS:
        ref_row = (idx // (2 * w)) * (2 * w) + w - 1
        blocks.append(tri[ref_row])
    return np.concatenate(blocks, axis=0)


def _level_matrix():
    i = np.arange(CHUNK)[:, None]
    j = np.arange(CHUNK)[None, :]
    lvl = np.full((CHUNK, CHUNK), -1, np.int32)
    for l, w in enumerate(HALF_WIDTHS):
        same = (i // (2 * w)) == (j // (2 * w))
        lvl[same & ((i & w) != 0) & ((j & w) == 0)] = l
    lvl[i == j] = N_LEVELS
    return np.tile(lvl, (HEADS, 1))


def _const_spec(shape):
    return pl.BlockSpec(shape, lambda *_: (0,) * len(shape))


def kernel(x, meta_tokens, norm_mix_w, w_in, w_gate_up, b_gate, gla_norm_w, conv_w, w_out,
           norm_mlp_w, w_up, w_down, norm_final_w):
    batch, seq, _ = x.shape
    assert seq % MIX_TILE == 0 and (batch * seq) % MLP_TILE == 0
    assert norm_mix_w.shape[0] == 1, "single layer"

    w_in0 = w_in[0]
    gr0 = 2 * QK + 2 * GW
    w_main = jnp.concatenate([w_in0[:, :gr0], w_in0[:, gr0 + RANK:]], axis=1).astype(BF16)
    w_gr = w_in0[:, gr0:gr0 + RANK]
    w_gr6 = jnp.pad(jnp.tile(w_gr, (1, GATE_TERMS)), ((0, 0), (0, LANES - GATE_TERMS * RANK))).astype(BF16)
    g_hi, g_mid, g_lo = _split3(w_gate_up[0])
    wgu6 = jnp.pad(jnp.concatenate([g_hi, g_mid, g_lo, g_hi, g_mid, g_hi], axis=0),
                   ((0, LANES - GATE_TERMS * RANK), (0, 0)))
    bg = b_gate[0].reshape(1, QK)
    nw_mix = norm_mix_w[0].reshape(1, D_MODEL)
    gnw = gla_norm_w[0].reshape(1, DV)
    cw = conv_w[0]
    wout = w_out[0].astype(BF16)
    nw_mlp = norm_mlp_w[0].reshape(1, D_MODEL)
    wup = w_up[0].astype(BF16)
    wdown = w_down[0].astype(BF16)
    fw = norm_final_w.reshape(1, D_MODEL)

    cmat = _cumsum_and_ref_matrix()
    cmat3 = jnp.asarray(np.concatenate([cmat, cmat, cmat], axis=1), BF16)
    tri_meta = jnp.asarray(_tri3(N_META), BF16)
    tri_wide = jnp.asarray(_tri3(WIDE_CHUNK), BF16)
    lvl = jnp.asarray(_level_matrix())

    st0, tail0 = pl.pallas_call(
        _meta_kernel,
        out_shape=(jax.ShapeDtypeStruct((DV, QK), F32), jax.ShapeDtypeStruct((SUBLANES, CONV_CH), F32)),
        compiler_params=pltpu.CompilerParams(vmem_limit_bytes=VMEM_LIMIT),
        name="gla_meta_state",
    )(meta_tokens, nw_mix, w_main, w_gr6, wgu6, bg, tri_meta)

    n_tiles = seq // MIX_TILE

    def next_tile(b, t):
        nxt = jnp.minimum(b * n_tiles + t + 1, batch * n_tiles - 1)
        return nxt // n_tiles, nxt % n_tiles, 0

    h1 = pl.pallas_call(
        _mixer_kernel,
        grid=(batch, n_tiles),
        in_specs=[
            pl.BlockSpec((1, MIX_TILE, D_MODEL), lambda b, t: (b, t, 0)),
            pl.BlockSpec((1, MIX_TILE, D_MODEL), next_tile),
            _const_spec((1, D_MODEL)),
            _const_spec(w_main.shape),
            _const_spec(w_gr6.shape),
            _const_spec(wgu6.shape),
            _const_spec((1, QK)),
            _const_spec((1, DV)),
            _const_spec(cw.shape),
            _const_spec(wout.shape),
            _const_spec(tri_wide.shape),
            _const_spec(cmat3.shape),
            _const_spec(lvl.shape),
            _const_spec((DV, QK)),
            _const_spec((SUBLANES, CONV_CH)),
        ],
        out_specs=pl.BlockSpec((1, MIX_TILE, D_MODEL), lambda b, t: (b, t, 0)),
        out_shape=jax.ShapeDtypeStruct((batch, seq, D_MODEL), F32),
        scratch_shapes=[
            pltpu.VMEM((DV, QK), F32),
            pltpu.VMEM((SUBLANES, CONV_CH), F32),
            pltpu.VMEM((MIX_TILE, D_MODEL), BF16),
            pltpu.VMEM((MIX_TILE, D_MODEL), BF16),
            pltpu.VMEM((MIX_TILE, LANES), F32),
            pltpu.VMEM((MIX_TILE, QK), F32),
            pltpu.VMEM((MIX_TILE, QK), F32),
            pltpu.VMEM((MIX_TILE, GW), F32),
            pltpu.VMEM((MIX_TILE, QK), F32),
            pltpu.VMEM((MIX_TILE, QK), F32),
            pltpu.VMEM((MIX_TILE, QK), F32),
            pltpu.VMEM((MIX_TILE, QK), F32),
            pltpu.VMEM((MIX_TILE, GW), F32),
            pltpu.VMEM((MIX_TILE + SUBLANES, CONV_CH), F32),
            pltpu.VMEM((MIX_TILE, GW), F32),
            pltpu.VMEM((MIX_TILE, CONV_CH), BF16),
        ],
        compiler_params=pltpu.CompilerParams(
            vmem_limit_bytes=VMEM_LIMIT, dimension_semantics=("arbitrary", "arbitrary")),
        name="gla_conv_mixer",
    )(x, x, nw_mix, w_main, w_gr6, wgu6, bg, gnw, cw, wout, tri_wide, cmat3, lvl, st0, tail0)

    rows = batch * seq
    out = pl.pallas_call(
        _mlp_kernel,
        grid=(rows // MLP_TILE,),
        in_specs=[
            pl.BlockSpec((MLP_TILE, D_MODEL), lambda i: (i, 0)),
            _const_spec((1, D_MODEL)),
            _const_spec(wup.shape),
            _const_spec(wdown.shape),
            _const_spec((1, D_MODEL)),
        ],
        out_specs=pl.BlockSpec((MLP_TILE, D_MODEL), lambda i: (i, 0)),
        out_shape=jax.ShapeDtypeStruct((rows, D_MODEL), F32),
        compiler_params=pltpu.CompilerParams(
            vmem_limit_bytes=VMEM_LIMIT, dimension_semantics=("arbitrary",)),
        name="relu2_mlp_final_norm",
    )(h1.reshape(rows, D_MODEL), nw_mlp, wup, wdown, fw)
    return out.reshape(batch, seq, D_MODEL)
```

```python
import numpy as np
import jax
import jax.numpy as jnp
from jax import lax
from jax.experimental import pallas as pl
from jax.experimental.pallas import tpu as pltpu

D_MODEL = 1024
N_META = 16
HEADS = 4
DK = 64
DV = 128
QK = HEADS * DK
GW = HEADS * DV
RANK = 16
GATE_NORM = 16.0
CONV_CH = 512
D_FF = 4096
EPS = 1e-6

LANES = 128
SUBLANES = 8
VMEM_LIMIT = 56 * 1024 * 1024

MIX_TILE = 512
WIDE_CHUNK = 256
CHUNK = 64
HALF_WIDTHS = (32, 16, 8, 4, 2, 1)
N_LEVELS = len(HALF_WIDTHS)
WIDE_KEY_BOUND = 1e18
GATE_TERMS = 6
MLP_TILE = 1024
MLP_ROWS = 512
FF_CHUNK = 1024

BF16 = jnp.bfloat16
F32 = jnp.float32


def _rms(x, w):
    ms = jnp.mean(x * x, axis=-1, keepdims=True)
    return x * lax.rsqrt(ms + EPS) * w


def _log_sigmoid(z):
    return -(jnp.maximum(-z, 0.0) + jnp.log1p(jnp.exp(-jnp.abs(z))))


def _silu(g):
    return g * (0.5 * (1.0 + jnp.tanh(0.5 * g)))


def _dot(a, b):
    return jnp.dot(a, b, preferred_element_type=F32)


def _dot_nt(a, b):
    return lax.dot_general(a, b, (((1,), (1,)), ((), ())), preferred_element_type=F32)


def _split3(x):
    hi = x.astype(BF16)
    r1 = x - hi.astype(F32)
    mid = r1.astype(BF16)
    lo = (r1 - mid.astype(F32)).astype(BF16)
    return hi, mid, lo


def _stack_heads(x):
    lane_head = lax.broadcasted_iota(jnp.int32, x.shape, 1) >> (DK.bit_length() - 1)
    return jnp.concatenate([jnp.where(lane_head == h, x, 0.0) for h in range(HEADS)], axis=0)


def _rows_by_head(v):
    return jnp.concatenate([v[:, h * DV:(h + 1) * DV] for h in range(HEADS)], axis=0)


def _gate_log_decay(gr6, wgu6, bg):
    hi, mid, lo = _split3(gr6)
    group = lax.broadcasted_iota(jnp.int32, gr6.shape, 1) >> (RANK.bit_length() - 1)
    lhs = jnp.where(group < 3, hi, jnp.where(group < 5, mid, lo))
    z = _dot(lhs, wgu6) + bg
    return _log_sigmoid(z) / GATE_NORM


def _cumsum_rows(tri3, gk):
    return _dot(tri3, jnp.concatenate(_split3(gk), axis=0))


def _state_update(st, kdec, v, decay_last):
    kv = _dot(_rows_by_head(v).T.astype(BF16), _stack_heads(kdec).astype(BF16))
    return st * decay_last + kv


def _meta_kernel(meta_ref, nw_ref, wmain_ref, wgr_ref, wgu_ref, bg_ref, tri_ref, st_ref, tail_ref):
    hn = _rms(meta_ref[...], nw_ref[...]).astype(BF16)
    c0 = 2 * QK + 2 * GW
    k = _dot(hn, wmain_ref[:, QK:2 * QK])
    v = _dot(hn, wmain_ref[:, 2 * QK:2 * QK + GW])
    cc = _dot(hn, wmain_ref[:, c0 + CONV_CH:c0 + 2 * CONV_CH])
    cx = _dot(hn, wmain_ref[:, c0 + 2 * CONV_CH:c0 + 3 * CONV_CH])
    gk = _gate_log_decay(_dot(hn, wgr_ref[...]), wgu_ref[...], bg_ref[...])
    b = _cumsum_rows(tri_ref[...], gk)
    b_last = b[N_META - 1:N_META, :]
    st_ref[...] = _state_update(jnp.zeros((DV, QK), F32), k * jnp.exp(b_last - b), v, jnp.exp(b_last))
    u = cc * cx
    tail_ref[...] = u[N_META - SUBLANES:N_META, :]


def _gla_wide(q_s, v_s, b_s, qp_s, kp_s, o_s, st_ref):
    row = lax.broadcasted_iota(jnp.int32, (HEADS * WIDE_CHUNK, WIDE_CHUNK), 0) & (WIDE_CHUNK - 1)
    col = lax.broadcasted_iota(jnp.int32, (HEADS * WIDE_CHUNK, WIDE_CHUNK), 1)
    causal = row >= col
    st = st_ref[...]
    for c in range(MIX_TILE // WIDE_CHUNK):
        rows = slice(c * WIDE_CHUNK, (c + 1) * WIDE_CHUNK)
        kp = kp_s[rows, :]
        v = v_s[rows, :]
        qp_heads = _stack_heads(qp_s[rows, :]).astype(BF16)
        sc = _dot_nt(qp_heads, jnp.concatenate([kp.astype(BF16), st.astype(BF16)], axis=0))
        a = jnp.where(causal, sc[:, 0:WIDE_CHUNK], 0.0).astype(BF16)
        for h in range(HEADS):
            hr = slice(h * WIDE_CHUNK, (h + 1) * WIDE_CHUNK)
            o_h = _dot(a[hr], v[:, h * DV:(h + 1) * DV].astype(BF16))
            o_s[rows, h * DV:(h + 1) * DV] = o_h + sc[hr, WIDE_CHUNK:WIDE_CHUNK + DV]
        decay_last = jnp.exp(b_s[(c + 1) * WIDE_CHUNK - 1:(c + 1) * WIDE_CHUNK, :])
        st = _state_update(st, kp * decay_last, v, decay_last)
    st_ref[...] = st


def _gla_stable(q_s, k_s, v_s, gk_s, o_s, st_ref, cmat_ref, lvl_ref):
    lvl = lvl_ref[...]

    def chunk_body(c, st):
        r0 = pl.multiple_of(c * CHUNK, CHUNK)
        q = q_s[pl.ds(r0, CHUNK), :]
        k = k_s[pl.ds(r0, CHUNK), :]
        v = v_s[pl.ds(r0, CHUNK), :]
        gk = gk_s[pl.ds(r0, CHUNK), :]
        br = _cumsum_rows(cmat_ref[...], gk)
        b = br[0:CHUNK]
        row = lax.broadcasted_iota(jnp.int32, (CHUNK, QK), 0)
        a = jnp.zeros((HEADS * CHUNK, CHUNK), F32)
        for l, w in enumerate(HALF_WIDTHS):
            ref = br[(l + 1) * CHUNK:(l + 2) * CHUNK]
            e = jnp.exp(-jnp.abs(b - ref))
            second = (row & w) != 0
            ql = jnp.where(second, q * e, 0.0)
            kl = jnp.where(second, 0.0, k * e)
            g = _dot_nt(_stack_heads(ql).astype(BF16), kl.astype(BF16))
            a = jnp.where(lvl == l, g, a)
        g = _dot_nt(_stack_heads(q).astype(BF16), k.astype(BF16))
        a = jnp.where(lvl == N_LEVELS, g, a)
        a = a.astype(BF16)
        o_inter = _dot_nt(_stack_heads(q * jnp.exp(b)).astype(BF16), st.astype(BF16))
        for h in range(HEADS):
            o_h = _dot(a[h * CHUNK:(h + 1) * CHUNK], v[:, h * DV:(h + 1) * DV].astype(BF16))
            o_s[pl.ds(r0, CHUNK), h * DV:(h + 1) * DV] = o_h + o_inter[h * CHUNK:(h + 1) * CHUNK]
        b_last = b[CHUNK - 1:CHUNK, :]
        return _state_update(st, k * jnp.exp(b_last - b), v, jnp.exp(b_last))

    st_ref[...] = lax.fori_loop(0, MIX_TILE // CHUNK, chunk_body, st_ref[...])


def _mixer_kernel(x_ref, xnext_ref, nw_ref, wmain_ref, wgr_ref, wgu_ref, bg_ref, gnw_ref, cw_ref, wout_ref,
                  tri_ref, cmat_ref, lvl_ref, st0_ref, tail0_ref, wup32_ref, wdown32_ref,
                  h1_ref, wup16_ref, wdown16_ref,
                  st_ref, tail_ref, hn_s, hnext_s, grnext_s, q_s, k_s, v_s, gk_s, b_s, qp_s, kp_s, o_s, u_s,
                  gate_s, yconv_s):
    t = pl.program_id(1)

    @pl.when(t == 0)
    def _():
        st_ref[...] = st0_ref[...]
        tail_ref[...] = tail0_ref[...]

    @pl.when(pl.program_id(0) + t == 0)
    def _():
        hnext_s[...] = _rms(x_ref[0], nw_ref[...]).astype(BF16)
        grnext_s[...] = _dot(hnext_s[...], wgr_ref[...])

    wup16_ref[...] = wup32_ref[...].astype(BF16)
    wdown16_ref[...] = wdown32_ref[...].astype(BF16)

    hn_s[...] = hnext_s[...]
    gk = _gate_log_decay(grnext_s[...], wgu_ref[...], bg_ref[...])
    qk = _dot(hn_s[...], wmain_ref[:, 0:2 * QK])
    q = qk[:, 0:QK] * (DK ** -0.5)
    k = qk[:, QK:2 * QK]
    q_s[...] = q
    k_s[...] = k
    gk_s[...] = gk
    n_wide = MIX_TILE // WIDE_CHUNK
    gk_cols = jnp.concatenate([gk[c * WIDE_CHUNK:(c + 1) * WIDE_CHUNK] for c in range(n_wide)], axis=1)
    b_cols = _cumsum_rows(tri_ref[...], gk_cols)
    b = jnp.concatenate([b_cols[:, c * QK:(c + 1) * QK] for c in range(n_wide)], axis=0)
    kp = k * jnp.exp(-b)
    b_s[...] = b
    qp_s[...] = q * jnp.exp(b)
    kp_s[...] = kp
    n_bad = jnp.sum(jnp.where(jnp.abs(kp) <= WIDE_KEY_BOUND, 0.0, 1.0))
    wide_ok = n_bad == 0.0

    c0 = 2 * QK + 2 * GW
    cc = _dot(hn_s[...], wmain_ref[:, c0 + CONV_CH:c0 + 2 * CONV_CH])
    cx = _dot(hn_s[...], wmain_ref[:, c0 + 2 * CONV_CH:c0 + 3 * CONV_CH])
    hnext = _rms(xnext_ref[0], nw_ref[...]).astype(BF16)
    hnext_s[...] = hnext
    grnext_s[...] = _dot(hnext, wgr_ref[...])
    cb = _dot(hn_s[...], wmain_ref[:, c0:c0 + CONV_CH])
    u = cc * cx
    u_s[0:SUBLANES, :] = tail_ref[...]
    u_s[SUBLANES:SUBLANES + MIX_TILE, :] = u
    tail_ref[...] = u[MIX_TILE - SUBLANES:MIX_TILE, :]
    u1 = u_s[SUBLANES - 1:SUBLANES - 1 + MIX_TILE, :]
    u2 = u_s[SUBLANES - 2:SUBLANES - 2 + MIX_TILE, :]
    yconv_s[...] = (cb * (cw_ref[0:1, :] * u2 + cw_ref[1:2, :] * u1 + cw_ref[2:3, :] * u)).astype(BF16)
    gate_s[...] = _silu(_dot(hn_s[...], wmain_ref[:, 2 * QK + GW:2 * QK + 2 * GW]))
    v_s[...] = _dot(hn_s[...], wmain_ref[:, 2 * QK:2 * QK + GW])

    @pl.when(wide_ok)
    def _():
        _gla_wide(q_s, v_s, b_s, qp_s, kp_s, o_s, st_ref)

    @pl.when(jnp.logical_not(wide_ok))
    def _():
        _gla_stable(q_s, k_s, v_s, gk_s, o_s, st_ref, cmat_ref, lvl_ref)

    gnw = gnw_ref[...]
    y_gla = jnp.concatenate(
        [_rms(o_s[:, h * DV:(h + 1) * DV], gnw) for h in range(HEADS)], axis=1) * gate_s[...]
    mixed = (_dot(yconv_s[...], wout_ref[GW:GW + CONV_CH, :])
             + _dot(y_gla.astype(BF16), wout_ref[0:GW, :]))
    h1_ref[0] = x_ref[0] + mixed


def _mlp_kernel(h_ref, nw_ref, wup_ref, wdown_ref, fw_ref, out_ref):
    blocks = [slice(r, r + MLP_ROWS) for r in range(0, MLP_TILE, MLP_ROWS)]
    hn = [_rms(h_ref[rows, :], nw_ref[...]).astype(BF16) for rows in blocks]
    acc = [jnp.zeros((MLP_ROWS, D_MODEL), F32) for _ in blocks]
    for f in range(0, D_FF, FF_CHUNK):
        for i in range(len(blocks)):
            up = _dot(hn[i], wup_ref[:, f:f + FF_CHUNK])
            act = jnp.square(jnp.maximum(up, 0.0)).astype(BF16)
            acc[i] = acc[i] + _dot(act, wdown_ref[f:f + FF_CHUNK, :])
    for i, rows in enumerate(blocks):
        out_ref[rows, :] = _rms(h_ref[rows, :] + acc[i], fw_ref[...])


def _tri3(n):
    tri = np.tril(np.ones((n, n), np.float32))
    return np.concatenate([tri, tri, tri], axis=1)


def _cumsum_and_ref_matrix():
    tri = np.tril(np.ones((CHUNK, CHUNK), np.float32))
    blocks = [tri]
    idx = np.arange(CHUNK)
    for w in HALF_WIDTHS:
        ref_row = (idx // (2 * w)) * (2 * w) + w - 1
        blocks.append(tri[ref_row])
    return np.concatenate(blocks, axis=0)


def _level_matrix():
    i = np.arange(CHUNK)[:, None]
    j = np.arange(CHUNK)[None, :]
    lvl = np.full((CHUNK, CHUNK), -1, np.int32)
    for l, w in enumerate(HALF_WIDTHS):
        same = (i // (2 * w)) == (j // (2 * w))
        lvl[same & ((i & w) != 0) & ((j & w) == 0)] = l
    lvl[i == j] = N_LEVELS
    return np.tile(lvl, (HEADS, 1))


def _const_spec(shape):
    return pl.BlockSpec(shape, lambda *_: (0,) * len(shape))


def kernel(x, meta_tokens, norm_mix_w, w_in, w_gate_up, b_gate, gla_norm_w, conv_w, w_out,
           norm_mlp_w, w_up, w_down, norm_final_w):
    batch, seq, _ = x.shape
    assert seq % MIX_TILE == 0 and (batch * seq) % MLP_TILE == 0
    assert norm_mix_w.shape[0] == 1, "single layer"

    w_in0 = w_in[0]
    gr0 = 2 * QK + 2 * GW
    w_main = jnp.concatenate([w_in0[:, :gr0], w_in0[:, gr0 + RANK:]], axis=1).astype(BF16)
    w_gr = w_in0[:, gr0:gr0 + RANK]
    w_gr6 = jnp.pad(jnp.tile(w_gr, (1, GATE_TERMS)), ((0, 0), (0, LANES - GATE_TERMS * RANK))).astype(BF16)
    g_hi, g_mid, g_lo = _split3(w_gate_up[0])
    wgu6 = jnp.pad(jnp.concatenate([g_hi, g_mid, g_lo, g_hi, g_mid, g_hi], axis=0),
                   ((0, LANES - GATE_TERMS * RANK), (0, 0)))
    bg = b_gate[0].reshape(1, QK)
    nw_mix = norm_mix_w[0].reshape(1, D_MODEL)
    gnw = gla_norm_w[0].reshape(1, DV)
    cw = conv_w[0]
    wout = w_out[0].astype(BF16)
    nw_mlp = norm_mlp_w[0].reshape(1, D_MODEL)
    fw = norm_final_w.reshape(1, D_MODEL)

    cmat = _cumsum_and_ref_matrix()
    cmat3 = jnp.asarray(np.concatenate([cmat, cmat, cmat], axis=1), BF16)
    tri_meta = jnp.asarray(_tri3(N_META), BF16)
    tri_wide = jnp.asarray(_tri3(WIDE_CHUNK), BF16)
    lvl = jnp.asarray(_level_matrix())

    st0, tail0 = pl.pallas_call(
        _meta_kernel,
        out_shape=(jax.ShapeDtypeStruct((DV, QK), F32), jax.ShapeDtypeStruct((SUBLANES, CONV_CH), F32)),
        compiler_params=pltpu.CompilerParams(vmem_limit_bytes=VMEM_LIMIT),
        name="gla_meta_state",
    )(meta_tokens, nw_mix, w_main, w_gr6, wgu6, bg, tri_meta)

    n_tiles = seq // MIX_TILE

    def next_tile(b, t):
        nxt = jnp.minimum(b * n_tiles + t + 1, batch * n_tiles - 1)
        return nxt // n_tiles, nxt % n_tiles, 0

    n_steps = batch * n_tiles
    assert D_MODEL % n_steps == 0 and D_FF % n_steps == 0
    up_slab, down_slab = D_MODEL // n_steps, D_FF // n_steps

    h1, wup, wdown = pl.pallas_call(
        _mixer_kernel,
        grid=(batch, n_tiles),
        in_specs=[
            pl.BlockSpec((1, MIX_TILE, D_MODEL), lambda b, t: (b, t, 0)),
            pl.BlockSpec((1, MIX_TILE, D_MODEL), next_tile),
            _const_spec((1, D_MODEL)),
            _const_spec(w_main.shape),
            _const_spec(w_gr6.shape),
            _const_spec(wgu6.shape),
            _const_spec((1, QK)),
            _const_spec((1, DV)),
            _const_spec(cw.shape),
            _const_spec(wout.shape),
            _const_spec(tri_wide.shape),
            _const_spec(cmat3.shape),
            _const_spec(lvl.shape),
            _const_spec((DV, QK)),
            _const_spec((SUBLANES, CONV_CH)),
            pl.BlockSpec((up_slab, D_FF), lambda b, t: (b * n_tiles + t, 0)),
            pl.BlockSpec((down_slab, D_MODEL), lambda b, t: (b * n_tiles + t, 0)),
        ],
        out_specs=(
            pl.BlockSpec((1, MIX_TILE, D_MODEL), lambda b, t: (b, t, 0)),
            pl.BlockSpec((up_slab, D_FF), lambda b, t: (b * n_tiles + t, 0)),
            pl.BlockSpec((down_slab, D_MODEL), lambda b, t: (b * n_tiles + t, 0)),
        ),
        out_shape=(
            jax.ShapeDtypeStruct((batch, seq, D_MODEL), F32),
            jax.ShapeDtypeStruct((D_MODEL, D_FF), BF16),
            jax.ShapeDtypeStruct((D_FF, D_MODEL), BF16),
        ),
        scratch_shapes=[
            pltpu.VMEM((DV, QK), F32),
            pltpu.VMEM((SUBLANES, CONV_CH), F32),
            pltpu.VMEM((MIX_TILE, D_MODEL), BF16),
            pltpu.VMEM((MIX_TILE, D_MODEL), BF16),
            pltpu.VMEM((MIX_TILE, LANES), F32),
            pltpu.VMEM((MIX_TILE, QK), F32),
            pltpu.VMEM((MIX_TILE, QK), F32),
            pltpu.VMEM((MIX_TILE, GW), F32),
            pltpu.VMEM((MIX_TILE, QK), F32),
            pltpu.VMEM((MIX_TILE, QK), F32),
            pltpu.VMEM((MIX_TILE, QK), F32),
            pltpu.VMEM((MIX_TILE, QK), F32),
            pltpu.VMEM((MIX_TILE, GW), F32),
            pltpu.VMEM((MIX_TILE + SUBLANES, CONV_CH), F32),
            pltpu.VMEM((MIX_TILE, GW), F32),
            pltpu.VMEM((MIX_TILE, CONV_CH), BF16),
        ],
        compiler_params=pltpu.CompilerParams(
            vmem_limit_bytes=VMEM_LIMIT, dimension_semantics=("arbitrary", "arbitrary")),
        name="gla_conv_mixer",
    )(x, x, nw_mix, w_main, w_gr6, wgu6, bg, gnw, cw, wout, tri_wide, cmat3, lvl, st0, tail0,
      w_up[0], w_down[0])

    rows = batch * seq
    out = pl.pallas_call(
        _mlp_kernel,
        grid=(rows // MLP_TILE,),
        in_specs=[
            pl.BlockSpec((MLP_TILE, D_MODEL), lambda i: (i, 0)),
            _const_spec((1, D_MODEL)),
            _const_spec(wup.shape),
            _const_spec(wdown.shape),
            _const_spec((1, D_MODEL)),
        ],
        out_specs=pl.BlockSpec((MLP_TILE, D_MODEL), lambda i: (i, 0)),
        out_shape=jax.ShapeDtypeStruct((rows, D_MODEL), F32),
        compiler_params=pltpu.CompilerParams(
            vmem_limit_bytes=VMEM_LIMIT, dimension_semantics=("arbitrary",)),
        name="relu2_mlp_final_norm",
    )(h1.reshape(rows, D_MODEL), nw_mlp, wup, wdown, fw)
    return out.reshape(batch, seq, D_MODEL)
```

```python
import numpy as np
import jax
import jax.numpy as jnp
from jax import lax
from jax.experimental import pallas as pl
from jax.experimental.pallas import tpu as pltpu

D_MODEL = 1024
N_META = 16
HEADS = 4
DK = 64
DV = 128
QK = HEADS * DK
GW = HEADS * DV
RANK = 16
GATE_NORM = 16.0
CONV_CH = 512
D_FF = 4096
PROJ_WIDTH = 2 * QK + 2 * GW + RANK + 3 * CONV_CH
EPS = 1e-6

LANES = 128
SUBLANES = 8
VMEM_LIMIT = 56 * 1024 * 1024

MIX_TILE = 512
WIDE_CHUNK = 256
CHUNK = 64
HALF_WIDTHS = (32, 16, 8, 4, 2, 1)
N_LEVELS = len(HALF_WIDTHS)
WIDE_KEY_BOUND = 1e18
GATE_TERMS = 6
PREP_SLAB = 128
MLP_TILE = 1024
MLP_ROWS = 512
FF_CHUNK = 1024

BF16 = jnp.bfloat16
F32 = jnp.float32


def _rms(x, w):
    ms = jnp.mean(x * x, axis=-1, keepdims=True)
    return x * lax.rsqrt(ms + EPS) * w


def _log_sigmoid(z):
    return -(jnp.maximum(-z, 0.0) + jnp.log1p(jnp.exp(-jnp.abs(z))))


def _silu(g):
    return g * (0.5 * (1.0 + jnp.tanh(0.5 * g)))


def _dot(a, b):
    return jnp.dot(a, b, preferred_element_type=F32)


def _dot_nt(a, b):
    return lax.dot_general(a, b, (((1,), (1,)), ((), ())), preferred_element_type=F32)


def _split3(x):
    hi = x.astype(BF16)
    r1 = x - hi.astype(F32)
    mid = r1.astype(BF16)
    lo = (r1 - mid.astype(F32)).astype(BF16)
    return hi, mid, lo


def _stack_heads(x):
    lane_head = lax.broadcasted_iota(jnp.int32, x.shape, 1) >> (DK.bit_length() - 1)
    return jnp.concatenate([jnp.where(lane_head == h, x, 0.0) for h in range(HEADS)], axis=0)


def _rows_by_head(v):
    return jnp.concatenate([v[:, h * DV:(h + 1) * DV] for h in range(HEADS)], axis=0)


def _gate_log_decay(gr6, wgu6, bg):
    hi, mid, lo = _split3(gr6)
    group = lax.broadcasted_iota(jnp.int32, gr6.shape, 1) >> (RANK.bit_length() - 1)
    lhs = jnp.where(group < 3, hi, jnp.where(group < 5, mid, lo))
    z = _dot(lhs, wgu6) + bg
    return _log_sigmoid(z) / GATE_NORM


def _cumsum_rows(tri3, gk):
    return _dot(tri3, jnp.concatenate(_split3(gk), axis=0))


def _state_update(st, kdec, v, decay_last):
    kv = _dot(_rows_by_head(v).T.astype(BF16), _stack_heads(kdec).astype(BF16))
    return st * decay_last + kv


def _prep_meta_kernel(meta_ref, nw_ref, win_ref, wout_ref, wgu_ref, bg_ref, tri_ref,
                      wmain_ref, wgr6_ref, wout16_ref, wgu6_ref, st_ref, tail_ref,
                      hn_s, k_acc, v_acc, cc_acc, cx_acc, gr_acc):
    i = pl.program_id(0)
    gr0 = 2 * QK + 2 * GW

    @pl.when(i == 0)
    def _():
        hn = _rms(meta_ref[...], nw_ref[...]).astype(BF16)
        for j in range(D_MODEL // PREP_SLAB):
            hn_s[j] = hn[:, j * PREP_SLAB:(j + 1) * PREP_SLAB]
        for acc in (k_acc, v_acc, cc_acc, cx_acc, gr_acc):
            acc[...] = jnp.zeros(acc.shape, F32)
        g_hi, g_mid, g_lo = _split3(wgu_ref[0])
        zeros = jnp.zeros((LANES - GATE_TERMS * RANK, QK), BF16)
        wgu6_ref[...] = jnp.concatenate([g_hi, g_mid, g_lo, g_hi, g_mid, g_hi, zeros], axis=0)

    w = win_ref[0]
    main = jnp.concatenate([w[:, :gr0], w[:, gr0 + RANK:]], axis=1).astype(BF16)
    gr = w[:, gr0:gr0 + RANK]
    gr6 = jnp.concatenate(
        [gr] * GATE_TERMS + [jnp.zeros((PREP_SLAB, LANES - GATE_TERMS * RANK), F32)], axis=1).astype(BF16)
    wmain_ref[...] = main
    wgr6_ref[...] = gr6
    wout16_ref[...] = wout_ref[0].astype(BF16)

    hs = hn_s[i]
    k_acc[...] += _dot(hs, main[:, QK:2 * QK])
    v_acc[...] += _dot(hs, main[:, 2 * QK:2 * QK + GW])
    cc_acc[...] += _dot(hs, main[:, gr0 + CONV_CH:gr0 + 2 * CONV_CH])
    cx_acc[...] += _dot(hs, main[:, gr0 + 2 * CONV_CH:gr0 + 3 * CONV_CH])
    gr_acc[...] += _dot(hs, gr6)

    @pl.when(i == pl.num_programs(0) - 1)
    def _():
        gk = _gate_log_decay(gr_acc[...], wgu6_ref[...], bg_ref[...])
        b = _cumsum_rows(tri_ref[...], gk)
        b_last = b[N_META - 1:N_META, :]
        st_ref[...] = _state_update(jnp.zeros((DV, QK), F32), k_acc[...] * jnp.exp(b_last - b), v_acc[...],
                                    jnp.exp(b_last))
        u = cc_acc[...] * cx_acc[...]
        tail_ref[...] = u[N_META - SUBLANES:N_META, :]


def _gla_wide(q_s, v_s, b_s, qp_s, kp_s, o_s, st_ref):
    row = lax.broadcasted_iota(jnp.int32, (HEADS * WIDE_CHUNK, WIDE_CHUNK), 0) & (WIDE_CHUNK - 1)
    col = lax.broadcasted_iota(jnp.int32, (HEADS * WIDE_CHUNK, WIDE_CHUNK), 1)
    causal = row >= col
    st = st_ref[...]
    for c in range(MIX_TILE // WIDE_CHUNK):
        rows = slice(c * WIDE_CHUNK, (c + 1) * WIDE_CHUNK)
        kp = kp_s[rows, :]
        v = v_s[rows, :]
        qp_heads = _stack_heads(qp_s[rows, :]).astype(BF16)
        sc = _dot_nt(qp_heads, jnp.concatenate([kp.astype(BF16), st.astype(BF16)], axis=0))
        a = jnp.where(causal, sc[:, 0:WIDE_CHUNK], 0.0).astype(BF16)
        for h in range(HEADS):
            hr = slice(h * WIDE_CHUNK, (h + 1) * WIDE_CHUNK)
            o_h = _dot(a[hr], v[:, h * DV:(h + 1) * DV].astype(BF16))
            o_s[rows, h * DV:(h + 1) * DV] = o_h + sc[hr, WIDE_CHUNK:WIDE_CHUNK + DV]
        decay_last = jnp.exp(b_s[(c + 1) * WIDE_CHUNK - 1:(c + 1) * WIDE_CHUNK, :])
        st = _state_update(st, kp * decay_last, v, decay_last)
    st_ref[...] = st


def _gla_stable(q_s, k_s, v_s, gk_s, o_s, st_ref, cmat_ref, lvl_ref):
    lvl = lvl_ref[...]

    def chunk_body(c, st):
        r0 = pl.multiple_of(c * CHUNK, CHUNK)
        q = q_s[pl.ds(r0, CHUNK), :]
        k = k_s[pl.ds(r0, CHUNK), :]
        v = v_s[pl.ds(r0, CHUNK), :]
        gk = gk_s[pl.ds(r0, CHUNK), :]
        br = _cumsum_rows(cmat_ref[...], gk)
        b = br[0:CHUNK]
        row = lax.broadcasted_iota(jnp.int32, (CHUNK, QK), 0)
        a = jnp.zeros((HEADS * CHUNK, CHUNK), F32)
        for l, w in enumerate(HALF_WIDTHS):
            ref = br[(l + 1) * CHUNK:(l + 2) * CHUNK]
            e = jnp.exp(-jnp.abs(b - ref))
            second = (row & w) != 0
            ql = jnp.where(second, q * e, 0.0)
            kl = jnp.where(second, 0.0, k * e)
            g = _dot_nt(_stack_heads(ql).astype(BF16), kl.astype(BF16))
            a = jnp.where(lvl == l, g, a)
        g = _dot_nt(_stack_heads(q).astype(BF16), k.astype(BF16))
        a = jnp.where(lvl == N_LEVELS, g, a)
        a = a.astype(BF16)
        o_inter = _dot_nt(_stack_heads(q * jnp.exp(b)).astype(BF16), st.astype(BF16))
        for h in range(HEADS):
            o_h = _dot(a[h * CHUNK:(h + 1) * CHUNK], v[:, h * DV:(h + 1) * DV].astype(BF16))
            o_s[pl.ds(r0, CHUNK), h * DV:(h + 1) * DV] = o_h + o_inter[h * CHUNK:(h + 1) * CHUNK]
        b_last = b[CHUNK - 1:CHUNK, :]
        return _state_update(st, k * jnp.exp(b_last - b), v, jnp.exp(b_last))

    st_ref[...] = lax.fori_loop(0, MIX_TILE // CHUNK, chunk_body, st_ref[...])


def _mixer_kernel(x_ref, xnext_ref, nw_ref, wmain_ref, wgr_ref, wgu_ref, bg_ref, gnw_ref, cw_ref, wout_ref,
                  tri_ref, cmat_ref, lvl_ref, st0_ref, tail0_ref, wup32_ref, wdown32_ref,
                  h1_ref, wup16_ref, wdown16_ref,
                  st_ref, tail_ref, hn_s, hnext_s, grnext_s, q_s, k_s, v_s, gk_s, b_s, qp_s, kp_s, o_s, u_s,
                  gate_s, yconv_s):
    t = pl.program_id(1)

    @pl.when(t == 0)
    def _():
        st_ref[...] = st0_ref[...]
        tail_ref[...] = tail0_ref[...]

    @pl.when(pl.program_id(0) + t == 0)
    def _():
        hnext_s[...] = _rms(x_ref[0], nw_ref[...]).astype(BF16)
        grnext_s[...] = _dot(hnext_s[...], wgr_ref[...])

    wup16_ref[...] = wup32_ref[...].astype(BF16)
    wdown16_ref[...] = wdown32_ref[...].astype(BF16)

    hn_s[...] = hnext_s[...]
    gk = _gate_log_decay(grnext_s[...], wgu_ref[...], bg_ref[...])
    qk = _dot(hn_s[...], wmain_ref[:, 0:2 * QK])
    q = qk[:, 0:QK] * (DK ** -0.5)
    k = qk[:, QK:2 * QK]
    q_s[...] = q
    k_s[...] = k
    gk_s[...] = gk
    n_wide = MIX_TILE // WIDE_CHUNK
    gk_cols = jnp.concatenate([gk[c * WIDE_CHUNK:(c + 1) * WIDE_CHUNK] for c in range(n_wide)], axis=1)
    b_cols = _cumsum_rows(tri_ref[...], gk_cols)
    b = jnp.concatenate([b_cols[:, c * QK:(c + 1) * QK] for c in range(n_wide)], axis=0)
    kp = k * jnp.exp(-b)
    b_s[...] = b
    qp_s[...] = q * jnp.exp(b)
    kp_s[...] = kp
    n_bad = jnp.sum(jnp.where(jnp.abs(kp) <= WIDE_KEY_BOUND, 0.0, 1.0))
    wide_ok = n_bad == 0.0

    c0 = 2 * QK + 2 * GW
    cc = _dot(hn_s[...], wmain_ref[:, c0 + CONV_CH:c0 + 2 * CONV_CH])
    cx = _dot(hn_s[...], wmain_ref[:, c0 + 2 * CONV_CH:c0 + 3 * CONV_CH])
    hnext = _rms(xnext_ref[0], nw_ref[...]).astype(BF16)
    hnext_s[...] = hnext
    grnext_s[...] = _dot(hnext, wgr_ref[...])
    cb = _dot(hn_s[...], wmain_ref[:, c0:c0 + CONV_CH])
    u = cc * cx
    u_s[0:SUBLANES, :] = tail_ref[...]
    u_s[SUBLANES:SUBLANES + MIX_TILE, :] = u
    tail_ref[...] = u[MIX_TILE - SUBLANES:MIX_TILE, :]
    u1 = u_s[SUBLANES - 1:SUBLANES - 1 + MIX_TILE, :]
    u2 = u_s[SUBLANES - 2:SUBLANES - 2 + MIX_TILE, :]
    yconv_s[...] = (cb * (cw_ref[0:1, :] * u2 + cw_ref[1:2, :] * u1 + cw_ref[2:3, :] * u)).astype(BF16)
    gate_s[...] = _silu(_dot(hn_s[...], wmain_ref[:, 2 * QK + GW:2 * QK + 2 * GW]))
    v_s[...] = _dot(hn_s[...], wmain_ref[:, 2 * QK:2 * QK + GW])

    @pl.when(wide_ok)
    def _():
        _gla_wide(q_s, v_s, b_s, qp_s, kp_s, o_s, st_ref)

    @pl.when(jnp.logical_not(wide_ok))
    def _():
        _gla_stable(q_s, k_s, v_s, gk_s, o_s, st_ref, cmat_ref, lvl_ref)

    gnw = gnw_ref[...]
    y_gla = jnp.concatenate(
        [_rms(o_s[:, h * DV:(h + 1) * DV], gnw) for h in range(HEADS)], axis=1) * gate_s[...]
    mixed = (_dot(yconv_s[...], wout_ref[GW:GW + CONV_CH, :])
             + _dot(y_gla.astype(BF16), wout_ref[0:GW, :]))
    h1_ref[0] = x_ref[0] + mixed


def _mlp_kernel(h_ref, nw_ref, wup_ref, wdown_ref, fw_ref, out_ref):
    blocks = [slice(r, r + MLP_ROWS) for r in range(0, MLP_TILE, MLP_ROWS)]
    hn = [_rms(h_ref[rows, :], nw_ref[...]).astype(BF16) for rows in blocks]
    acc = [jnp.zeros((MLP_ROWS, D_MODEL), F32) for _ in blocks]
    for f in range(0, D_FF, FF_CHUNK):
        for i in range(len(blocks)):
            up = _dot(hn[i], wup_ref[:, f:f + FF_CHUNK])
            act = jnp.square(jnp.maximum(up, 0.0)).astype(BF16)
            acc[i] = acc[i] + _dot(act, wdown_ref[f:f + FF_CHUNK, :])
    for i, rows in enumerate(blocks):
        out_ref[rows, :] = _rms(h_ref[rows, :] + acc[i], fw_ref[...])


def _tri3(n):
    tri = np.tril(np.ones((n, n), np.float32))
    return np.concatenate([tri, tri, tri], axis=1)


def _cumsum_and_ref_matrix():
    tri = np.tril(np.ones((CHUNK, CHUNK), np.float32))
    blocks = [tri]
    idx = np.arange(CHUNK)
    for w in HALF_WIDTHS:
        ref_row = (idx // (2 * w)) * (2 * w) + w - 1
        blocks.append(tri[ref_row])
    return np.concatenate(blocks, axis=0)


def _level_matrix():
    i = np.arange(CHUNK)[:, None]
    j = np.arange(CHUNK)[None, :]
    lvl = np.full((CHUNK, CHUNK), -1, np.int32)
    for l, w in enumerate(HALF_WIDTHS):
        same = (i // (2 * w)) == (j // (2 * w))
        lvl[same & ((i & w) != 0) & ((j & w) == 0)] = l
    lvl[i == j] = N_LEVELS
    return np.tile(lvl, (HEADS, 1))


def _const_spec(shape):
    return pl.BlockSpec(shape, lambda *_: (0,) * len(shape))


def kernel(x, meta_tokens, norm_mix_w, w_in, w_gate_up, b_gate, gla_norm_w, conv_w, w_out,
           norm_mlp_w, w_up, w_down, norm_final_w):
    batch, seq, _ = x.shape
    assert seq % MIX_TILE == 0 and (batch * seq) % MLP_TILE == 0
    assert norm_mix_w.shape[0] == 1, "single layer"

    bg = b_gate[0].reshape(1, QK)
    nw_mix = norm_mix_w[0].reshape(1, D_MODEL)
    gnw = gla_norm_w[0].reshape(1, DV)
    cw = conv_w[0]
    nw_mlp = norm_mlp_w[0].reshape(1, D_MODEL)
    fw = norm_final_w.reshape(1, D_MODEL)

    cmat = _cumsum_and_ref_matrix()
    cmat3 = jnp.asarray(np.concatenate([cmat, cmat, cmat], axis=1), BF16)
    tri_meta = jnp.asarray(_tri3(N_META), BF16)
    tri_wide = jnp.asarray(_tri3(WIDE_CHUNK), BF16)
    lvl = jnp.asarray(_level_matrix())

    main_width = PROJ_WIDTH - RANK
    n_slabs = D_MODEL // PREP_SLAB
    w_main, w_gr6, wout, wgu6, st0, tail0 = pl.pallas_call(
        _prep_meta_kernel,
        grid=(n_slabs,),
        in_specs=[
            _const_spec((N_META, D_MODEL)),
            _const_spec((1, D_MODEL)),
            pl.BlockSpec((1, PREP_SLAB, PROJ_WIDTH), lambda i: (0, i, 0)),
            pl.BlockSpec((1, PREP_SLAB, D_MODEL), lambda i: (0, i, 0)),
            _const_spec((1, RANK, QK)),
            _const_spec((1, QK)),
            _const_spec(tri_meta.shape),
        ],
        out_specs=(
            pl.BlockSpec((PREP_SLAB, main_width), lambda i: (i, 0)),
            pl.BlockSpec((PREP_SLAB, LANES), lambda i: (i, 0)),
            pl.BlockSpec((PREP_SLAB, D_MODEL), lambda i: (i, 0)),
            _const_spec((LANES, QK)),
            _const_spec((DV, QK)),
            _const_spec((SUBLANES, CONV_CH)),
        ),
        out_shape=(
            jax.ShapeDtypeStruct((D_MODEL, main_width), BF16),
            jax.ShapeDtypeStruct((D_MODEL, LANES), BF16),
            jax.ShapeDtypeStruct((D_MODEL, D_MODEL), BF16),
            jax.ShapeDtypeStruct((LANES, QK), BF16),
            jax.ShapeDtypeStruct((DV, QK), F32),
            jax.ShapeDtypeStruct((SUBLANES, CONV_CH), F32),
        ),
        scratch_shapes=[
            pltpu.VMEM((n_slabs, N_META, PREP_SLAB), BF16),
            pltpu.VMEM((N_META, QK), F32),
            pltpu.VMEM((N_META, GW), F32),
            pltpu.VMEM((N_META, CONV_CH), F32),
            pltpu.VMEM((N_META, CONV_CH), F32),
            pltpu.VMEM((N_META, LANES), F32),
        ],
        compiler_params=pltpu.CompilerParams(
            vmem_limit_bytes=VMEM_LIMIT, dimension_semantics=("arbitrary",)),
        name="weight_prep_meta_state",
    )(meta_tokens, nw_mix, w_in, w_out, w_gate_up, bg, tri_meta)

    n_tiles = seq // MIX_TILE

    def next_tile(b, t):
        nxt = jnp.minimum(b * n_tiles + t + 1, batch * n_tiles - 1)
        return nxt // n_tiles, nxt % n_tiles, 0

    n_steps = batch * n_tiles
    assert D_MODEL % n_steps == 0 and D_FF % n_steps == 0
    up_slab, down_slab = D_MODEL // n_steps, D_FF // n_steps

    h1, wup, wdown = pl.pallas_call(
        _mixer_kernel,
        grid=(batch, n_tiles),
        in_specs=[
            pl.BlockSpec((1, MIX_TILE, D_MODEL), lambda b, t: (b, t, 0)),
            pl.BlockSpec((1, MIX_TILE, D_MODEL), next_tile),
            _const_spec((1, D_MODEL)),
            _const_spec(w_main.shape),
            _const_spec(w_gr6.shape),
            _const_spec(wgu6.shape),
            _const_spec((1, QK)),
            _const_spec((1, DV)),
            _const_spec(cw.shape),
            _const_spec(wout.shape),
            _const_spec(tri_wide.shape),
            _const_spec(cmat3.shape),
            _const_spec(lvl.shape),
            _const_spec((DV, QK)),
            _const_spec((SUBLANES, CONV_CH)),
            pl.BlockSpec((up_slab, D_FF), lambda b, t: (b * n_tiles + t, 0)),
            pl.BlockSpec((down_slab, D_MODEL), lambda b, t: (b * n_tiles + t, 0)),
        ],
        out_specs=(
            pl.BlockSpec((1, MIX_TILE, D_MODEL), lambda b, t: (b, t, 0)),
            pl.BlockSpec((up_slab, D_FF), lambda b, t: (b * n_tiles + t, 0)),
            pl.BlockSpec((down_slab, D_MODEL), lambda b, t: (b * n_tiles + t, 0)),
        ),
        out_shape=(
            jax.ShapeDtypeStruct((batch, seq, D_MODEL), F32),
            jax.ShapeDtypeStruct((D_MODEL, D_FF), BF16),
            jax.ShapeDtypeStruct((D_FF, D_MODEL), BF16),
        ),
        scratch_shapes=[
            pltpu.VMEM((DV, QK), F32),
            pltpu.VMEM((SUBLANES, CONV_CH), F32),
            pltpu.VMEM((MIX_TILE, D_MODEL), BF16),
            pltpu.VMEM((MIX_TILE, D_MODEL), BF16),
            pltpu.VMEM((MIX_TILE, LANES), F32),
            pltpu.VMEM((MIX_TILE, QK), F32),
            pltpu.VMEM((MIX_TILE, QK), F32),
            pltpu.VMEM((MIX_TILE, GW), F32),
            pltpu.VMEM((MIX_TILE, QK), F32),
            pltpu.VMEM((MIX_TILE, QK), F32),
            pltpu.VMEM((MIX_TILE, QK), F32),
            pltpu.VMEM((MIX_TILE, QK), F32),
            pltpu.VMEM((MIX_TILE, GW), F32),
            pltpu.VMEM((MIX_TILE + SUBLANES, CONV_CH), F32),
            pltpu.VMEM((MIX_TILE, GW), F32),
            pltpu.VMEM((MIX_TILE, CONV_CH), BF16),
        ],
        compiler_params=pltpu.CompilerParams(
            vmem_limit_bytes=VMEM_LIMIT, dimension_semantics=("arbitrary", "arbitrary")),
        name="gla_conv_mixer",
    )(x, x, nw_mix, w_main, w_gr6, wgu6, bg, gnw, cw, wout, tri_wide, cmat3, lvl, st0, tail0,
      w_up[0], w_down[0])

    rows = batch * seq
    out = pl.pallas_call(
        _mlp_kernel,
        grid=(rows // MLP_TILE,),
        in_specs=[
            pl.BlockSpec((MLP_TILE, D_MODEL), lambda i: (i, 0)),
            _const_spec((1, D_MODEL)),
            _const_spec(wup.shape),
            _const_spec(wdown.shape),
            _const_spec((1, D_MODEL)),
        ],
        out_specs=pl.BlockSpec((MLP_TILE, D_MODEL), lambda i: (i, 0)),
        out_shape=jax.ShapeDtypeStruct((rows, D_MODEL), F32),
        compiler_params=pltpu.CompilerParams(
            vmem_limit_bytes=VMEM_LIMIT, dimension_semantics=("arbitrary",)),
        name="relu2_mlp_final_norm",
    )(h1.reshape(rows, D_MODEL), nw_mlp, wup, wdown, fw)
    return out.reshape(batch, seq, D_MODEL)
```

```python
import numpy as np
import jax
import jax.numpy as jnp
from jax import lax
from jax.experimental import pallas as pl
from jax.experimental.pallas import tpu as pltpu

D_MODEL = 1024
N_META = 16
HEADS = 4
DK = 64
DV = 128
QK = HEADS * DK
GW = HEADS * DV
RANK = 16
GATE_NORM = 16.0
CONV_CH = 512
D_FF = 4096
GLA_ROWS = 2 * QK + 2 * GW
PROJ_WIDTH = GLA_ROWS + RANK + 3 * CONV_CH
EPS = 1e-6

LANES = 128
SUBLANES = 8
VMEM_LIMIT = 56 * 1024 * 1024

MIX_TILE = 512
WIDE_CHUNK = 256
CHUNK = 64
HALF_WIDTHS = (32, 16, 8, 4, 2, 1)
N_LEVELS = len(HALF_WIDTHS)
WIDE_KEY_BOUND = 1e18
GATE_TERMS = 6
PREP_SLAB = 128
MLP_TILE = 1024
MLP_ROWS = 512
FF_CHUNK = 1024

BF16 = jnp.bfloat16
F32 = jnp.float32


def _rms(x, w):
    ms = jnp.mean(x * x, axis=-1, keepdims=True)
    return x * lax.rsqrt(ms + EPS) * w


def _log_sigmoid(z):
    return -(jnp.maximum(-z, 0.0) + jnp.log1p(jnp.exp(-jnp.abs(z))))


def _silu(g):
    return g * (0.5 * (1.0 + jnp.tanh(0.5 * g)))


def _dot(a, b):
    return jnp.dot(a, b, preferred_element_type=F32)


def _dot_nt(a, b):
    return lax.dot_general(a, b, (((1,), (1,)), ((), ())), preferred_element_type=F32)


def _split3(x):
    hi = x.astype(BF16)
    r1 = x - hi.astype(F32)
    mid = r1.astype(BF16)
    lo = (r1 - mid.astype(F32)).astype(BF16)
    return hi, mid, lo


def _stack_heads(x):
    lane_head = lax.broadcasted_iota(jnp.int32, x.shape, 1) >> (DK.bit_length() - 1)
    return jnp.concatenate([jnp.where(lane_head == h, x, 0.0) for h in range(HEADS)], axis=0)


def _rows_by_head(v):
    return jnp.concatenate([v[:, h * DV:(h + 1) * DV] for h in range(HEADS)], axis=0)


def _gate_log_decay(gr6, wgu6, bg):
    hi, mid, lo = _split3(gr6)
    group = lax.broadcasted_iota(jnp.int32, gr6.shape, 1) >> (RANK.bit_length() - 1)
    lhs = jnp.where(group < 3, hi, jnp.where(group < 5, mid, lo))
    z = _dot(lhs, wgu6) + bg
    return _log_sigmoid(z) / GATE_NORM


def _cumsum_rows(tri3, gk):
    return _dot(tri3, jnp.concatenate(_split3(gk), axis=0))


def _state_update(st, kdec, v, decay_last):
    kv = _dot(_rows_by_head(v).T.astype(BF16), _stack_heads(kdec).astype(BF16))
    return st * decay_last + kv


def _prep_meta_kernel(meta_ref, nw_ref, wa_ref, wb_ref, wg_ref, wout_ref, wgu_ref, bg_ref, tri_ref,
                      wa16_ref, wb16_ref, wgr6_ref, wout16_ref, wgu6_ref, st_ref, tail_ref,
                      hn_s, proj_s):
    i = pl.program_id(0)
    n_a = GLA_ROWS // PREP_SLAB

    @pl.when(i == 0)
    def _():
        hn_s[...] = _rms(meta_ref[...], nw_ref[...]).astype(BF16)
        g_hi, g_mid, g_lo = _split3(wgu_ref[0])
        zeros = jnp.zeros((LANES - GATE_TERMS * RANK, QK), BF16)
        wgu6_ref[...] = jnp.concatenate([g_hi, g_mid, g_lo, g_hi, g_mid, g_hi, zeros], axis=0)
        gr = wg_ref[...].astype(BF16)
        wgr6_ref[...] = jnp.concatenate(
            [gr] * GATE_TERMS + [jnp.zeros((LANES - GATE_TERMS * RANK, D_MODEL), BF16)], axis=0)

    wa = wa_ref[...].astype(BF16)
    wb = wb_ref[...].astype(BF16)
    wa16_ref[...] = wa
    wb16_ref[...] = wb
    wout16_ref[...] = wout_ref[0].astype(BF16)
    proj_s[i] = _dot_nt(hn_s[...], wa)
    proj_s[n_a + i] = _dot_nt(hn_s[...], wb)

    @pl.when(i == pl.num_programs(0) - 1)
    def _():
        def cols(first, width):
            lo = first // PREP_SLAB
            return jnp.concatenate([proj_s[j] for j in range(lo, lo + width // PREP_SLAB)], axis=1)

        k = cols(QK, QK)
        v = cols(2 * QK, GW)
        cc = cols(GLA_ROWS + CONV_CH, CONV_CH)
        cx = cols(GLA_ROWS + 2 * CONV_CH, CONV_CH)
        gk = _gate_log_decay(_dot_nt(hn_s[...], wgr6_ref[...]), wgu6_ref[...], bg_ref[...])
        b = _cumsum_rows(tri_ref[...], gk)
        b_last = b[N_META - 1:N_META, :]
        st_ref[...] = _state_update(jnp.zeros((DV, QK), F32), k * jnp.exp(b_last - b), v, jnp.exp(b_last))
        u = cc * cx
        tail_ref[...] = u[N_META - SUBLANES:N_META, :]


def _gla_wide(q_s, v_s, b_s, qp_s, kp_s, o_s, st_ref):
    row = lax.broadcasted_iota(jnp.int32, (HEADS * WIDE_CHUNK, WIDE_CHUNK), 0) & (WIDE_CHUNK - 1)
    col = lax.broadcasted_iota(jnp.int32, (HEADS * WIDE_CHUNK, WIDE_CHUNK), 1)
    causal = row >= col
    st = st_ref[...]
    for c in range(MIX_TILE // WIDE_CHUNK):
        rows = slice(c * WIDE_CHUNK, (c + 1) * WIDE_CHUNK)
        kp = kp_s[rows, :]
        v = v_s[rows, :]
        qp_heads = _stack_heads(qp_s[rows, :]).astype(BF16)
        sc = _dot_nt(qp_heads, jnp.concatenate([kp.astype(BF16), st.astype(BF16)], axis=0))
        a = jnp.where(causal, sc[:, 0:WIDE_CHUNK], 0.0).astype(BF16)
        for h in range(HEADS):
            hr = slice(h * WIDE_CHUNK, (h + 1) * WIDE_CHUNK)
            o_h = _dot(a[hr], v[:, h * DV:(h + 1) * DV].astype(BF16))
            o_s[rows, h * DV:(h + 1) * DV] = o_h + sc[hr, WIDE_CHUNK:WIDE_CHUNK + DV]
        decay_last = jnp.exp(b_s[(c + 1) * WIDE_CHUNK - 1:(c + 1) * WIDE_CHUNK, :])
        st = _state_update(st, kp * decay_last, v, decay_last)
    st_ref[...] = st


def _gla_stable(q_s, k_s, v_s, gk_s, o_s, st_ref, cmat_ref, lvl_ref):
    lvl = lvl_ref[...]

    def chunk_body(c, st):
        r0 = pl.multiple_of(c * CHUNK, CHUNK)
        q = q_s[pl.ds(r0, CHUNK), :]
        k = k_s[pl.ds(r0, CHUNK), :]
        v = v_s[pl.ds(r0, CHUNK), :]
        gk = gk_s[pl.ds(r0, CHUNK), :]
        br = _cumsum_rows(cmat_ref[...], gk)
        b = br[0:CHUNK]
        row = lax.broadcasted_iota(jnp.int32, (CHUNK, QK), 0)
        a = jnp.zeros((HEADS * CHUNK, CHUNK), F32)
        for l, w in enumerate(HALF_WIDTHS):
            ref = br[(l + 1) * CHUNK:(l + 2) * CHUNK]
            e = jnp.exp(-jnp.abs(b - ref))
            second = (row & w) != 0
            ql = jnp.where(second, q * e, 0.0)
            kl = jnp.where(second, 0.0, k * e)
            g = _dot_nt(_stack_heads(ql).astype(BF16), kl.astype(BF16))
            a = jnp.where(lvl == l, g, a)
        g = _dot_nt(_stack_heads(q).astype(BF16), k.astype(BF16))
        a = jnp.where(lvl == N_LEVELS, g, a)
        a = a.astype(BF16)
        o_inter = _dot_nt(_stack_heads(q * jnp.exp(b)).astype(BF16), st.astype(BF16))
        for h in range(HEADS):
            o_h = _dot(a[h * CHUNK:(h + 1) * CHUNK], v[:, h * DV:(h + 1) * DV].astype(BF16))
            o_s[pl.ds(r0, CHUNK), h * DV:(h + 1) * DV] = o_h + o_inter[h * CHUNK:(h + 1) * CHUNK]
        b_last = b[CHUNK - 1:CHUNK, :]
        return _state_update(st, k * jnp.exp(b_last - b), v, jnp.exp(b_last))

    st_ref[...] = lax.fori_loop(0, MIX_TILE // CHUNK, chunk_body, st_ref[...])


def _mixer_kernel(x_ref, xnext_ref, nw_ref, wa_ref, wb_ref, wgr_ref, wgu_ref, bg_ref, gnw_ref, cw_ref, wout_ref,
                  tri_ref, cmat_ref, lvl_ref, st0_ref, tail0_ref, wup32_ref, wdown32_ref,
                  h1_ref, wup16_ref, wdown16_ref,
                  st_ref, tail_ref, hn_s, hnext_s, grnext_s, q_s, k_s, v_s, gk_s, b_s, qp_s, kp_s, o_s, u_s,
                  gate_s, yconv_s):
    t = pl.program_id(1)

    @pl.when(t == 0)
    def _():
        st_ref[...] = st0_ref[...]
        tail_ref[...] = tail0_ref[...]

    @pl.when(pl.program_id(0) + t == 0)
    def _():
        hnext_s[...] = _rms(x_ref[0], nw_ref[...]).astype(BF16)
        grnext_s[...] = _dot_nt(hnext_s[...], wgr_ref[...])

    wup16_ref[...] = wup32_ref[...].astype(BF16)
    wdown16_ref[...] = wdown32_ref[...].astype(BF16)

    hn_s[...] = hnext_s[...]
    gk = _gate_log_decay(grnext_s[...], wgu_ref[...], bg_ref[...])
    qk = _dot_nt(hn_s[...], wa_ref[0:2 * QK, :])
    q = qk[:, 0:QK] * (DK ** -0.5)
    k = qk[:, QK:2 * QK]
    q_s[...] = q
    k_s[...] = k
    gk_s[...] = gk
    n_wide = MIX_TILE // WIDE_CHUNK
    gk_cols = jnp.concatenate([gk[c * WIDE_CHUNK:(c + 1) * WIDE_CHUNK] for c in range(n_wide)], axis=1)
    b_cols = _cumsum_rows(tri_ref[...], gk_cols)
    b = jnp.concatenate([b_cols[:, c * QK:(c + 1) * QK] for c in range(n_wide)], axis=0)
    kp = k * jnp.exp(-b)
    b_s[...] = b
    qp_s[...] = q * jnp.exp(b)
    kp_s[...] = kp
    n_bad = jnp.sum(jnp.where(jnp.abs(kp) <= WIDE_KEY_BOUND, 0.0, 1.0))
    wide_ok = n_bad == 0.0

    cc = _dot_nt(hn_s[...], wb_ref[CONV_CH:2 * CONV_CH, :])
    cx = _dot_nt(hn_s[...], wb_ref[2 * CONV_CH:3 * CONV_CH, :])
    hnext = _rms(xnext_ref[0], nw_ref[...]).astype(BF16)
    hnext_s[...] = hnext
    grnext_s[...] = _dot_nt(hnext, wgr_ref[...])
    cb = _dot_nt(hn_s[...], wb_ref[0:CONV_CH, :])
    u = cc * cx
    u_s[0:SUBLANES, :] = tail_ref[...]
    u_s[SUBLANES:SUBLANES + MIX_TILE, :] = u
    tail_ref[...] = u[MIX_TILE - SUBLANES:MIX_TILE, :]
    u1 = u_s[SUBLANES - 1:SUBLANES - 1 + MIX_TILE, :]
    u2 = u_s[SUBLANES - 2:SUBLANES - 2 + MIX_TILE, :]
    yconv_s[...] = (cb * (cw_ref[0:1, :] * u2 + cw_ref[1:2, :] * u1 + cw_ref[2:3, :] * u)).astype(BF16)
    gate_s[...] = _silu(_dot_nt(hn_s[...], wa_ref[2 * QK + GW:2 * QK + 2 * GW, :]))
    v_s[...] = _dot_nt(hn_s[...], wa_ref[2 * QK:2 * QK + GW, :])

    @pl.when(wide_ok)
    def _():
        _gla_wide(q_s, v_s, b_s, qp_s, kp_s, o_s, st_ref)

    @pl.when(jnp.logical_not(wide_ok))
    def _():
        _gla_stable(q_s, k_s, v_s, gk_s, o_s, st_ref, cmat_ref, lvl_ref)

    gnw = gnw_ref[...]
    y_gla = jnp.concatenate(
        [_rms(o_s[:, h * DV:(h + 1) * DV], gnw) for h in range(HEADS)], axis=1) * gate_s[...]
    mixed = (_dot(yconv_s[...], wout_ref[GW:GW + CONV_CH, :])
             + _dot(y_gla.astype(BF16), wout_ref[0:GW, :]))
    h1_ref[0] = x_ref[0] + mixed


def _mlp_kernel(h_ref, nw_ref, wup_ref, wdown_ref, fw_ref, out_ref):
    blocks = [slice(r, r + MLP_ROWS) for r in range(0, MLP_TILE, MLP_ROWS)]
    hn = [_rms(h_ref[rows, :], nw_ref[...]).astype(BF16) for rows in blocks]
    acc = [jnp.zeros((MLP_ROWS, D_MODEL), F32) for _ in blocks]
    for f in range(0, D_FF, FF_CHUNK):
        for i in range(len(blocks)):
            up = _dot(hn[i], wup_ref[:, f:f + FF_CHUNK])
            act = jnp.square(jnp.maximum(up, 0.0)).astype(BF16)
            acc[i] = acc[i] + _dot(act, wdown_ref[f:f + FF_CHUNK, :])
    for i, rows in enumerate(blocks):
        out_ref[rows, :] = _rms(h_ref[rows, :] + acc[i], fw_ref[...])


def _tri3(n):
    tri = np.tril(np.ones((n, n), np.float32))
    return np.concatenate([tri, tri, tri], axis=1)


def _cumsum_and_ref_matrix():
    tri = np.tril(np.ones((CHUNK, CHUNK), np.float32))
    blocks = [tri]
    idx = np.arange(CHUNK)
    for w in HALF_WIDTHS:
        ref_row = (idx // (2 * w)) * (2 * w) + w - 1
        blocks.append(tri[ref_row])
    return np.concatenate(blocks, axis=0)


def _level_matrix():
    i = np.arange(CHUNK)[:, None]
    j = np.arange(CHUNK)[None, :]
    lvl = np.full((CHUNK, CHUNK), -1, np.int32)
    for l, w in enumerate(HALF_WIDTHS):
        same = (i // (2 * w)) == (j // (2 * w))
        lvl[same & ((i & w) != 0) & ((j & w) == 0)] = l
    lvl[i == j] = N_LEVELS
    return np.tile(lvl, (HEADS, 1))


def _const_spec(shape):
    return pl.BlockSpec(shape, lambda *_: (0,) * len(shape))


def kernel(x, meta_tokens, norm_mix_w, w_in, w_gate_up, b_gate, gla_norm_w, conv_w, w_out,
           norm_mlp_w, w_up, w_down, norm_final_w):
    batch, seq, _ = x.shape
    assert seq % MIX_TILE == 0 and (batch * seq) % MLP_TILE == 0
    assert norm_mix_w.shape[0] == 1, "single layer"

    bg = b_gate[0].reshape(1, QK)
    nw_mix = norm_mix_w[0].reshape(1, D_MODEL)
    gnw = gla_norm_w[0].reshape(1, DV)
    cw = conv_w[0]
    nw_mlp = norm_mlp_w[0].reshape(1, D_MODEL)
    fw = norm_final_w.reshape(1, D_MODEL)

    cmat = _cumsum_and_ref_matrix()
    cmat3 = jnp.asarray(np.concatenate([cmat, cmat, cmat], axis=1), BF16)
    tri_meta = jnp.asarray(_tri3(N_META), BF16)
    tri_wide = jnp.asarray(_tri3(WIDE_CHUNK), BF16)
    lvl = jnp.asarray(_level_matrix())

    w_in_t = jnp.swapaxes(w_in, 1, 2)[0]
    conv_row0 = GLA_ROWS + RANK
    assert GLA_ROWS == 3 * CONV_CH and GLA_ROWS % PREP_SLAB == 0 and D_MODEL % PREP_SLAB == 0
    n_slabs = GLA_ROWS // PREP_SLAB
    last_out_slab = D_MODEL // PREP_SLAB - 1
    slab = (PREP_SLAB, D_MODEL)
    wa, wb, w_gr6, wout, wgu6, st0, tail0 = pl.pallas_call(
        _prep_meta_kernel,
        grid=(n_slabs,),
        in_specs=[
            _const_spec((N_META, D_MODEL)),
            _const_spec((1, D_MODEL)),
            pl.BlockSpec(slab, lambda i: (i, 0)),
            pl.BlockSpec((pl.Element(PREP_SLAB), pl.Element(D_MODEL)),
                         lambda i: (pl.multiple_of(conv_row0 + i * PREP_SLAB, RANK), 0)),
            pl.BlockSpec((pl.Element(RANK), pl.Element(D_MODEL)), lambda i: (GLA_ROWS, 0)),
            pl.BlockSpec((1,) + slab, lambda i: (0, jnp.minimum(i, last_out_slab), 0)),
            _const_spec((1, RANK, QK)),
            _const_spec((1, QK)),
            _const_spec(tri_meta.shape),
        ],
        out_specs=(
            pl.BlockSpec(slab, lambda i: (i, 0)),
            pl.BlockSpec(slab, lambda i: (i, 0)),
            _const_spec((LANES, D_MODEL)),
            pl.BlockSpec(slab, lambda i: (jnp.minimum(i, last_out_slab), 0)),
            _const_spec((LANES, QK)),
            _const_spec((DV, QK)),
            _const_spec((SUBLANES, CONV_CH)),
        ),
        out_shape=(
            jax.ShapeDtypeStruct((GLA_ROWS, D_MODEL), BF16),
            jax.ShapeDtypeStruct((3 * CONV_CH, D_MODEL), BF16),
            jax.ShapeDtypeStruct((LANES, D_MODEL), BF16),
            jax.ShapeDtypeStruct((D_MODEL, D_MODEL), BF16),
            jax.ShapeDtypeStruct((LANES, QK), BF16),
            jax.ShapeDtypeStruct((DV, QK), F32),
            jax.ShapeDtypeStruct((SUBLANES, CONV_CH), F32),
        ),
        scratch_shapes=[
            pltpu.VMEM((N_META, D_MODEL), BF16),
            pltpu.VMEM((2 * n_slabs, N_META, PREP_SLAB), F32),
        ],
        compiler_params=pltpu.CompilerParams(
            vmem_limit_bytes=VMEM_LIMIT, dimension_semantics=("arbitrary",)),
        name="weight_prep_meta_state",
    )(meta_tokens, nw_mix, w_in_t, w_in_t, w_in_t, w_out, w_gate_up, bg, tri_meta)

    n_tiles = seq // MIX_TILE

    def next_tile(b, t):
        nxt = jnp.minimum(b * n_tiles + t + 1, batch * n_tiles - 1)
        return nxt // n_tiles, nxt % n_tiles, 0

    n_steps = batch * n_tiles
    assert D_MODEL % n_steps == 0 and D_FF % n_steps == 0
    up_slab, down_slab = D_MODEL // n_steps, D_FF // n_steps

    h1, wup, wdown = pl.pallas_call(
        _mixer_kernel,
        grid=(batch, n_tiles),
        in_specs=[
            pl.BlockSpec((1, MIX_TILE, D_MODEL), lambda b, t: (b, t, 0)),
            pl.BlockSpec((1, MIX_TILE, D_MODEL), next_tile),
            _const_spec((1, D_MODEL)),
            _const_spec(wa.shape),
            _const_spec(wb.shape),
            _const_spec(w_gr6.shape),
            _const_spec(wgu6.shape),
            _const_spec((1, QK)),
            _const_spec((1, DV)),
            _const_spec(cw.shape),
            _const_spec(wout.shape),
            _const_spec(tri_wide.shape),
            _const_spec(cmat3.shape),
            _const_spec(lvl.shape),
            _const_spec((DV, QK)),
            _const_spec((SUBLANES, CONV_CH)),
            pl.BlockSpec((up_slab, D_FF), lambda b, t: (b * n_tiles + t, 0)),
            pl.BlockSpec((down_slab, D_MODEL), lambda b, t: (b * n_tiles + t, 0)),
        ],
        out_specs=(
            pl.BlockSpec((1, MIX_TILE, D_MODEL), lambda b, t: (b, t, 0)),
            pl.BlockSpec((up_slab, D_FF), lambda b, t: (b * n_tiles + t, 0)),
            pl.BlockSpec((down_slab, D_MODEL), lambda b, t: (b * n_tiles + t, 0)),
        ),
        out_shape=(
            jax.ShapeDtypeStruct((batch, seq, D_MODEL), F32),
            jax.ShapeDtypeStruct((D_MODEL, D_FF), BF16),
            jax.ShapeDtypeStruct((D_FF, D_MODEL), BF16),
        ),
        scratch_shapes=[
            pltpu.VMEM((DV, QK), F32),
            pltpu.VMEM((SUBLANES, CONV_CH), F32),
            pltpu.VMEM((MIX_TILE, D_MODEL), BF16),
            pltpu.VMEM((MIX_TILE, D_MODEL), BF16),
            pltpu.VMEM((MIX_TILE, LANES), F32),
            pltpu.VMEM((MIX_TILE, QK), F32),
            pltpu.VMEM((MIX_TILE, QK), F32),
            pltpu.VMEM((MIX_TILE, GW), F32),
            pltpu.VMEM((MIX_TILE, QK), F32),
            pltpu.VMEM((MIX_TILE, QK), F32),
            pltpu.VMEM((MIX_TILE, QK), F32),
            pltpu.VMEM((MIX_TILE, QK), F32),
            pltpu.VMEM((MIX_TILE, GW), F32),
            pltpu.VMEM((MIX_TILE + SUBLANES, CONV_CH), F32),
            pltpu.VMEM((MIX_TILE, GW), F32),
            pltpu.VMEM((MIX_TILE, CONV_CH), BF16),
        ],
        compiler_params=pltpu.CompilerParams(
            vmem_limit_bytes=VMEM_LIMIT, dimension_semantics=("arbitrary", "arbitrary")),
        name="gla_conv_mixer",
    )(x, x, nw_mix, wa, wb, w_gr6, wgu6, bg, gnw, cw, wout, tri_wide, cmat3, lvl, st0, tail0,
      w_up[0], w_down[0])

    rows = batch * seq
    out = pl.pallas_call(
        _mlp_kernel,
        grid=(rows // MLP_TILE,),
        in_specs=[
            pl.BlockSpec((MLP_TILE, D_MODEL), lambda i: (i, 0)),
            _const_spec((1, D_MODEL)),
            _const_spec(wup.shape),
            _const_spec(wdown.shape),
            _const_spec((1, D_MODEL)),
        ],
        out_specs=pl.BlockSpec((MLP_TILE, D_MODEL), lambda i: (i, 0)),
        out_shape=jax.ShapeDtypeStruct((rows, D_MODEL), F32),
        compiler_params=pltpu.CompilerParams(
            vmem_limit_bytes=VMEM_LIMIT, dimension_semantics=("arbitrary",)),
        name="relu2_mlp_final_norm",
    )(h1.reshape(rows, D_MODEL), nw_mlp, wup, wdown, fw)
    return out.reshape(batch, seq, D_MODEL)
```

```python
import numpy as np
import jax
import jax.numpy as jnp
from jax import lax
from jax.experimental import pallas as pl
from jax.experimental.pallas import tpu as pltpu

D_MODEL = 1024
N_META = 16
HEADS = 4
DK = 64
DV = 128
QK = HEADS * DK
GW = HEADS * DV
RANK = 16
GATE_NORM = 16.0
CONV_CH = 512
D_FF = 4096
GLA_ROWS = 2 * QK + 2 * GW
PROJ_WIDTH = GLA_ROWS + RANK + 3 * CONV_CH
EPS = 1e-6

LANES = 128
SUBLANES = 8
VMEM_LIMIT = 56 * 1024 * 1024

MIX_TILE = 512
WIDE_CHUNK = 256
CHUNK = 64
HALF_WIDTHS = (32, 16, 8, 4, 2, 1)
N_LEVELS = len(HALF_WIDTHS)
WIDE_KEY_BOUND = 1e18
GATE_TERMS = 6
PREP_SLAB = 512
MLP_TILE = 1024
MLP_ROWS = 512
FF_CHUNK = 1024

BF16 = jnp.bfloat16
F32 = jnp.float32


def _rms(x, w):
    ms = jnp.mean(x * x, axis=-1, keepdims=True)
    return x * lax.rsqrt(ms + EPS) * w


def _log_sigmoid(z):
    return -(jnp.maximum(-z, 0.0) + jnp.log1p(jnp.exp(-jnp.abs(z))))


def _silu(g):
    return g * (0.5 * (1.0 + jnp.tanh(0.5 * g)))


def _dot(a, b):
    return jnp.dot(a, b, preferred_element_type=F32)


def _dot_nt(a, b):
    return lax.dot_general(a, b, (((1,), (1,)), ((), ())), preferred_element_type=F32)


def _split3(x):
    hi = x.astype(BF16)
    r1 = x - hi.astype(F32)
    mid = r1.astype(BF16)
    lo = (r1 - mid.astype(F32)).astype(BF16)
    return hi, mid, lo


def _stack_heads(x):
    lane_head = lax.broadcasted_iota(jnp.int32, x.shape, 1) >> (DK.bit_length() - 1)
    return jnp.concatenate([jnp.where(lane_head == h, x, 0.0) for h in range(HEADS)], axis=0)


def _rows_by_head(v):
    return jnp.concatenate([v[:, h * DV:(h + 1) * DV] for h in range(HEADS)], axis=0)


def _gate_log_decay(gr6, wgu6, bg):
    hi, mid, lo = _split3(gr6)
    group = lax.broadcasted_iota(jnp.int32, gr6.shape, 1) >> (RANK.bit_length() - 1)
    lhs = jnp.where(group < 3, hi, jnp.where(group < 5, mid, lo))
    z = _dot(lhs, wgu6) + bg
    return _log_sigmoid(z) / GATE_NORM


def _cumsum_rows(tri3, gk):
    return _dot(tri3, jnp.concatenate(_split3(gk), axis=0))


def _state_update(st, kdec, v, decay_last):
    kv = _dot(_rows_by_head(v).T.astype(BF16), _stack_heads(kdec).astype(BF16))
    return st * decay_last + kv


def _prep_meta_kernel(meta_ref, nw_ref, wa_ref, wb_ref, wg_ref, wout_ref, wgu_ref, bg_ref, tri_ref,
                      wa16_ref, wb16_ref, wgr6_ref, wout16_ref, wgu6_ref, st_ref, tail_ref,
                      hn_s, proj_s):
    i = pl.program_id(0)
    n_a = GLA_ROWS // PREP_SLAB

    @pl.when(i == 0)
    def _():
        hn_s[...] = _rms(meta_ref[...], nw_ref[...]).astype(BF16)
        g_hi, g_mid, g_lo = _split3(wgu_ref[0])
        zeros = jnp.zeros((LANES - GATE_TERMS * RANK, QK), BF16)
        wgu6_ref[...] = jnp.concatenate([g_hi, g_mid, g_lo, g_hi, g_mid, g_hi, zeros], axis=0)
        gr = wg_ref[...].astype(BF16)
        wgr6_ref[...] = jnp.concatenate(
            [gr] * GATE_TERMS + [jnp.zeros((LANES - GATE_TERMS * RANK, D_MODEL), BF16)], axis=0)

    wa = wa_ref[...].astype(BF16)
    wb = wb_ref[...].astype(BF16)
    wa16_ref[...] = wa
    wb16_ref[...] = wb
    wout16_ref[...] = wout_ref[0].astype(BF16)
    proj_s[i] = _dot_nt(hn_s[...], wa)
    proj_s[n_a + i] = _dot_nt(hn_s[...], wb)

    @pl.when(i == pl.num_programs(0) - 1)
    def _():
        proj = jnp.concatenate([proj_s[j] for j in range(2 * n_a)], axis=1)
        k = proj[:, QK:2 * QK]
        v = proj[:, 2 * QK:2 * QK + GW]
        cc = proj[:, GLA_ROWS + CONV_CH:GLA_ROWS + 2 * CONV_CH]
        cx = proj[:, GLA_ROWS + 2 * CONV_CH:GLA_ROWS + 3 * CONV_CH]
        gk = _gate_log_decay(_dot_nt(hn_s[...], wgr6_ref[...]), wgu6_ref[...], bg_ref[...])
        b = _cumsum_rows(tri_ref[...], gk)
        b_last = b[N_META - 1:N_META, :]
        st_ref[...] = _state_update(jnp.zeros((DV, QK), F32), k * jnp.exp(b_last - b), v, jnp.exp(b_last))
        u = cc * cx
        tail_ref[...] = u[N_META - SUBLANES:N_META, :]


def _gla_wide(q_s, v_s, b_s, qp_s, kp_s, o_s, st_ref):
    row = lax.broadcasted_iota(jnp.int32, (HEADS * WIDE_CHUNK, WIDE_CHUNK), 0) & (WIDE_CHUNK - 1)
    col = lax.broadcasted_iota(jnp.int32, (HEADS * WIDE_CHUNK, WIDE_CHUNK), 1)
    causal = row >= col
    st = st_ref[...]
    for c in range(MIX_TILE // WIDE_CHUNK):
        rows = slice(c * WIDE_CHUNK, (c + 1) * WIDE_CHUNK)
        kp = kp_s[rows, :]
        v = v_s[rows, :]
        qp_heads = _stack_heads(qp_s[rows, :]).astype(BF16)
        sc = _dot_nt(qp_heads, jnp.concatenate([kp.astype(BF16), st.astype(BF16)], axis=0))
        a = jnp.where(causal, sc[:, 0:WIDE_CHUNK], 0.0).astype(BF16)
        for h in range(HEADS):
            hr = slice(h * WIDE_CHUNK, (h + 1) * WIDE_CHUNK)
            o_h = _dot(a[hr], v[:, h * DV:(h + 1) * DV].astype(BF16))
            o_s[rows, h * DV:(h + 1) * DV] = o_h + sc[hr, WIDE_CHUNK:WIDE_CHUNK + DV]
        decay_last = jnp.exp(b_s[(c + 1) * WIDE_CHUNK - 1:(c + 1) * WIDE_CHUNK, :])
        st = _state_update(st, kp * decay_last, v, decay_last)
    st_ref[...] = st


def _gla_stable(q_s, k_s, v_s, gk_s, o_s, st_ref, cmat_ref, lvl_ref):
    lvl = lvl_ref[...]

    def chunk_body(c, st):
        r0 = pl.multiple_of(c * CHUNK, CHUNK)
        q = q_s[pl.ds(r0, CHUNK), :]
        k = k_s[pl.ds(r0, CHUNK), :]
        v = v_s[pl.ds(r0, CHUNK), :]
        gk = gk_s[pl.ds(r0, CHUNK), :]
        br = _cumsum_rows(cmat_ref[...], gk)
        b = br[0:CHUNK]
        row = lax.broadcasted_iota(jnp.int32, (CHUNK, QK), 0)
        a = jnp.zeros((HEADS * CHUNK, CHUNK), F32)
        for l, w in enumerate(HALF_WIDTHS):
            ref = br[(l + 1) * CHUNK:(l + 2) * CHUNK]
            e = jnp.exp(-jnp.abs(b - ref))
            second = (row & w) != 0
            ql = jnp.where(second, q * e, 0.0)
            kl = jnp.where(second, 0.0, k * e)
            g = _dot_nt(_stack_heads(ql).astype(BF16), kl.astype(BF16))
            a = jnp.where(lvl == l, g, a)
        g = _dot_nt(_stack_heads(q).astype(BF16), k.astype(BF16))
        a = jnp.where(lvl == N_LEVELS, g, a)
        a = a.astype(BF16)
        o_inter = _dot_nt(_stack_heads(q * jnp.exp(b)).astype(BF16), st.astype(BF16))
        for h in range(HEADS):
            o_h = _dot(a[h * CHUNK:(h + 1) * CHUNK], v[:, h * DV:(h + 1) * DV].astype(BF16))
            o_s[pl.ds(r0, CHUNK), h * DV:(h + 1) * DV] = o_h + o_inter[h * CHUNK:(h + 1) * CHUNK]
        b_last = b[CHUNK - 1:CHUNK, :]
        return _state_update(st, k * jnp.exp(b_last - b), v, jnp.exp(b_last))

    st_ref[...] = lax.fori_loop(0, MIX_TILE // CHUNK, chunk_body, st_ref[...])


def _mixer_kernel(x_ref, xnext_ref, nw_ref, wa_ref, wb_ref, wgr_ref, wgu_ref, bg_ref, gnw_ref, cw_ref, wout_ref,
                  tri_ref, cmat_ref, lvl_ref, st0_ref, tail0_ref, wup32_ref, wdown32_ref,
                  h1_ref, wup16_ref, wdown16_ref,
                  st_ref, tail_ref, hn_s, hnext_s, grnext_s, q_s, k_s, v_s, gk_s, b_s, qp_s, kp_s, o_s, u_s,
                  gate_s, yconv_s):
    t = pl.program_id(1)

    @pl.when(t == 0)
    def _():
        st_ref[...] = st0_ref[...]
        tail_ref[...] = tail0_ref[...]

    @pl.when(pl.program_id(0) + t == 0)
    def _():
        hnext_s[...] = _rms(x_ref[0], nw_ref[...]).astype(BF16)
        grnext_s[...] = _dot_nt(hnext_s[...], wgr_ref[...])

    wup16_ref[...] = wup32_ref[...].astype(BF16)
    wdown16_ref[...] = wdown32_ref[...].astype(BF16)

    hn_s[...] = hnext_s[...]
    gk = _gate_log_decay(grnext_s[...], wgu_ref[...], bg_ref[...])
    qk = _dot_nt(hn_s[...], wa_ref[0:2 * QK, :])
    q = qk[:, 0:QK] * (DK ** -0.5)
    k = qk[:, QK:2 * QK]
    q_s[...] = q
    k_s[...] = k
    gk_s[...] = gk
    n_wide = MIX_TILE // WIDE_CHUNK
    gk_cols = jnp.concatenate([gk[c * WIDE_CHUNK:(c + 1) * WIDE_CHUNK] for c in range(n_wide)], axis=1)
    b_cols = _cumsum_rows(tri_ref[...], gk_cols)
    b = jnp.concatenate([b_cols[:, c * QK:(c + 1) * QK] for c in range(n_wide)], axis=0)
    kp = k * jnp.exp(-b)
    b_s[...] = b
    qp_s[...] = q * jnp.exp(b)
    kp_s[...] = kp
    n_bad = jnp.sum(jnp.where(jnp.abs(kp) <= WIDE_KEY_BOUND, 0.0, 1.0))
    wide_ok = n_bad == 0.0

    cc = _dot_nt(hn_s[...], wb_ref[CONV_CH:2 * CONV_CH, :])
    cx = _dot_nt(hn_s[...], wb_ref[2 * CONV_CH:3 * CONV_CH, :])
    hnext = _rms(xnext_ref[0], nw_ref[...]).astype(BF16)
    hnext_s[...] = hnext
    grnext_s[...] = _dot_nt(hnext, wgr_ref[...])
    cb = _dot_nt(hn_s[...], wb_ref[0:CONV_CH, :])
    u = cc * cx
    u_s[0:SUBLANES, :] = tail_ref[...]
    u_s[SUBLANES:SUBLANES + MIX_TILE, :] = u
    tail_ref[...] = u[MIX_TILE - SUBLANES:MIX_TILE, :]
    u1 = u_s[SUBLANES - 1:SUBLANES - 1 + MIX_TILE, :]
    u2 = u_s[SUBLANES - 2:SUBLANES - 2 + MIX_TILE, :]
    yconv_s[...] = (cb * (cw_ref[0:1, :] * u2 + cw_ref[1:2, :] * u1 + cw_ref[2:3, :] * u)).astype(BF16)
    gate_s[...] = _silu(_dot_nt(hn_s[...], wa_ref[2 * QK + GW:2 * QK + 2 * GW, :]))
    v_s[...] = _dot_nt(hn_s[...], wa_ref[2 * QK:2 * QK + GW, :])

    @pl.when(wide_ok)
    def _():
        _gla_wide(q_s, v_s, b_s, qp_s, kp_s, o_s, st_ref)

    @pl.when(jnp.logical_not(wide_ok))
    def _():
        _gla_stable(q_s, k_s, v_s, gk_s, o_s, st_ref, cmat_ref, lvl_ref)

    gnw = gnw_ref[...]
    y_gla = jnp.concatenate(
        [_rms(o_s[:, h * DV:(h + 1) * DV], gnw) for h in range(HEADS)], axis=1) * gate_s[...]
    mixed = (_dot(yconv_s[...], wout_ref[GW:GW + CONV_CH, :])
             + _dot(y_gla.astype(BF16), wout_ref[0:GW, :]))
    h1_ref[0] = x_ref[0] + mixed


def _mlp_kernel(h_ref, nw_ref, wup_ref, wdown_ref, fw_ref, out_ref):
    blocks = [slice(r, r + MLP_ROWS) for r in range(0, MLP_TILE, MLP_ROWS)]
    hn = [_rms(h_ref[rows, :], nw_ref[...]).astype(BF16) for rows in blocks]
    acc = [jnp.zeros((MLP_ROWS, D_MODEL), F32) for _ in blocks]
    for f in range(0, D_FF, FF_CHUNK):
        for i in range(len(blocks)):
            up = _dot(hn[i], wup_ref[:, f:f + FF_CHUNK])
            act = jnp.square(jnp.maximum(up, 0.0)).astype(BF16)
            acc[i] = acc[i] + _dot(act, wdown_ref[f:f + FF_CHUNK, :])
    for i, rows in enumerate(blocks):
        out_ref[rows, :] = _rms(h_ref[rows, :] + acc[i], fw_ref[...])


def _tri3(n):
    tri = np.tril(np.ones((n, n), np.float32))
    return np.concatenate([tri, tri, tri], axis=1)


def _cumsum_and_ref_matrix():
    tri = np.tril(np.ones((CHUNK, CHUNK), np.float32))
    blocks = [tri]
    idx = np.arange(CHUNK)
    for w in HALF_WIDTHS:
        ref_row = (idx // (2 * w)) * (2 * w) + w - 1
        blocks.append(tri[ref_row])
    return np.concatenate(blocks, axis=0)


def _level_matrix():
    i = np.arange(CHUNK)[:, None]
    j = np.arange(CHUNK)[None, :]
    lvl = np.full((CHUNK, CHUNK), -1, np.int32)
    for l, w in enumerate(HALF_WIDTHS):
        same = (i // (2 * w)) == (j // (2 * w))
        lvl[same & ((i & w) != 0) & ((j & w) == 0)] = l
    lvl[i == j] = N_LEVELS
    return np.tile(lvl, (HEADS, 1))


def _const_spec(shape):
    return pl.BlockSpec(shape, lambda *_: (0,) * len(shape))


def kernel(x, meta_tokens, norm_mix_w, w_in, w_gate_up, b_gate, gla_norm_w, conv_w, w_out,
           norm_mlp_w, w_up, w_down, norm_final_w):
    batch, seq, _ = x.shape
    assert seq % MIX_TILE == 0 and (batch * seq) % MLP_TILE == 0
    assert norm_mix_w.shape[0] == 1, "single layer"

    bg = b_gate[0].reshape(1, QK)
    nw_mix = norm_mix_w[0].reshape(1, D_MODEL)
    gnw = gla_norm_w[0].reshape(1, DV)
    cw = conv_w[0]
    nw_mlp = norm_mlp_w[0].reshape(1, D_MODEL)
    fw = norm_final_w.reshape(1, D_MODEL)

    cmat = _cumsum_and_ref_matrix()
    cmat3 = jnp.asarray(np.concatenate([cmat, cmat, cmat], axis=1), BF16)
    tri_meta = jnp.asarray(_tri3(N_META), BF16)
    tri_wide = jnp.asarray(_tri3(WIDE_CHUNK), BF16)
    lvl = jnp.asarray(_level_matrix())

    w_in_t = jnp.swapaxes(w_in, 1, 2)[0]
    conv_row0 = GLA_ROWS + RANK
    assert GLA_ROWS == 3 * CONV_CH and GLA_ROWS % PREP_SLAB == 0 and D_MODEL % PREP_SLAB == 0
    n_slabs = GLA_ROWS // PREP_SLAB
    last_out_slab = D_MODEL // PREP_SLAB - 1
    slab = (PREP_SLAB, D_MODEL)
    wa, wb, w_gr6, wout, wgu6, st0, tail0 = pl.pallas_call(
        _prep_meta_kernel,
        grid=(n_slabs,),
        in_specs=[
            _const_spec((N_META, D_MODEL)),
            _const_spec((1, D_MODEL)),
            pl.BlockSpec(slab, lambda i: (i, 0)),
            pl.BlockSpec((pl.Element(PREP_SLAB), pl.Element(D_MODEL)),
                         lambda i: (pl.multiple_of(conv_row0 + i * PREP_SLAB, RANK), 0)),
            pl.BlockSpec((pl.Element(RANK), pl.Element(D_MODEL)), lambda i: (GLA_ROWS, 0)),
            pl.BlockSpec((1,) + slab, lambda i: (0, jnp.minimum(i, last_out_slab), 0)),
            _const_spec((1, RANK, QK)),
            _const_spec((1, QK)),
            _const_spec(tri_meta.shape),
        ],
        out_specs=(
            pl.BlockSpec(slab, lambda i: (i, 0)),
            pl.BlockSpec(slab, lambda i: (i, 0)),
            _const_spec((LANES, D_MODEL)),
            pl.BlockSpec(slab, lambda i: (jnp.minimum(i, last_out_slab), 0)),
            _const_spec((LANES, QK)),
            _const_spec((DV, QK)),
            _const_spec((SUBLANES, CONV_CH)),
        ),
        out_shape=(
            jax.ShapeDtypeStruct((GLA_ROWS, D_MODEL), BF16),
            jax.ShapeDtypeStruct((3 * CONV_CH, D_MODEL), BF16),
            jax.ShapeDtypeStruct((LANES, D_MODEL), BF16),
            jax.ShapeDtypeStruct((D_MODEL, D_MODEL), BF16),
            jax.ShapeDtypeStruct((LANES, QK), BF16),
            jax.ShapeDtypeStruct((DV, QK), F32),
            jax.ShapeDtypeStruct((SUBLANES, CONV_CH), F32),
        ),
        scratch_shapes=[
            pltpu.VMEM((N_META, D_MODEL), BF16),
            pltpu.VMEM((2 * n_slabs, N_META, PREP_SLAB), F32),
        ],
        compiler_params=pltpu.CompilerParams(
            vmem_limit_bytes=VMEM_LIMIT, dimension_semantics=("arbitrary",)),
        name="weight_prep_meta_state",
    )(meta_tokens, nw_mix, w_in_t, w_in_t, w_in_t, w_out, w_gate_up, bg, tri_meta)

    n_tiles = seq // MIX_TILE

    def next_tile(b, t):
        nxt = jnp.minimum(b * n_tiles + t + 1, batch * n_tiles - 1)
        return nxt // n_tiles, nxt % n_tiles, 0

    n_steps = batch * n_tiles
    assert D_MODEL % n_steps == 0 and D_FF % n_steps == 0
    up_slab, down_slab = D_MODEL // n_steps, D_FF // n_steps

    h1, wup, wdown = pl.pallas_call(
        _mixer_kernel,
        grid=(batch, n_tiles),
        in_specs=[
            pl.BlockSpec((1, MIX_TILE, D_MODEL), lambda b, t: (b, t, 0)),
            pl.BlockSpec((1, MIX_TILE, D_MODEL), next_tile),
            _const_spec((1, D_MODEL)),
            _const_spec(wa.shape),
            _const_spec(wb.shape),
            _const_spec(w_gr6.shape),
            _const_spec(wgu6.shape),
            _const_spec((1, QK)),
            _const_spec((1, DV)),
            _const_spec(cw.shape),
            _const_spec(wout.shape),
            _const_spec(tri_wide.shape),
            _const_spec(cmat3.shape),
            _const_spec(lvl.shape),
            _const_spec((DV, QK)),
            _const_spec((SUBLANES, CONV_CH)),
            pl.BlockSpec((up_slab, D_FF), lambda b, t: (b * n_tiles + t, 0)),
            pl.BlockSpec((down_slab, D_MODEL), lambda b, t: (b * n_tiles + t, 0)),
        ],
        out_specs=(
            pl.BlockSpec((1, MIX_TILE, D_MODEL), lambda b, t: (b, t, 0)),
            pl.BlockSpec((up_slab, D_FF), lambda b, t: (b * n_tiles + t, 0)),
            pl.BlockSpec((down_slab, D_MODEL), lambda b, t: (b * n_tiles + t, 0)),
        ),
        out_shape=(
            jax.ShapeDtypeStruct((batch, seq, D_MODEL), F32),
            jax.ShapeDtypeStruct((D_MODEL, D_FF), BF16),
            jax.ShapeDtypeStruct((D_FF, D_MODEL), BF16),
        ),
        scratch_shapes=[
            pltpu.VMEM((DV, QK), F32),
            pltpu.VMEM((SUBLANES, CONV_CH), F32),
            pltpu.VMEM((MIX_TILE, D_MODEL), BF16),
            pltpu.VMEM((MIX_TILE, D_MODEL), BF16),
            pltpu.VMEM((MIX_TILE, LANES), F32),
            pltpu.VMEM((MIX_TILE, QK), F32),
            pltpu.VMEM((MIX_TILE, QK), F32),
            pltpu.VMEM((MIX_TILE, GW), F32),
            pltpu.VMEM((MIX_TILE, QK), F32),
            pltpu.VMEM((MIX_TILE, QK), F32),
            pltpu.VMEM((MIX_TILE, QK), F32),
            pltpu.VMEM((MIX_TILE, QK), F32),
            pltpu.VMEM((MIX_TILE, GW), F32),
            pltpu.VMEM((MIX_TILE + SUBLANES, CONV_CH), F32),
            pltpu.VMEM((MIX_TILE, GW), F32),
            pltpu.VMEM((MIX_TILE, CONV_CH), BF16),
        ],
        compiler_params=pltpu.CompilerParams(
            vmem_limit_bytes=VMEM_LIMIT, dimension_semantics=("arbitrary", "arbitrary")),
        name="gla_conv_mixer",
    )(x, x, nw_mix, wa, wb, w_gr6, wgu6, bg, gnw, cw, wout, tri_wide, cmat3, lvl, st0, tail0,
      w_up[0], w_down[0])

    rows = batch * seq
    out = pl.pallas_call(
        _mlp_kernel,
        grid=(rows // MLP_TILE,),
        in_specs=[
            pl.BlockSpec((MLP_TILE, D_MODEL), lambda i: (i, 0)),
            _const_spec((1, D_MODEL)),
            _const_spec(wup.shape),
            _const_spec(wdown.shape),
            _const_spec((1, D_MODEL)),
        ],
        out_specs=pl.BlockSpec((MLP_TILE, D_MODEL), lambda i: (i, 0)),
        out_shape=jax.ShapeDtypeStruct((rows, D_MODEL), F32),
        compiler_params=pltpu.CompilerParams(
            vmem_limit_bytes=VMEM_LIMIT, dimension_semantics=("arbitrary",)),
        name="relu2_mlp_final_norm",
    )(h1.reshape(rows, D_MODEL), nw_mlp, wup, wdown, fw)
    return out.reshape(batch, seq, D_MODEL)
```

```python
import numpy as np
import jax
import jax.numpy as jnp
from jax import lax
from jax.experimental import pallas as pl
from jax.experimental.pallas import tpu as pltpu

D_MODEL = 1024
N_META = 16
HEADS = 4
DK = 64
DV = 128
QK = HEADS * DK
GW = HEADS * DV
RANK = 16
GATE_NORM = 16.0
CONV_CH = 512
D_FF = 4096
GLA_ROWS = 2 * QK + 2 * GW
PROJ_WIDTH = GLA_ROWS + RANK + 3 * CONV_CH
EPS = 1e-6

LANES = 128
SUBLANES = 8
VMEM_LIMIT = 56 * 1024 * 1024

MIX_TILE = 512
WIDE_CHUNK = 128
CHUNK = 64
HALF_WIDTHS = (32, 16, 8, 4, 2, 1)
N_LEVELS = len(HALF_WIDTHS)
WIDE_KEY_BOUND = 1e18
GATE_TERMS = 6
PREP_SLAB = 512
MLP_TILE = 1024
MLP_ROWS = 512
FF_CHUNK = 1024

BF16 = jnp.bfloat16
F32 = jnp.float32


def _rms(x, w):
    ms = jnp.mean(x * x, axis=-1, keepdims=True)
    return x * lax.rsqrt(ms + EPS) * w


def _log_sigmoid(z):
    return -(jnp.maximum(-z, 0.0) + jnp.log1p(jnp.exp(-jnp.abs(z))))


def _silu(g):
    return g * (0.5 * (1.0 + jnp.tanh(0.5 * g)))


def _dot(a, b):
    return jnp.dot(a, b, preferred_element_type=F32)


def _dot_nt(a, b):
    return lax.dot_general(a, b, (((1,), (1,)), ((), ())), preferred_element_type=F32)


def _split3(x):
    hi = x.astype(BF16)
    r1 = x - hi.astype(F32)
    mid = r1.astype(BF16)
    lo = (r1 - mid.astype(F32)).astype(BF16)
    return hi, mid, lo


def _stack_heads(x):
    lane_head = lax.broadcasted_iota(jnp.int32, x.shape, 1) >> (DK.bit_length() - 1)
    return jnp.concatenate([jnp.where(lane_head == h, x, 0.0) for h in range(HEADS)], axis=0)


def _rows_by_head(v):
    return jnp.concatenate([v[:, h * DV:(h + 1) * DV] for h in range(HEADS)], axis=0)


def _gate_log_decay(gr6, wgu6, bg):
    hi, mid, lo = _split3(gr6)
    group = lax.broadcasted_iota(jnp.int32, gr6.shape, 1) >> (RANK.bit_length() - 1)
    lhs = jnp.where(group < 3, hi, jnp.where(group < 5, mid, lo))
    z = _dot(lhs, wgu6) + bg
    return _log_sigmoid(z) / GATE_NORM


def _cumsum_rows(tri3, gk):
    return _dot(tri3, jnp.concatenate(_split3(gk), axis=0))


def _state_increment(kdec, v):
    return _dot(_rows_by_head(v).T.astype(BF16), _stack_heads(kdec).astype(BF16))


def _state_update(st, kdec, v, decay_last):
    return st * decay_last + _state_increment(kdec, v)


def _prep_meta_kernel(meta_ref, nw_ref, wa_ref, wb_ref, wg_ref, wout_ref, wgu_ref, bg_ref, tri_ref,
                      wa16_ref, wb16_ref, wgr6_ref, wout16_ref, wgu6_ref, st_ref, tail_ref,
                      hn_s, proj_s):
    i = pl.program_id(0)
    n_a = GLA_ROWS // PREP_SLAB

    @pl.when(i == 0)
    def _():
        hn_s[...] = _rms(meta_ref[...], nw_ref[...]).astype(BF16)
        g_hi, g_mid, g_lo = _split3(wgu_ref[0])
        zeros = jnp.zeros((LANES - GATE_TERMS * RANK, QK), BF16)
        wgu6_ref[...] = jnp.concatenate([g_hi, g_mid, g_lo, g_hi, g_mid, g_hi, zeros], axis=0)
        gr = wg_ref[...].astype(BF16)
        wgr6_ref[...] = jnp.concatenate(
            [gr] * GATE_TERMS + [jnp.zeros((LANES - GATE_TERMS * RANK, D_MODEL), BF16)], axis=0)

    wa = wa_ref[...].astype(BF16)
    wb = wb_ref[...].astype(BF16)
    wa16_ref[...] = wa
    wb16_ref[...] = wb
    wout16_ref[...] = wout_ref[0].astype(BF16)
    proj_s[i] = _dot_nt(hn_s[...], wa)
    proj_s[n_a + i] = _dot_nt(hn_s[...], wb)

    @pl.when(i == pl.num_programs(0) - 1)
    def _():
        proj = jnp.concatenate([proj_s[j] for j in range(2 * n_a)], axis=1)
        k = proj[:, QK:2 * QK]
        v = proj[:, 2 * QK:2 * QK + GW]
        cc = proj[:, GLA_ROWS + CONV_CH:GLA_ROWS + 2 * CONV_CH]
        cx = proj[:, GLA_ROWS + 2 * CONV_CH:GLA_ROWS + 3 * CONV_CH]
        gk = _gate_log_decay(_dot_nt(hn_s[...], wgr6_ref[...]), wgu6_ref[...], bg_ref[...])
        b = _cumsum_rows(tri_ref[...], gk)
        b_last = b[N_META - 1:N_META, :]
        st_ref[...] = _state_update(jnp.zeros((DV, QK), F32), k * jnp.exp(b_last - b), v, jnp.exp(b_last))
        u = cc * cx
        tail_ref[...] = u[N_META - SUBLANES:N_META, :]


def _gla_wide(q_s, v_s, b_s, qp_s, kp_s, o_s, st_ref):
    row = lax.broadcasted_iota(jnp.int32, (HEADS * WIDE_CHUNK, WIDE_CHUNK), 0) & (WIDE_CHUNK - 1)
    col = lax.broadcasted_iota(jnp.int32, (HEADS * WIDE_CHUNK, WIDE_CHUNK), 1)
    causal = row >= col
    chunks = [slice(c * WIDE_CHUNK, (c + 1) * WIDE_CHUNK) for c in range(MIX_TILE // WIDE_CHUNK)]
    qp_heads = [_stack_heads(qp_s[rows, :]).astype(BF16) for rows in chunks]
    scores = [jnp.where(causal, _dot_nt(qp_heads[c], kp_s[rows, :].astype(BF16)), 0.0).astype(BF16)
              for c, rows in enumerate(chunks)]
    decay_last = [jnp.exp(b_s[rows.stop - 1:rows.stop, :]) for rows in chunks]
    kv = [_state_increment(kp_s[rows, :] * decay_last[c], v_s[rows, :]) for c, rows in enumerate(chunks)]
    states = [st_ref[...]]
    for c in range(len(chunks)):
        states.append(states[c] * decay_last[c] + kv[c])
    st_ref[...] = states[-1]
    for c, rows in enumerate(chunks):
        inter = _dot_nt(qp_heads[c], states[c].astype(BF16))
        for h in range(HEADS):
            hr = slice(h * WIDE_CHUNK, (h + 1) * WIDE_CHUNK)
            o_h = _dot(scores[c][hr], v_s[rows, h * DV:(h + 1) * DV].astype(BF16))
            o_s[rows, h * DV:(h + 1) * DV] = o_h + inter[hr]


def _gla_stable(q_s, k_s, v_s, gk_s, o_s, st_ref, cmat_ref, lvl_ref):
    lvl = lvl_ref[...]

    def chunk_body(c, st):
        r0 = pl.multiple_of(c * CHUNK, CHUNK)
        q = q_s[pl.ds(r0, CHUNK), :]
        k = k_s[pl.ds(r0, CHUNK), :]
        v = v_s[pl.ds(r0, CHUNK), :]
        gk = gk_s[pl.ds(r0, CHUNK), :]
        br = _cumsum_rows(cmat_ref[...], gk)
        b = br[0:CHUNK]
        row = lax.broadcasted_iota(jnp.int32, (CHUNK, QK), 0)
        a = jnp.zeros((HEADS * CHUNK, CHUNK), F32)
        for l, w in enumerate(HALF_WIDTHS):
            ref = br[(l + 1) * CHUNK:(l + 2) * CHUNK]
            e = jnp.exp(-jnp.abs(b - ref))
            second = (row & w) != 0
            ql = jnp.where(second, q * e, 0.0)
            kl = jnp.where(second, 0.0, k * e)
            g = _dot_nt(_stack_heads(ql).astype(BF16), kl.astype(BF16))
            a = jnp.where(lvl == l, g, a)
        g = _dot_nt(_stack_heads(q).astype(BF16), k.astype(BF16))
        a = jnp.where(lvl == N_LEVELS, g, a)
        a = a.astype(BF16)
        o_inter = _dot_nt(_stack_heads(q * jnp.exp(b)).astype(BF16), st.astype(BF16))
        for h in range(HEADS):
            o_h = _dot(a[h * CHUNK:(h + 1) * CHUNK], v[:, h * DV:(h + 1) * DV].astype(BF16))
            o_s[pl.ds(r0, CHUNK), h * DV:(h + 1) * DV] = o_h + o_inter[h * CHUNK:(h + 1) * CHUNK]
        b_last = b[CHUNK - 1:CHUNK, :]
        return _state_update(st, k * jnp.exp(b_last - b), v, jnp.exp(b_last))

    st_ref[...] = lax.fori_loop(0, MIX_TILE // CHUNK, chunk_body, st_ref[...])


def _mixer_kernel(x_ref, xnext_ref, nw_ref, wa_ref, wb_ref, wgr_ref, wgu_ref, bg_ref, gnw_ref, cw_ref, wout_ref,
                  tri_ref, cmat_ref, lvl_ref, st0_ref, tail0_ref, wup32_ref, wdown32_ref,
                  h1_ref, wup16_ref, wdown16_ref,
                  st_ref, tail_ref, hn_s, hnext_s, grnext_s, q_s, k_s, v_s, gk_s, b_s, qp_s, kp_s, o_s, u_s,
                  gate_s, yconv_s):
    t = pl.program_id(1)

    @pl.when(t == 0)
    def _():
        st_ref[...] = st0_ref[...]
        tail_ref[...] = tail0_ref[...]

    @pl.when(pl.program_id(0) + t == 0)
    def _():
        hnext_s[...] = _rms(x_ref[0], nw_ref[...]).astype(BF16)
        grnext_s[...] = _dot_nt(hnext_s[...], wgr_ref[...])

    wup16_ref[...] = wup32_ref[...].astype(BF16)
    wdown16_ref[...] = wdown32_ref[...].astype(BF16)

    hn_s[...] = hnext_s[...]
    gk = _gate_log_decay(grnext_s[...], wgu_ref[...], bg_ref[...])
    qk = _dot_nt(hn_s[...], wa_ref[0:2 * QK, :])
    q = qk[:, 0:QK] * (DK ** -0.5)
    k = qk[:, QK:2 * QK]
    q_s[...] = q
    k_s[...] = k
    gk_s[...] = gk
    n_wide = MIX_TILE // WIDE_CHUNK
    gk_cols = jnp.concatenate([gk[c * WIDE_CHUNK:(c + 1) * WIDE_CHUNK] for c in range(n_wide)], axis=1)
    b_cols = _cumsum_rows(tri_ref[...], gk_cols)
    b = jnp.concatenate([b_cols[:, c * QK:(c + 1) * QK] for c in range(n_wide)], axis=0)
    kp = k * jnp.exp(-b)
    b_s[...] = b
    qp_s[...] = q * jnp.exp(b)
    kp_s[...] = kp
    n_bad = jnp.sum(jnp.where(jnp.abs(kp) <= WIDE_KEY_BOUND, 0.0, 1.0))
    wide_ok = n_bad == 0.0

    cc = _dot_nt(hn_s[...], wb_ref[CONV_CH:2 * CONV_CH, :])
    cx = _dot_nt(hn_s[...], wb_ref[2 * CONV_CH:3 * CONV_CH, :])
    hnext = _rms(xnext_ref[0], nw_ref[...]).astype(BF16)
    hnext_s[...] = hnext
    grnext_s[...] = _dot_nt(hnext, wgr_ref[...])
    cb = _dot_nt(hn_s[...], wb_ref[0:CONV_CH, :])
    u = cc * cx
    u_s[0:SUBLANES, :] = tail_ref[...]
    u_s[SUBLANES:SUBLANES + MIX_TILE, :] = u
    tail_ref[...] = u[MIX_TILE - SUBLANES:MIX_TILE, :]
    u1 = u_s[SUBLANES - 1:SUBLANES - 1 + MIX_TILE, :]
    u2 = u_s[SUBLANES - 2:SUBLANES - 2 + MIX_TILE, :]
    yconv_s[...] = (cb * (cw_ref[0:1, :] * u2 + cw_ref[1:2, :] * u1 + cw_ref[2:3, :] * u)).astype(BF16)
    gate_s[...] = _silu(_dot_nt(hn_s[...], wa_ref[2 * QK + GW:2 * QK + 2 * GW, :]))
    v_s[...] = _dot_nt(hn_s[...], wa_ref[2 * QK:2 * QK + GW, :])

    @pl.when(wide_ok)
    def _():
        _gla_wide(q_s, v_s, b_s, qp_s, kp_s, o_s, st_ref)

    @pl.when(jnp.logical_not(wide_ok))
    def _():
        _gla_stable(q_s, k_s, v_s, gk_s, o_s, st_ref, cmat_ref, lvl_ref)

    gnw = gnw_ref[...]
    y_gla = jnp.concatenate(
        [_rms(o_s[:, h * DV:(h + 1) * DV], gnw) for h in range(HEADS)], axis=1) * gate_s[...]
    mixed = (_dot(yconv_s[...], wout_ref[GW:GW + CONV_CH, :])
             + _dot(y_gla.astype(BF16), wout_ref[0:GW, :]))
    h1_ref[0] = x_ref[0] + mixed


def _mlp_kernel(h_ref, nw_ref, wup_ref, wdown_ref, fw_ref, out_ref):
    blocks = [slice(r, r + MLP_ROWS) for r in range(0, MLP_TILE, MLP_ROWS)]
    hn = [_rms(h_ref[rows, :], nw_ref[...]).astype(BF16) for rows in blocks]
    acc = [jnp.zeros((MLP_ROWS, D_MODEL), F32) for _ in blocks]
    for f in range(0, D_FF, FF_CHUNK):
        for i in range(len(blocks)):
            up = _dot(hn[i], wup_ref[:, f:f + FF_CHUNK])
            act = jnp.square(jnp.maximum(up, 0.0)).astype(BF16)
            acc[i] = acc[i] + _dot(act, wdown_ref[f:f + FF_CHUNK, :])
    for i, rows in enumerate(blocks):
        out_ref[rows, :] = _rms(h_ref[rows, :] + acc[i], fw_ref[...])


def _tri3(n):
    tri = np.tril(np.ones((n, n), np.float32))
    return np.concatenate([tri, tri, tri], axis=1)


def _cumsum_and_ref_matrix():
    tri = np.tril(np.ones((CHUNK, CHUNK), np.float32))
    blocks = [tri]
    idx = np.arange(CHUNK)
    for w in HALF_WIDTHS:
        ref_row = (idx // (2 * w)) * (2 * w) + w - 1
        blocks.append(tri[ref_row])
    return np.concatenate(blocks, axis=0)


def _level_matrix():
    i = np.arange(CHUNK)[:, None]
    j = np.arange(CHUNK)[None, :]
    lvl = np.full((CHUNK, CHUNK), -1, np.int32)
    for l, w in enumerate(HALF_WIDTHS):
        same = (i // (2 * w)) == (j // (2 * w))
        lvl[same & ((i & w) != 0) & ((j & w) == 0)] = l
    lvl[i == j] = N_LEVELS
    return np.tile(lvl, (HEADS, 1))


def _const_spec(shape):
    return pl.BlockSpec(shape, lambda *_: (0,) * len(shape))


def kernel(x, meta_tokens, norm_mix_w, w_in, w_gate_up, b_gate, gla_norm_w, conv_w, w_out,
           norm_mlp_w, w_up, w_down, norm_final_w):
    batch, seq, _ = x.shape
    assert seq % MIX_TILE == 0 and (batch * seq) % MLP_TILE == 0
    assert norm_mix_w.shape[0] == 1, "single layer"

    bg = b_gate[0].reshape(1, QK)
    nw_mix = norm_mix_w[0].reshape(1, D_MODEL)
    gnw = gla_norm_w[0].reshape(1, DV)
    cw = conv_w[0]
    nw_mlp = norm_mlp_w[0].reshape(1, D_MODEL)
    fw = norm_final_w.reshape(1, D_MODEL)

    cmat = _cumsum_and_ref_matrix()
    cmat3 = jnp.asarray(np.concatenate([cmat, cmat, cmat], axis=1), BF16)
    tri_meta = jnp.asarray(_tri3(N_META), BF16)
    tri_wide = jnp.asarray(_tri3(WIDE_CHUNK), BF16)
    lvl = jnp.asarray(_level_matrix())

    w_in_t = jnp.swapaxes(w_in, 1, 2)[0]
    conv_row0 = GLA_ROWS + RANK
    assert GLA_ROWS == 3 * CONV_CH and GLA_ROWS % PREP_SLAB == 0 and D_MODEL % PREP_SLAB == 0
    n_slabs = GLA_ROWS // PREP_SLAB
    last_out_slab = D_MODEL // PREP_SLAB - 1
    slab = (PREP_SLAB, D_MODEL)
    wa, wb, w_gr6, wout, wgu6, st0, tail0 = pl.pallas_call(
        _prep_meta_kernel,
        grid=(n_slabs,),
        in_specs=[
            _const_spec((N_META, D_MODEL)),
            _const_spec((1, D_MODEL)),
            pl.BlockSpec(slab, lambda i: (i, 0)),
            pl.BlockSpec((pl.Element(PREP_SLAB), pl.Element(D_MODEL)),
                         lambda i: (pl.multiple_of(conv_row0 + i * PREP_SLAB, RANK), 0)),
            pl.BlockSpec((pl.Element(RANK), pl.Element(D_MODEL)), lambda i: (GLA_ROWS, 0)),
            pl.BlockSpec((1,) + slab, lambda i: (0, jnp.minimum(i, last_out_slab), 0)),
            _const_spec((1, RANK, QK)),
            _const_spec((1, QK)),
            _const_spec(tri_meta.shape),
        ],
        out_specs=(
            pl.BlockSpec(slab, lambda i: (i, 0)),
            pl.BlockSpec(slab, lambda i: (i, 0)),
            _const_spec((LANES, D_MODEL)),
            pl.BlockSpec(slab, lambda i: (jnp.minimum(i, last_out_slab), 0)),
            _const_spec((LANES, QK)),
            _const_spec((DV, QK)),
            _const_spec((SUBLANES, CONV_CH)),
        ),
        out_shape=(
            jax.ShapeDtypeStruct((GLA_ROWS, D_MODEL), BF16),
            jax.ShapeDtypeStruct((3 * CONV_CH, D_MODEL), BF16),
            jax.ShapeDtypeStruct((LANES, D_MODEL), BF16),
            jax.ShapeDtypeStruct((D_MODEL, D_MODEL), BF16),
            jax.ShapeDtypeStruct((LANES, QK), BF16),
            jax.ShapeDtypeStruct((DV, QK), F32),
            jax.ShapeDtypeStruct((SUBLANES, CONV_CH), F32),
        ),
        scratch_shapes=[
            pltpu.VMEM((N_META, D_MODEL), BF16),
            pltpu.VMEM((2 * n_slabs, N_META, PREP_SLAB), F32),
        ],
        compiler_params=pltpu.CompilerParams(
            vmem_limit_bytes=VMEM_LIMIT, dimension_semantics=("arbitrary",)),
        name="weight_prep_meta_state",
    )(meta_tokens, nw_mix, w_in_t, w_in_t, w_in_t, w_out, w_gate_up, bg, tri_meta)

    n_tiles = seq // MIX_TILE

    def next_tile(b, t):
        nxt = jnp.minimum(b * n_tiles + t + 1, batch * n_tiles - 1)
        return nxt // n_tiles, nxt % n_tiles, 0

    n_steps = batch * n_tiles
    assert D_MODEL % n_steps == 0 and D_FF % n_steps == 0
    up_slab, down_slab = D_MODEL // n_steps, D_FF // n_steps

    h1, wup, wdown = pl.pallas_call(
        _mixer_kernel,
        grid=(batch, n_tiles),
        in_specs=[
            pl.BlockSpec((1, MIX_TILE, D_MODEL), lambda b, t: (b, t, 0)),
            pl.BlockSpec((1, MIX_TILE, D_MODEL), next_tile),
            _const_spec((1, D_MODEL)),
            _const_spec(wa.shape),
            _const_spec(wb.shape),
            _const_spec(w_gr6.shape),
            _const_spec(wgu6.shape),
            _const_spec((1, QK)),
            _const_spec((1, DV)),
            _const_spec(cw.shape),
            _const_spec(wout.shape),
            _const_spec(tri_wide.shape),
            _const_spec(cmat3.shape),
            _const_spec(lvl.shape),
            _const_spec((DV, QK)),
            _const_spec((SUBLANES, CONV_CH)),
            pl.BlockSpec((up_slab, D_FF), lambda b, t: (b * n_tiles + t, 0)),
            pl.BlockSpec((down_slab, D_MODEL), lambda b, t: (b * n_tiles + t, 0)),
        ],
        out_specs=(
            pl.BlockSpec((1, MIX_TILE, D_MODEL), lambda b, t: (b, t, 0)),
            pl.BlockSpec((up_slab, D_FF), lambda b, t: (b * n_tiles + t, 0)),
            pl.BlockSpec((down_slab, D_MODEL), lambda b, t: (b * n_tiles + t, 0)),
        ),
        out_shape=(
            jax.ShapeDtypeStruct((batch, seq, D_MODEL), F32),
            jax.ShapeDtypeStruct((D_MODEL, D_FF), BF16),
            jax.ShapeDtypeStruct((D_FF, D_MODEL), BF16),
        ),
        scratch_shapes=[
            pltpu.VMEM((DV, QK), F32),
            pltpu.VMEM((SUBLANES, CONV_CH), F32),
            pltpu.VMEM((MIX_TILE, D_MODEL), BF16),
            pltpu.VMEM((MIX_TILE, D_MODEL), BF16),
            pltpu.VMEM((MIX_TILE, LANES), F32),
            pltpu.VMEM((MIX_TILE, QK), F32),
            pltpu.VMEM((MIX_TILE, QK), F32),
            pltpu.VMEM((MIX_TILE, GW), F32),
            pltpu.VMEM((MIX_TILE, QK), F32),
            pltpu.VMEM((MIX_TILE, QK), F32),
            pltpu.VMEM((MIX_TILE, QK), F32),
            pltpu.VMEM((MIX_TILE, QK), F32),
            pltpu.VMEM((MIX_TILE, GW), F32),
            pltpu.VMEM((MIX_TILE + SUBLANES, CONV_CH), F32),
            pltpu.VMEM((MIX_TILE, GW), F32),
            pltpu.VMEM((MIX_TILE, CONV_CH), BF16),
        ],
        compiler_params=pltpu.CompilerParams(
            vmem_limit_bytes=VMEM_LIMIT, dimension_semantics=("arbitrary", "arbitrary")),
        name="gla_conv_mixer",
    )(x, x, nw_mix, wa, wb, w_gr6, wgu6, bg, gnw, cw, wout, tri_wide, cmat3, lvl, st0, tail0,
      w_up[0], w_down[0])

    rows = batch * seq
    out = pl.pallas_call(
        _mlp_kernel,
        grid=(rows // MLP_TILE,),
        in_specs=[
            pl.BlockSpec((MLP_TILE, D_MODEL), lambda i: (i, 0)),
            _const_spec((1, D_MODEL)),
            _const_spec(wup.shape),
            _const_spec(wdown.shape),
            _const_spec((1, D_MODEL)),
        ],
        out_specs=pl.BlockSpec((MLP_TILE, D_MODEL), lambda i: (i, 0)),
        out_shape=jax.ShapeDtypeStruct((rows, D_MODEL), F32),
        compiler_params=pltpu.CompilerParams(
            vmem_limit_bytes=VMEM_LIMIT, dimension_semantics=("arbitrary",)),
        name="relu2_mlp_final_norm",
    )(h1.reshape(rows, D_MODEL), nw_mlp, wup, wdown, fw)
    return out.reshape(batch, seq, D_MODEL)
```

```python
import numpy as np
import jax
import jax.numpy as jnp
from jax import lax
from jax.experimental import pallas as pl
from jax.experimental.pallas import tpu as pltpu

D_MODEL = 1024
N_META = 16
HEADS = 4
DK = 64
DV = 128
QK = HEADS * DK
GW = HEADS * DV
RANK = 16
GATE_NORM = 16.0
CONV_CH = 512
D_FF = 4096
GLA_ROWS = 2 * QK + 2 * GW
PROJ_WIDTH = GLA_ROWS + RANK + 3 * CONV_CH
EPS = 1e-6

LANES = 128
SUBLANES = 8
VMEM_LIMIT = 56 * 1024 * 1024

MIX_TILE = 512
WIDE_CHUNK = 128
CHUNK = 64
HALF_WIDTHS = (32, 16, 8, 4, 2, 1)
N_LEVELS = len(HALF_WIDTHS)
WIDE_KEY_BOUND = 1e18
GATE_TERMS = 6
PREP_SLAB = 512
MLP_TILE = 1024
MLP_ROWS = 512
FF_CHUNK = 1024

BF16 = jnp.bfloat16
F32 = jnp.float32


def _rms(x, w):
    ms = jnp.mean(x * x, axis=-1, keepdims=True)
    return x * lax.rsqrt(ms + EPS) * w


def _log_sigmoid(z):
    return -(jnp.maximum(-z, 0.0) + jnp.log1p(jnp.exp(-jnp.abs(z))))


def _silu(g):
    return g * (0.5 * (1.0 + jnp.tanh(0.5 * g)))


def _dot(a, b):
    return jnp.dot(a, b, preferred_element_type=F32)


def _dot_nt(a, b):
    return lax.dot_general(a, b, (((1,), (1,)), ((), ())), preferred_element_type=F32)


def _split3(x):
    hi = x.astype(BF16)
    r1 = x - hi.astype(F32)
    mid = r1.astype(BF16)
    lo = (r1 - mid.astype(F32)).astype(BF16)
    return hi, mid, lo


def _stack_heads(x):
    lane_head = lax.broadcasted_iota(jnp.int32, x.shape, 1) >> (DK.bit_length() - 1)
    return jnp.concatenate([jnp.where(lane_head == h, x, 0.0) for h in range(HEADS)], axis=0)


def _rows_by_head(v):
    return jnp.concatenate([v[:, h * DV:(h + 1) * DV] for h in range(HEADS)], axis=0)


def _gate_log_decay(gr6, wgu6, bg):
    hi, mid, lo = _split3(gr6)
    group = lax.broadcasted_iota(jnp.int32, gr6.shape, 1) >> (RANK.bit_length() - 1)
    lhs = jnp.where(group < 3, hi, jnp.where(group < 5, mid, lo))
    z = _dot(lhs, wgu6) + bg
    return _log_sigmoid(z) / GATE_NORM


def _cumsum_rows(tri3, gk):
    return _dot(tri3, jnp.concatenate(_split3(gk), axis=0))


def _state_increment(kdec, v):
    return _dot(_rows_by_head(v).T.astype(BF16), _stack_heads(kdec).astype(BF16))


def _state_update(st, kdec, v, decay_last):
    return st * decay_last + _state_increment(kdec, v)


def _prep_meta_kernel(meta_ref, nw_ref, wa_ref, wb_ref, wg_ref, wout_ref, wgu_ref, bg_ref, tri_ref,
                      wa16_ref, wb16_ref, wgr6_ref, wout16_ref, wgu6_ref, st_ref, tail_ref,
                      hn_s, proj_s):
    i = pl.program_id(0)
    n_a = GLA_ROWS // PREP_SLAB

    @pl.when(i == 0)
    def _():
        hn_s[...] = _rms(meta_ref[...], nw_ref[...]).astype(BF16)
        g_hi, g_mid, g_lo = _split3(wgu_ref[0])
        zeros = jnp.zeros((LANES - GATE_TERMS * RANK, QK), BF16)
        wgu6_ref[...] = jnp.concatenate([g_hi, g_mid, g_lo, g_hi, g_mid, g_hi, zeros], axis=0)
        gr = wg_ref[...].astype(BF16)
        wgr6_ref[...] = jnp.concatenate(
            [gr] * GATE_TERMS + [jnp.zeros((LANES - GATE_TERMS * RANK, D_MODEL), BF16)], axis=0)

    wa = wa_ref[...].astype(BF16)
    wb = wb_ref[...].astype(BF16)
    wa16_ref[...] = wa
    wb16_ref[...] = wb
    wout16_ref[...] = wout_ref[0].astype(BF16)
    proj_s[i] = _dot_nt(hn_s[...], wa)
    proj_s[n_a + i] = _dot_nt(hn_s[...], wb)

    @pl.when(i == pl.num_programs(0) - 1)
    def _():
        proj = jnp.concatenate([proj_s[j] for j in range(2 * n_a)], axis=1)
        k = proj[:, QK:2 * QK]
        v = proj[:, 2 * QK:2 * QK + GW]
        cc = proj[:, GLA_ROWS + CONV_CH:GLA_ROWS + 2 * CONV_CH]
        cx = proj[:, GLA_ROWS + 2 * CONV_CH:GLA_ROWS + 3 * CONV_CH]
        gk = _gate_log_decay(_dot_nt(hn_s[...], wgr6_ref[...]), wgu6_ref[...], bg_ref[...])
        b = _cumsum_rows(tri_ref[...], gk)
        b_last = b[N_META - 1:N_META, :]
        st_ref[...] = _state_update(jnp.zeros((DV, QK), F32), k * jnp.exp(b_last - b), v, jnp.exp(b_last))
        u = cc * cx
        tail_ref[...] = u[N_META - SUBLANES:N_META, :]


def _gla_wide(q_s, v_s, b_s, qp_s, kp_s, o_s, st_ref):
    row = lax.broadcasted_iota(jnp.int32, (HEADS * WIDE_CHUNK, WIDE_CHUNK), 0) & (WIDE_CHUNK - 1)
    col = lax.broadcasted_iota(jnp.int32, (HEADS * WIDE_CHUNK, WIDE_CHUNK), 1)
    causal = row >= col
    chunks = [slice(c * WIDE_CHUNK, (c + 1) * WIDE_CHUNK) for c in range(MIX_TILE // WIDE_CHUNK)]
    qp_heads = [_stack_heads(qp_s[rows, :]).astype(BF16) for rows in chunks]
    scores = [jnp.where(causal, _dot_nt(qp_heads[c], kp_s[rows, :].astype(BF16)), 0.0).astype(BF16)
              for c, rows in enumerate(chunks)]
    decay_last = [jnp.exp(b_s[rows.stop - 1:rows.stop, :]) for rows in chunks]
    kv = [_state_increment(kp_s[rows, :] * decay_last[c], v_s[rows, :]) for c, rows in enumerate(chunks)]
    states = [st_ref[...]]
    for c in range(len(chunks)):
        states.append(states[c] * decay_last[c] + kv[c])
    st_ref[...] = states[-1]
    for c, rows in enumerate(chunks):
        inter = _dot_nt(qp_heads[c], states[c].astype(BF16))
        for h in range(HEADS):
            hr = slice(h * WIDE_CHUNK, (h + 1) * WIDE_CHUNK)
            o_h = _dot(scores[c][hr], v_s[rows, h * DV:(h + 1) * DV].astype(BF16))
            o_s[rows, h * DV:(h + 1) * DV] = o_h + inter[hr]


def _gla_stable(q_s, k_s, v_s, gk_s, o_s, st_ref, cmat_ref, lvl_ref):
    lvl = lvl_ref[...]

    def chunk_body(c, st):
        r0 = pl.multiple_of(c * CHUNK, CHUNK)
        q = q_s[pl.ds(r0, CHUNK), :]
        k = k_s[pl.ds(r0, CHUNK), :]
        v = v_s[pl.ds(r0, CHUNK), :]
        gk = gk_s[pl.ds(r0, CHUNK), :]
        br = _cumsum_rows(cmat_ref[...], gk)
        b = br[0:CHUNK]
        row = lax.broadcasted_iota(jnp.int32, (CHUNK, QK), 0)
        a = jnp.zeros((HEADS * CHUNK, CHUNK), F32)
        for l, w in enumerate(HALF_WIDTHS):
            ref = br[(l + 1) * CHUNK:(l + 2) * CHUNK]
            e = jnp.exp(-jnp.abs(b - ref))
            second = (row & w) != 0
            ql = jnp.where(second, q * e, 0.0)
            kl = jnp.where(second, 0.0, k * e)
            g = _dot_nt(_stack_heads(ql).astype(BF16), kl.astype(BF16))
            a = jnp.where(lvl == l, g, a)
        g = _dot_nt(_stack_heads(q).astype(BF16), k.astype(BF16))
        a = jnp.where(lvl == N_LEVELS, g, a)
        a = a.astype(BF16)
        o_inter = _dot_nt(_stack_heads(q * jnp.exp(b)).astype(BF16), st.astype(BF16))
        for h in range(HEADS):
            o_h = _dot(a[h * CHUNK:(h + 1) * CHUNK], v[:, h * DV:(h + 1) * DV].astype(BF16))
            o_s[pl.ds(r0, CHUNK), h * DV:(h + 1) * DV] = o_h + o_inter[h * CHUNK:(h + 1) * CHUNK]
        b_last = b[CHUNK - 1:CHUNK, :]
        return _state_update(st, k * jnp.exp(b_last - b), v, jnp.exp(b_last))

    st_ref[...] = lax.fori_loop(0, MIX_TILE // CHUNK, chunk_body, st_ref[...])


def _mixer_kernel(x_ref, xnext_ref, nw_ref, wa_ref, wb_ref, wgr_ref, wgu_ref, bg_ref, gnw_ref, cw_ref, wout_ref,
                  tri_ref, cmat_ref, lvl_ref, st0_ref, tail0_ref, wup32_ref, wdown32_ref,
                  h1_ref, wup16_ref, wdown16_ref,
                  st_ref, stprev_s, tail_ref, hn_s, hnext_s, grnext_s, q_s, k_s, v_s, gk_s, b_s, qp_s, kp_s, o_s,
                  u_s, gate_s, yconv_s):
    t = pl.program_id(1)

    @pl.when(t == 0)
    def _():
        st_ref[...] = st0_ref[...]
        tail_ref[...] = tail0_ref[...]

    @pl.when(pl.program_id(0) + t == 0)
    def _():
        hnext_s[...] = _rms(x_ref[0], nw_ref[...]).astype(BF16)
        grnext_s[...] = _dot_nt(hnext_s[...], wgr_ref[...])

    wup16_ref[...] = wup32_ref[...].astype(BF16)
    wdown16_ref[...] = wdown32_ref[...].astype(BF16)

    hn_s[...] = hnext_s[...]
    gk = _gate_log_decay(grnext_s[...], wgu_ref[...], bg_ref[...])
    qk = _dot_nt(hn_s[...], wa_ref[0:2 * QK, :])
    q = qk[:, 0:QK] * (DK ** -0.5)
    k = qk[:, QK:2 * QK]
    q_s[...] = q
    k_s[...] = k
    gk_s[...] = gk
    n_wide = MIX_TILE // WIDE_CHUNK
    gk_cols = jnp.concatenate([gk[c * WIDE_CHUNK:(c + 1) * WIDE_CHUNK] for c in range(n_wide)], axis=1)
    b_cols = _cumsum_rows(tri_ref[...], gk_cols)
    b = jnp.concatenate([b_cols[:, c * QK:(c + 1) * QK] for c in range(n_wide)], axis=0)
    kp = k * jnp.exp(-b)
    b_s[...] = b
    qp_s[...] = q * jnp.exp(b)
    kp_s[...] = kp
    n_bad = jnp.sum(jnp.where(jnp.abs(kp) <= WIDE_KEY_BOUND, 0.0, 1.0))
    wide_ok = n_bad == 0.0

    cc = _dot_nt(hn_s[...], wb_ref[CONV_CH:2 * CONV_CH, :])
    cx = _dot_nt(hn_s[...], wb_ref[2 * CONV_CH:3 * CONV_CH, :])
    hnext = _rms(xnext_ref[0], nw_ref[...]).astype(BF16)
    hnext_s[...] = hnext
    grnext_s[...] = _dot_nt(hnext, wgr_ref[...])
    cb = _dot_nt(hn_s[...], wb_ref[0:CONV_CH, :])
    u = cc * cx
    u_s[0:SUBLANES, :] = tail_ref[...]
    u_s[SUBLANES:SUBLANES + MIX_TILE, :] = u
    tail_ref[...] = u[MIX_TILE - SUBLANES:MIX_TILE, :]
    u1 = u_s[SUBLANES - 1:SUBLANES - 1 + MIX_TILE, :]
    u2 = u_s[SUBLANES - 2:SUBLANES - 2 + MIX_TILE, :]
    yconv_s[...] = (cb * (cw_ref[0:1, :] * u2 + cw_ref[1:2, :] * u1 + cw_ref[2:3, :] * u)).astype(BF16)
    gate_s[...] = _silu(_dot_nt(hn_s[...], wa_ref[2 * QK + GW:2 * QK + 2 * GW, :]))
    v_s[...] = _dot_nt(hn_s[...], wa_ref[2 * QK:2 * QK + GW, :])

    def mix_out():
        gnw = gnw_ref[...]
        y_gla = jnp.concatenate(
            [_rms(o_s[:, h * DV:(h + 1) * DV], gnw) for h in range(HEADS)], axis=1) * gate_s[...]
        mixed = (_dot(yconv_s[...], wout_ref[GW:GW + CONV_CH, :])
                 + _dot(y_gla.astype(BF16), wout_ref[0:GW, :]))
        h1_ref[0] = x_ref[0] + mixed

    stprev_s[...] = st_ref[...]
    _gla_wide(q_s, v_s, b_s, qp_s, kp_s, o_s, st_ref)
    mix_out()

    @pl.when(jnp.logical_not(wide_ok))
    def _():
        st_ref[...] = stprev_s[...]
        _gla_stable(q_s, k_s, v_s, gk_s, o_s, st_ref, cmat_ref, lvl_ref)
        mix_out()


def _mlp_kernel(h_ref, nw_ref, wup_ref, wdown_ref, fw_ref, out_ref):
    blocks = [slice(r, r + MLP_ROWS) for r in range(0, MLP_TILE, MLP_ROWS)]
    hn = [_rms(h_ref[rows, :], nw_ref[...]).astype(BF16) for rows in blocks]
    acc = [jnp.zeros((MLP_ROWS, D_MODEL), F32) for _ in blocks]
    for f in range(0, D_FF, FF_CHUNK):
        for i in range(len(blocks)):
            up = _dot(hn[i], wup_ref[:, f:f + FF_CHUNK])
            act = jnp.square(jnp.maximum(up, 0.0)).astype(BF16)
            acc[i] = acc[i] + _dot(act, wdown_ref[f:f + FF_CHUNK, :])
    for i, rows in enumerate(blocks):
        out_ref[rows, :] = _rms(h_ref[rows, :] + acc[i], fw_ref[...])


def _tri3(n):
    tri = np.tril(np.ones((n, n), np.float32))
    return np.concatenate([tri, tri, tri], axis=1)


def _cumsum_and_ref_matrix():
    tri = np.tril(np.ones((CHUNK, CHUNK), np.float32))
    blocks = [tri]
    idx = np.arange(CHUNK)
    for w in HALF_WIDTHS:
        ref_row = (idx // (2 * w)) * (2 * w) + w - 1
        blocks.append(tri[ref_row])
    return np.concatenate(blocks, axis=0)


def _level_matrix():
    i = np.arange(CHUNK)[:, None]
    j = np.arange(CHUNK)[None, :]
    lvl = np.full((CHUNK, CHUNK), -1, np.int32)
    for l, w in enumerate(HALF_WIDTHS):
        same = (i // (2 * w)) == (j // (2 * w))
        lvl[same & ((i & w) != 0) & ((j & w) == 0)] = l
    lvl[i == j] = N_LEVELS
    return np.tile(lvl, (HEADS, 1))


def _const_spec(shape):
    return pl.BlockSpec(shape, lambda *_: (0,) * len(shape))


def kernel(x, meta_tokens, norm_mix_w, w_in, w_gate_up, b_gate, gla_norm_w, conv_w, w_out,
           norm_mlp_w, w_up, w_down, norm_final_w):
    batch, seq, _ = x.shape
    assert seq % MIX_TILE == 0 and (batch * seq) % MLP_TILE == 0
    assert norm_mix_w.shape[0] == 1, "single layer"

    bg = b_gate[0].reshape(1, QK)
    nw_mix = norm_mix_w[0].reshape(1, D_MODEL)
    gnw = gla_norm_w[0].reshape(1, DV)
    cw = conv_w[0]
    nw_mlp = norm_mlp_w[0].reshape(1, D_MODEL)
    fw = norm_final_w.reshape(1, D_MODEL)

    cmat = _cumsum_and_ref_matrix()
    cmat3 = jnp.asarray(np.concatenate([cmat, cmat, cmat], axis=1), BF16)
    tri_meta = jnp.asarray(_tri3(N_META), BF16)
    tri_wide = jnp.asarray(_tri3(WIDE_CHUNK), BF16)
    lvl = jnp.asarray(_level_matrix())

    w_in_t = jnp.swapaxes(w_in, 1, 2)[0]
    conv_row0 = GLA_ROWS + RANK
    assert GLA_ROWS == 3 * CONV_CH and GLA_ROWS % PREP_SLAB == 0 and D_MODEL % PREP_SLAB == 0
    n_slabs = GLA_ROWS // PREP_SLAB
    last_out_slab = D_MODEL // PREP_SLAB - 1
    slab = (PREP_SLAB, D_MODEL)
    wa, wb, w_gr6, wout, wgu6, st0, tail0 = pl.pallas_call(
        _prep_meta_kernel,
        grid=(n_slabs,),
        in_specs=[
            _const_spec((N_META, D_MODEL)),
            _const_spec((1, D_MODEL)),
            pl.BlockSpec(slab, lambda i: (i, 0)),
            pl.BlockSpec((pl.Element(PREP_SLAB), pl.Element(D_MODEL)),
                         lambda i: (pl.multiple_of(conv_row0 + i * PREP_SLAB, RANK), 0)),
            pl.BlockSpec((pl.Element(RANK), pl.Element(D_MODEL)), lambda i: (GLA_ROWS, 0)),
            pl.BlockSpec((1,) + slab, lambda i: (0, jnp.minimum(i, last_out_slab), 0)),
            _const_spec((1, RANK, QK)),
            _const_spec((1, QK)),
            _const_spec(tri_meta.shape),
        ],
        out_specs=(
            pl.BlockSpec(slab, lambda i: (i, 0)),
            pl.BlockSpec(slab, lambda i: (i, 0)),
            _const_spec((LANES, D_MODEL)),
            pl.BlockSpec(slab, lambda i: (jnp.minimum(i, last_out_slab), 0)),
            _const_spec((LANES, QK)),
            _const_spec((DV, QK)),
            _const_spec((SUBLANES, CONV_CH)),
        ),
        out_shape=(
            jax.ShapeDtypeStruct((GLA_ROWS, D_MODEL), BF16),
            jax.ShapeDtypeStruct((3 * CONV_CH, D_MODEL), BF16),
            jax.ShapeDtypeStruct((LANES, D_MODEL), BF16),
            jax.ShapeDtypeStruct((D_MODEL, D_MODEL), BF16),
            jax.ShapeDtypeStruct((LANES, QK), BF16),
            jax.ShapeDtypeStruct((DV, QK), F32),
            jax.ShapeDtypeStruct((SUBLANES, CONV_CH), F32),
        ),
        scratch_shapes=[
            pltpu.VMEM((N_META, D_MODEL), BF16),
            pltpu.VMEM((2 * n_slabs, N_META, PREP_SLAB), F32),
        ],
        compiler_params=pltpu.CompilerParams(
            vmem_limit_bytes=VMEM_LIMIT, dimension_semantics=("arbitrary",)),
        name="weight_prep_meta_state",
    )(meta_tokens, nw_mix, w_in_t, w_in_t, w_in_t, w_out, w_gate_up, bg, tri_meta)

    n_tiles = seq // MIX_TILE

    def next_tile(b, t):
        nxt = jnp.minimum(b * n_tiles + t + 1, batch * n_tiles - 1)
        return nxt // n_tiles, nxt % n_tiles, 0

    n_steps = batch * n_tiles
    assert D_MODEL % n_steps == 0 and D_FF % n_steps == 0
    up_slab, down_slab = D_MODEL // n_steps, D_FF // n_steps

    h1, wup, wdown = pl.pallas_call(
        _mixer_kernel,
        grid=(batch, n_tiles),
        in_specs=[
            pl.BlockSpec((1, MIX_TILE, D_MODEL), lambda b, t: (b, t, 0)),
            pl.BlockSpec((1, MIX_TILE, D_MODEL), next_tile),
            _const_spec((1, D_MODEL)),
            _const_spec(wa.shape),
            _const_spec(wb.shape),
            _const_spec(w_gr6.shape),
            _const_spec(wgu6.shape),
            _const_spec((1, QK)),
            _const_spec((1, DV)),
            _const_spec(cw.shape),
            _const_spec(wout.shape),
            _const_spec(tri_wide.shape),
            _const_spec(cmat3.shape),
            _const_spec(lvl.shape),
            _const_spec((DV, QK)),
            _const_spec((SUBLANES, CONV_CH)),
            pl.BlockSpec((up_slab, D_FF), lambda b, t: (b * n_tiles + t, 0)),
            pl.BlockSpec((down_slab, D_MODEL), lambda b, t: (b * n_tiles + t, 0)),
        ],
        out_specs=(
            pl.BlockSpec((1, MIX_TILE, D_MODEL), lambda b, t: (b, t, 0)),
            pl.BlockSpec((up_slab, D_FF), lambda b, t: (b * n_tiles + t, 0)),
            pl.BlockSpec((down_slab, D_MODEL), lambda b, t: (b * n_tiles + t, 0)),
        ),
        out_shape=(
            jax.ShapeDtypeStruct((batch, seq, D_MODEL), F32),
            jax.ShapeDtypeStruct((D_MODEL, D_FF), BF16),
            jax.ShapeDtypeStruct((D_FF, D_MODEL), BF16),
        ),
        scratch_shapes=[
            pltpu.VMEM((DV, QK), F32),
            pltpu.VMEM((DV, QK), F32),
            pltpu.VMEM((SUBLANES, CONV_CH), F32),
            pltpu.VMEM((MIX_TILE, D_MODEL), BF16),
            pltpu.VMEM((MIX_TILE, D_MODEL), BF16),
            pltpu.VMEM((MIX_TILE, LANES), F32),
            pltpu.VMEM((MIX_TILE, QK), F32),
            pltpu.VMEM((MIX_TILE, QK), F32),
            pltpu.VMEM((MIX_TILE, GW), F32),
            pltpu.VMEM((MIX_TILE, QK), F32),
            pltpu.VMEM((MIX_TILE, QK), F32),
            pltpu.VMEM((MIX_TILE, QK), F32),
            pltpu.VMEM((MIX_TILE, QK), F32),
            pltpu.VMEM((MIX_TILE, GW), F32),
            pltpu.VMEM((MIX_TILE + SUBLANES, CONV_CH), F32),
            pltpu.VMEM((MIX_TILE, GW), F32),
            pltpu.VMEM((MIX_TILE, CONV_CH), BF16),
        ],
        compiler_params=pltpu.CompilerParams(
            vmem_limit_bytes=VMEM_LIMIT, dimension_semantics=("arbitrary", "arbitrary")),
        name="gla_conv_mixer",
    )(x, x, nw_mix, wa, wb, w_gr6, wgu6, bg, gnw, cw, wout, tri_wide, cmat3, lvl, st0, tail0,
      w_up[0], w_down[0])

    rows = batch * seq
    out = pl.pallas_call(
        _mlp_kernel,
        grid=(rows // MLP_TILE,),
        in_specs=[
            pl.BlockSpec((MLP_TILE, D_MODEL), lambda i: (i, 0)),
            _const_spec((1, D_MODEL)),
            _const_spec(wup.shape),
            _const_spec(wdown.shape),
            _const_spec((1, D_MODEL)),
        ],
        out_specs=pl.BlockSpec((MLP_TILE, D_MODEL), lambda i: (i, 0)),
        out_shape=jax.ShapeDtypeStruct((rows, D_MODEL), F32),
        compiler_params=pltpu.CompilerParams(
            vmem_limit_bytes=VMEM_LIMIT, dimension_semantics=("arbitrary",)),
        name="relu2_mlp_final_norm",
    )(h1.reshape(rows, D_MODEL), nw_mlp, wup, wdown, fw)
    return out.reshape(batch, seq, D_MODEL)
```

```python
import numpy as np
import jax
import jax.numpy as jnp
from jax import lax
from jax.experimental import pallas as pl
from jax.experimental.pallas import tpu as pltpu

D_MODEL = 1024
N_META = 16
HEADS = 4
DK = 64
DV = 128
QK = HEADS * DK
GW = HEADS * DV
RANK = 16
GATE_NORM = 16.0
CONV_CH = 512
D_FF = 4096
GLA_ROWS = 2 * QK + 2 * GW
PROJ_WIDTH = GLA_ROWS + RANK + 3 * CONV_CH
EPS = 1e-6

LANES = 128
SUBLANES = 8
VMEM_LIMIT = 56 * 1024 * 1024

MIX_TILE = 512
WIDE_CHUNK = 128
CHUNK = 64
HALF_WIDTHS = (32, 16, 8, 4, 2, 1)
N_LEVELS = len(HALF_WIDTHS)
WIDE_KEY_BOUND = 1e18
GATE_TERMS = 6
PREP_SLAB = 512
MLP_TILE = 1024
MLP_ROWS = 512
FF_CHUNK = 1024

BF16 = jnp.bfloat16
F32 = jnp.float32


def _rms(x, w):
    ms = jnp.mean(x * x, axis=-1, keepdims=True)
    return x * lax.rsqrt(ms + EPS) * w


def _log_sigmoid(z):
    return -(jnp.maximum(-z, 0.0) + jnp.log1p(jnp.exp(-jnp.abs(z))))


def _silu(g):
    return g * (0.5 * (1.0 + jnp.tanh(0.5 * g)))


def _dot(a, b):
    return jnp.dot(a, b, preferred_element_type=F32)


def _dot_nt(a, b):
    return lax.dot_general(a, b, (((1,), (1,)), ((), ())), preferred_element_type=F32)


def _split3(x):
    hi = x.astype(BF16)
    r1 = x - hi.astype(F32)
    mid = r1.astype(BF16)
    lo = (r1 - mid.astype(F32)).astype(BF16)
    return hi, mid, lo


def _stack_heads(x):
    lane_head = lax.broadcasted_iota(jnp.int32, x.shape, 1) >> (DK.bit_length() - 1)
    return jnp.concatenate([jnp.where(lane_head == h, x, 0.0) for h in range(HEADS)], axis=0)


def _rows_by_head(v):
    return jnp.concatenate([v[:, h * DV:(h + 1) * DV] for h in range(HEADS)], axis=0)


def _gate_log_decay(gr6, wgu6, bg):
    hi, mid, lo = _split3(gr6)
    group = lax.broadcasted_iota(jnp.int32, gr6.shape, 1) >> (RANK.bit_length() - 1)
    lhs = jnp.where(group < 3, hi, jnp.where(group < 5, mid, lo))
    z = _dot(lhs, wgu6) + bg
    return _log_sigmoid(z) / GATE_NORM


def _cumsum_rows(tri3, gk):
    return _dot(tri3, jnp.concatenate(_split3(gk), axis=0))


def _state_increment(kdec, v):
    return _dot(_rows_by_head(v).T.astype(BF16), _stack_heads(kdec).astype(BF16))


def _state_update(st, kdec, v, decay_last):
    return st * decay_last + _state_increment(kdec, v)


def _prep_meta_kernel(meta_ref, nw_ref, wa_ref, wb_ref, wg_ref, wout_ref, wgu_ref, bg_ref, tri_ref,
                      wa16_ref, wb16_ref, wgr6_ref, wout16_ref, wgu6_ref, st_ref, tail_ref,
                      hn_s, proj_s):
    i = pl.program_id(0)
    n_a = GLA_ROWS // PREP_SLAB

    @pl.when(i == 0)
    def _():
        hn_s[...] = _rms(meta_ref[...], nw_ref[...]).astype(BF16)
        g_hi, g_mid, g_lo = _split3(wgu_ref[0])
        zeros = jnp.zeros((LANES - GATE_TERMS * RANK, QK), BF16)
        wgu6_ref[...] = jnp.concatenate([g_hi, g_mid, g_lo, g_hi, g_mid, g_hi, zeros], axis=0)
        gr = wg_ref[...].astype(BF16)
        wgr6_ref[...] = jnp.concatenate(
            [gr] * GATE_TERMS + [jnp.zeros((LANES - GATE_TERMS * RANK, D_MODEL), BF16)], axis=0)

    wa = wa_ref[...].astype(BF16)
    wb = wb_ref[...].astype(BF16)
    wa16_ref[...] = wa
    wb16_ref[...] = wb
    wout16_ref[...] = wout_ref[0].astype(BF16)
    proj_s[i] = _dot_nt(hn_s[...], wa)
    proj_s[n_a + i] = _dot_nt(hn_s[...], wb)

    @pl.when(i == pl.num_programs(0) - 1)
    def _():
        proj = jnp.concatenate([proj_s[j] for j in range(2 * n_a)], axis=1)
        k = proj[:, QK:2 * QK]
        v = proj[:, 2 * QK:2 * QK + GW]
        cc = proj[:, GLA_ROWS + CONV_CH:GLA_ROWS + 2 * CONV_CH]
        cx = proj[:, GLA_ROWS + 2 * CONV_CH:GLA_ROWS + 3 * CONV_CH]
        gk = _gate_log_decay(_dot_nt(hn_s[...], wgr6_ref[...]), wgu6_ref[...], bg_ref[...])
        b = _cumsum_rows(tri_ref[...], gk)
        b_last = b[N_META - 1:N_META, :]
        st_ref[...] = _state_update(jnp.zeros((DV, QK), F32), k * jnp.exp(b_last - b), v, jnp.exp(b_last))
        u = cc * cx
        tail_ref[...] = u[N_META - SUBLANES:N_META, :]


def _wide_chunks():
    return [slice(c * WIDE_CHUNK, (c + 1) * WIDE_CHUNK) for c in range(MIX_TILE // WIDE_CHUNK)]


def _wide_states(v_s, b_s, kp_s, st_ref):
    chunks = _wide_chunks()
    decay_last = [jnp.exp(b_s[rows.stop - 1:rows.stop, :]) for rows in chunks]
    kv = [_state_increment(kp_s[rows, :] * decay_last[c], v_s[rows, :]) for c, rows in enumerate(chunks)]
    states = [st_ref[...]]
    for c in range(len(chunks)):
        states.append(states[c] * decay_last[c] + kv[c])
    st_ref[...] = states[-1]
    return states[:-1]


def _wide_scores(qp_s, kp_s, states):
    return [_dot_nt(_stack_heads(qp_s[rows, :]).astype(BF16),
                    jnp.concatenate([kp_s[rows, :].astype(BF16), states[c].astype(BF16)], axis=0))
            for c, rows in enumerate(_wide_chunks())]


def _wide_outputs(sc, v_s, o_s):
    row = lax.broadcasted_iota(jnp.int32, (HEADS * WIDE_CHUNK, WIDE_CHUNK), 0) & (WIDE_CHUNK - 1)
    col = lax.broadcasted_iota(jnp.int32, (HEADS * WIDE_CHUNK, WIDE_CHUNK), 1)
    causal = row >= col
    for c, rows in enumerate(_wide_chunks()):
        scores = jnp.where(causal, sc[c][:, 0:WIDE_CHUNK], 0.0).astype(BF16)
        for h in range(HEADS):
            hr = slice(h * WIDE_CHUNK, (h + 1) * WIDE_CHUNK)
            o_h = _dot(scores[hr], v_s[rows, h * DV:(h + 1) * DV].astype(BF16))
            o_s[rows, h * DV:(h + 1) * DV] = o_h + sc[c][hr, WIDE_CHUNK:WIDE_CHUNK + DV]


def _gla_stable(q_s, k_s, v_s, gk_s, o_s, st_ref, cmat_ref, lvl_ref):
    lvl = lvl_ref[...]

    def chunk_body(c, st):
        r0 = pl.multiple_of(c * CHUNK, CHUNK)
        q = q_s[pl.ds(r0, CHUNK), :]
        k = k_s[pl.ds(r0, CHUNK), :]
        v = v_s[pl.ds(r0, CHUNK), :]
        gk = gk_s[pl.ds(r0, CHUNK), :]
        br = _cumsum_rows(cmat_ref[...], gk)
        b = br[0:CHUNK]
        row = lax.broadcasted_iota(jnp.int32, (CHUNK, QK), 0)
        a = jnp.zeros((HEADS * CHUNK, CHUNK), F32)
        for l, w in enumerate(HALF_WIDTHS):
            ref = br[(l + 1) * CHUNK:(l + 2) * CHUNK]
            e = jnp.exp(-jnp.abs(b - ref))
            second = (row & w) != 0
            ql = jnp.where(second, q * e, 0.0)
            kl = jnp.where(second, 0.0, k * e)
            g = _dot_nt(_stack_heads(ql).astype(BF16), kl.astype(BF16))
            a = jnp.where(lvl == l, g, a)
        g = _dot_nt(_stack_heads(q).astype(BF16), k.astype(BF16))
        a = jnp.where(lvl == N_LEVELS, g, a)
        a = a.astype(BF16)
        o_inter = _dot_nt(_stack_heads(q * jnp.exp(b)).astype(BF16), st.astype(BF16))
        for h in range(HEADS):
            o_h = _dot(a[h * CHUNK:(h + 1) * CHUNK], v[:, h * DV:(h + 1) * DV].astype(BF16))
            o_s[pl.ds(r0, CHUNK), h * DV:(h + 1) * DV] = o_h + o_inter[h * CHUNK:(h + 1) * CHUNK]
        b_last = b[CHUNK - 1:CHUNK, :]
        return _state_update(st, k * jnp.exp(b_last - b), v, jnp.exp(b_last))

    st_ref[...] = lax.fori_loop(0, MIX_TILE // CHUNK, chunk_body, st_ref[...])


def _mixer_kernel(x_ref, xnext_ref, nw_ref, wa_ref, wb_ref, wgr_ref, wgu_ref, bg_ref, gnw_ref, cw_ref, wout_ref,
                  tri_ref, cmat_ref, lvl_ref, st0_ref, tail0_ref, wup32_ref, wdown32_ref,
                  h1_ref, wup16_ref, wdown16_ref,
                  st_ref, stprev_s, tail_ref, hn_s, hnext_s, grnext_s, q_s, k_s, v_s, gk_s, b_s, qp_s, kp_s, o_s,
                  u_s, gate_s, yconv_s):
    t = pl.program_id(1)

    @pl.when(t == 0)
    def _():
        st_ref[...] = st0_ref[...]
        tail_ref[...] = tail0_ref[...]

    @pl.when(pl.program_id(0) + t == 0)
    def _():
        hnext_s[...] = _rms(x_ref[0], nw_ref[...]).astype(BF16)
        grnext_s[...] = _dot_nt(hnext_s[...], wgr_ref[...])

    wup16_ref[...] = wup32_ref[...].astype(BF16)
    wdown16_ref[...] = wdown32_ref[...].astype(BF16)

    hn_s[...] = hnext_s[...]
    gk = _gate_log_decay(grnext_s[...], wgu_ref[...], bg_ref[...])
    qk = _dot_nt(hn_s[...], wa_ref[0:2 * QK, :])
    q = qk[:, 0:QK] * (DK ** -0.5)
    k = qk[:, QK:2 * QK]
    q_s[...] = q
    k_s[...] = k
    gk_s[...] = gk
    n_wide = MIX_TILE // WIDE_CHUNK
    gk_cols = jnp.concatenate([gk[c * WIDE_CHUNK:(c + 1) * WIDE_CHUNK] for c in range(n_wide)], axis=1)
    b_cols = _cumsum_rows(tri_ref[...], gk_cols)
    b = jnp.concatenate([b_cols[:, c * QK:(c + 1) * QK] for c in range(n_wide)], axis=0)
    kp = k * jnp.exp(-b)
    b_s[...] = b
    qp_s[...] = q * jnp.exp(b)
    kp_s[...] = kp
    n_bad = jnp.sum(jnp.where(jnp.abs(kp) <= WIDE_KEY_BOUND, 0.0, 1.0))
    wide_ok = n_bad == 0.0
    stprev_s[...] = st_ref[...]
    v_s[...] = _dot_nt(hn_s[...], wa_ref[2 * QK:2 * QK + GW, :])
    cc = _dot_nt(hn_s[...], wb_ref[CONV_CH:2 * CONV_CH, :])
    states = _wide_states(v_s, b_s, kp_s, st_ref)
    cx = _dot_nt(hn_s[...], wb_ref[2 * CONV_CH:3 * CONV_CH, :])
    sc = _wide_scores(qp_s, kp_s, states)
    hnext = _rms(xnext_ref[0], nw_ref[...]).astype(BF16)
    hnext_s[...] = hnext
    grnext_s[...] = _dot_nt(hnext, wgr_ref[...])
    cb = _dot_nt(hn_s[...], wb_ref[0:CONV_CH, :])
    _wide_outputs(sc, v_s, o_s)
    gate_s[...] = _silu(_dot_nt(hn_s[...], wa_ref[2 * QK + GW:2 * QK + 2 * GW, :]))

    u = cc * cx
    u_s[0:SUBLANES, :] = tail_ref[...]
    u_s[SUBLANES:SUBLANES + MIX_TILE, :] = u
    tail_ref[...] = u[MIX_TILE - SUBLANES:MIX_TILE, :]
    u1 = u_s[SUBLANES - 1:SUBLANES - 1 + MIX_TILE, :]
    u2 = u_s[SUBLANES - 2:SUBLANES - 2 + MIX_TILE, :]
    yconv_s[...] = (cb * (cw_ref[0:1, :] * u2 + cw_ref[1:2, :] * u1 + cw_ref[2:3, :] * u)).astype(BF16)

    def mix_out():
        gnw = gnw_ref[...]
        y_gla = jnp.concatenate(
            [_rms(o_s[:, h * DV:(h + 1) * DV], gnw) for h in range(HEADS)], axis=1) * gate_s[...]
        mixed = (_dot(yconv_s[...], wout_ref[GW:GW + CONV_CH, :])
                 + _dot(y_gla.astype(BF16), wout_ref[0:GW, :]))
        h1_ref[0] = x_ref[0] + mixed

    mix_out()

    @pl.when(jnp.logical_not(wide_ok))
    def _():
        st_ref[...] = stprev_s[...]
        _gla_stable(q_s, k_s, v_s, gk_s, o_s, st_ref, cmat_ref, lvl_ref)
        mix_out()


def _mlp_kernel(h_ref, nw_ref, wup_ref, wdown_ref, fw_ref, out_ref):
    blocks = [slice(r, r + MLP_ROWS) for r in range(0, MLP_TILE, MLP_ROWS)]
    hn = [_rms(h_ref[rows, :], nw_ref[...]).astype(BF16) for rows in blocks]
    acc = [jnp.zeros((MLP_ROWS, D_MODEL), F32) for _ in blocks]
    for f in range(0, D_FF, FF_CHUNK):
        for i in range(len(blocks)):
            up = _dot(hn[i], wup_ref[:, f:f + FF_CHUNK])
            act = jnp.square(jnp.maximum(up, 0.0)).astype(BF16)
            acc[i] = acc[i] + _dot(act, wdown_ref[f:f + FF_CHUNK, :])
    for i, rows in enumerate(blocks):
        out_ref[rows, :] = _rms(h_ref[rows, :] + acc[i], fw_ref[...])


def _tri3(n):
    tri = np.tril(np.ones((n, n), np.float32))
    return np.concatenate([tri, tri, tri], axis=1)


def _cumsum_and_ref_matrix():
    tri = np.tril(np.ones((CHUNK, CHUNK), np.float32))
    blocks = [tri]
    idx = np.arange(CHUNK)
    for w in HALF_WIDTHS:
        ref_row = (idx // (2 * w)) * (2 * w) + w - 1
        blocks.append(tri[ref_row])
    return np.concatenate(blocks, axis=0)


def _level_matrix():
    i = np.arange(CHUNK)[:, None]
    j = np.arange(CHUNK)[None, :]
    lvl = np.full((CHUNK, CHUNK), -1, np.int32)
    for l, w in enumerate(HALF_WIDTHS):
        same = (i // (2 * w)) == (j // (2 * w))
        lvl[same & ((i & w) != 0) & ((j & w) == 0)] = l
    lvl[i == j] = N_LEVELS
    return np.tile(lvl, (HEADS, 1))


def _const_spec(shape):
    return pl.BlockSpec(shape, lambda *_: (0,) * len(shape))


def kernel(x, meta_tokens, norm_mix_w, w_in, w_gate_up, b_gate, gla_norm_w, conv_w, w_out,
           norm_mlp_w, w_up, w_down, norm_final_w):
    batch, seq, _ = x.shape
    assert seq % MIX_TILE == 0 and (batch * seq) % MLP_TILE == 0
    assert norm_mix_w.shape[0] == 1, "single layer"

    bg = b_gate[0].reshape(1, QK)
    nw_mix = norm_mix_w[0].reshape(1, D_MODEL)
    gnw = gla_norm_w[0].reshape(1, DV)
    cw = conv_w[0]
    nw_mlp = norm_mlp_w[0].reshape(1, D_MODEL)
    fw = norm_final_w.reshape(1, D_MODEL)

    cmat = _cumsum_and_ref_matrix()
    cmat3 = jnp.asarray(np.concatenate([cmat, cmat, cmat], axis=1), BF16)
    tri_meta = jnp.asarray(_tri3(N_META), BF16)
    tri_wide = jnp.asarray(_tri3(WIDE_CHUNK), BF16)
    lvl = jnp.asarray(_level_matrix())

    w_in_t = jnp.swapaxes(w_in, 1, 2)[0]
    conv_row0 = GLA_ROWS + RANK
    assert GLA_ROWS == 3 * CONV_CH and GLA_ROWS % PREP_SLAB == 0 and D_MODEL % PREP_SLAB == 0
    n_slabs = GLA_ROWS // PREP_SLAB
    last_out_slab = D_MODEL // PREP_SLAB - 1
    slab = (PREP_SLAB, D_MODEL)
    wa, wb, w_gr6, wout, wgu6, st0, tail0 = pl.pallas_call(
        _prep_meta_kernel,
        grid=(n_slabs,),
        in_specs=[
            _const_spec((N_META, D_MODEL)),
            _const_spec((1, D_MODEL)),
            pl.BlockSpec(slab, lambda i: (i, 0)),
            pl.BlockSpec((pl.Element(PREP_SLAB), pl.Element(D_MODEL)),
                         lambda i: (pl.multiple_of(conv_row0 + i * PREP_SLAB, RANK), 0)),
            pl.BlockSpec((pl.Element(RANK), pl.Element(D_MODEL)), lambda i: (GLA_ROWS, 0)),
            pl.BlockSpec((1,) + slab, lambda i: (0, jnp.minimum(i, last_out_slab), 0)),
            _const_spec((1, RANK, QK)),
            _const_spec((1, QK)),
            _const_spec(tri_meta.shape),
        ],
        out_specs=(
            pl.BlockSpec(slab, lambda i: (i, 0)),
            pl.BlockSpec(slab, lambda i: (i, 0)),
            _const_spec((LANES, D_MODEL)),
            pl.BlockSpec(slab, lambda i: (jnp.minimum(i, last_out_slab), 0)),
            _const_spec((LANES, QK)),
            _const_spec((DV, QK)),
            _const_spec((SUBLANES, CONV_CH)),
        ),
        out_shape=(
            jax.ShapeDtypeStruct((GLA_ROWS, D_MODEL), BF16),
            jax.ShapeDtypeStruct((3 * CONV_CH, D_MODEL), BF16),
            jax.ShapeDtypeStruct((LANES, D_MODEL), BF16),
            jax.ShapeDtypeStruct((D_MODEL, D_MODEL), BF16),
            jax.ShapeDtypeStruct((LANES, QK), BF16),
            jax.ShapeDtypeStruct((DV, QK), F32),
            jax.ShapeDtypeStruct((SUBLANES, CONV_CH), F32),
        ),
        scratch_shapes=[
            pltpu.VMEM((N_META, D_MODEL), BF16),
            pltpu.VMEM((2 * n_slabs, N_META, PREP_SLAB), F32),
        ],
        compiler_params=pltpu.CompilerParams(
            vmem_limit_bytes=VMEM_LIMIT, dimension_semantics=("arbitrary",)),
        name="weight_prep_meta_state",
    )(meta_tokens, nw_mix, w_in_t, w_in_t, w_in_t, w_out, w_gate_up, bg, tri_meta)

    n_tiles = seq // MIX_TILE

    def next_tile(b, t):
        nxt = jnp.minimum(b * n_tiles + t + 1, batch * n_tiles - 1)
        return nxt // n_tiles, nxt % n_tiles, 0

    n_steps = batch * n_tiles
    assert D_MODEL % n_steps == 0 and D_FF % n_steps == 0
    up_slab, down_slab = D_MODEL // n_steps, D_FF // n_steps

    h1, wup, wdown = pl.pallas_call(
        _mixer_kernel,
        grid=(batch, n_tiles),
        in_specs=[
            pl.BlockSpec((1, MIX_TILE, D_MODEL), lambda b, t: (b, t, 0)),
            pl.BlockSpec((1, MIX_TILE, D_MODEL), next_tile),
            _const_spec((1, D_MODEL)),
            _const_spec(wa.shape),
            _const_spec(wb.shape),
            _const_spec(w_gr6.shape),
            _const_spec(wgu6.shape),
            _const_spec((1, QK)),
            _const_spec((1, DV)),
            _const_spec(cw.shape),
            _const_spec(wout.shape),
            _const_spec(tri_wide.shape),
            _const_spec(cmat3.shape),
            _const_spec(lvl.shape),
            _const_spec((DV, QK)),
            _const_spec((SUBLANES, CONV_CH)),
            pl.BlockSpec((up_slab, D_FF), lambda b, t: (b * n_tiles + t, 0)),
            pl.BlockSpec((down_slab, D_MODEL), lambda b, t: (b * n_tiles + t, 0)),
        ],
        out_specs=(
            pl.BlockSpec((1, MIX_TILE, D_MODEL), lambda b, t: (b, t, 0)),
            pl.BlockSpec((up_slab, D_FF), lambda b, t: (b * n_tiles + t, 0)),
            pl.BlockSpec((down_slab, D_MODEL), lambda b, t: (b * n_tiles + t, 0)),
        ),
        out_shape=(
            jax.ShapeDtypeStruct((batch, seq, D_MODEL), F32),
            jax.ShapeDtypeStruct((D_MODEL, D_FF), BF16),
            jax.ShapeDtypeStruct((D_FF, D_MODEL), BF16),
        ),
        scratch_shapes=[
            pltpu.VMEM((DV, QK), F32),
            pltpu.VMEM((DV, QK), F32),
            pltpu.VMEM((SUBLANES, CONV_CH), F32),
            pltpu.VMEM((MIX_TILE, D_MODEL), BF16),
            pltpu.VMEM((MIX_TILE, D_MODEL), BF16),
            pltpu.VMEM((MIX_TILE, LANES), F32),
            pltpu.VMEM((MIX_TILE, QK), F32),
            pltpu.VMEM((MIX_TILE, QK), F32),
            pltpu.VMEM((MIX_TILE, GW), F32),
            pltpu.VMEM((MIX_TILE, QK), F32),
            pltpu.VMEM((MIX_TILE, QK), F32),
            pltpu.VMEM((MIX_TILE, QK), F32),
            pltpu.VMEM((MIX_TILE, QK), F32),
            pltpu.VMEM((MIX_TILE, GW), F32),
            pltpu.VMEM((MIX_TILE + SUBLANES, CONV_CH), F32),
            pltpu.VMEM((MIX_TILE, GW), F32),
            pltpu.VMEM((MIX_TILE, CONV_CH), BF16),
        ],
        compiler_params=pltpu.CompilerParams(
            vmem_limit_bytes=VMEM_LIMIT, dimension_semantics=("arbitrary", "arbitrary")),
        name="gla_conv_mixer",
    )(x, x, nw_mix, wa, wb, w_gr6, wgu6, bg, gnw, cw, wout, tri_wide, cmat3, lvl, st0, tail0,
      w_up[0], w_down[0])

    rows = batch * seq
    out = pl.pallas_call(
        _mlp_kernel,
        grid=(rows // MLP_TILE,),
        in_specs=[
            pl.BlockSpec((MLP_TILE, D_MODEL), lambda i: (i, 0)),
            _const_spec((1, D_MODEL)),
            _const_spec(wup.shape),
            _const_spec(wdown.shape),
            _const_spec((1, D_MODEL)),
        ],
        out_specs=pl.BlockSpec((MLP_TILE, D_MODEL), lambda i: (i, 0)),
        out_shape=jax.ShapeDtypeStruct((rows, D_MODEL), F32),
        compiler_params=pltpu.CompilerParams(
            vmem_limit_bytes=VMEM_LIMIT, dimension_semantics=("arbitrary",)),
        name="relu2_mlp_final_norm",
    )(h1.reshape(rows, D_MODEL), nw_mlp, wup, wdown, fw)
    return out.reshape(batch, seq, D_MODEL)
```

```python
import numpy as np
import jax
import jax.numpy as jnp
from jax import lax
from jax.experimental import pallas as pl
from jax.experimental.pallas import tpu as pltpu

D_MODEL = 1024
N_META = 16
HEADS = 4
DK = 64
DV = 128
QK = HEADS * DK
GW = HEADS * DV
RANK = 16
GATE_NORM = 16.0
CONV_CH = 512
D_FF = 4096
GLA_ROWS = 2 * QK + 2 * GW
PROJ_WIDTH = GLA_ROWS + RANK + 3 * CONV_CH
EPS = 1e-6

LANES = 128
SUBLANES = 8
VMEM_LIMIT = 56 * 1024 * 1024

MIX_TILE = 512
WIDE_CHUNK = 128
CHUNK = 64
HALF_WIDTHS = (32, 16, 8, 4, 2, 1)
N_LEVELS = len(HALF_WIDTHS)
WIDE_KEY_BOUND = 1e18
GATE_TERMS = 6
PREP_SLAB = 512
MLP_TILE = 1024
MLP_ROWS = 512
FF_CHUNK = 1024

BF16 = jnp.bfloat16
F32 = jnp.float32


def _rms(x, w):
    ms = jnp.mean(x * x, axis=-1, keepdims=True)
    return x * lax.rsqrt(ms + EPS) * w


def _log_sigmoid(z):
    return -(jnp.maximum(-z, 0.0) + jnp.log1p(jnp.exp(-jnp.abs(z))))


def _silu(g):
    return g * (0.5 * (1.0 + jnp.tanh(0.5 * g)))


def _dot(a, b):
    return jnp.dot(a, b, preferred_element_type=F32)


def _dot_nt(a, b):
    return lax.dot_general(a, b, (((1,), (1,)), ((), ())), preferred_element_type=F32)


def _split3(x):
    hi = x.astype(BF16)
    r1 = x - hi.astype(F32)
    mid = r1.astype(BF16)
    lo = (r1 - mid.astype(F32)).astype(BF16)
    return hi, mid, lo


def _stack_heads(x):
    lane_head = lax.broadcasted_iota(jnp.int32, x.shape, 1) >> (DK.bit_length() - 1)
    return jnp.concatenate([jnp.where(lane_head == h, x, 0.0) for h in range(HEADS)], axis=0)


def _rows_by_head(v):
    return jnp.concatenate([v[:, h * DV:(h + 1) * DV] for h in range(HEADS)], axis=0)


def _gate_log_decay(gr6, wgu6, bg):
    hi, mid, lo = _split3(gr6)
    group = lax.broadcasted_iota(jnp.int32, gr6.shape, 1) >> (RANK.bit_length() - 1)
    lhs = jnp.where(group < 3, hi, jnp.where(group < 5, mid, lo))
    z = _dot(lhs, wgu6) + bg
    return _log_sigmoid(z) / GATE_NORM


def _cumsum_rows(tri3, gk):
    return _dot(tri3, jnp.concatenate(_split3(gk), axis=0))


def _state_increment(kdec, v):
    return _dot(_rows_by_head(v).T.astype(BF16), _stack_heads(kdec).astype(BF16))


def _state_update(st, kdec, v, decay_last):
    return st * decay_last + _state_increment(kdec, v)


def _prep_meta_kernel(meta_ref, nw_ref, wa_ref, wb_ref, wg_ref, wout_ref, wgu_ref, bg_ref, tri_ref,
                      wa16_ref, wb16_ref, wgr6_ref, wout16_ref, wgu6_ref, st_ref, tail_ref,
                      hn_s, proj_s):
    i = pl.program_id(0)
    n_a = GLA_ROWS // PREP_SLAB

    @pl.when(i == 0)
    def _():
        hn_s[...] = _rms(meta_ref[...], nw_ref[...]).astype(BF16)
        g_hi, g_mid, g_lo = _split3(wgu_ref[0])
        zeros = jnp.zeros((LANES - GATE_TERMS * RANK, QK), BF16)
        wgu6_ref[...] = jnp.concatenate([g_hi, g_mid, g_lo, g_hi, g_mid, g_hi, zeros], axis=0)
        gr = wg_ref[...].astype(BF16)
        wgr6_ref[...] = jnp.concatenate(
            [gr] * GATE_TERMS + [jnp.zeros((LANES - GATE_TERMS * RANK, D_MODEL), BF16)], axis=0)

    wa = wa_ref[...].astype(BF16)
    wb = wb_ref[...].astype(BF16)
    wa16_ref[...] = wa
    wb16_ref[...] = wb
    wout16_ref[...] = wout_ref[0].astype(BF16)
    proj_s[i] = _dot_nt(hn_s[...], wa)
    proj_s[n_a + i] = _dot_nt(hn_s[...], wb)

    @pl.when(i == pl.num_programs(0) - 1)
    def _():
        proj = jnp.concatenate([proj_s[j] for j in range(2 * n_a)], axis=1)
        k = proj[:, QK:2 * QK]
        v = proj[:, 2 * QK:2 * QK + GW]
        cc = proj[:, GLA_ROWS + CONV_CH:GLA_ROWS + 2 * CONV_CH]
        cx = proj[:, GLA_ROWS + 2 * CONV_CH:GLA_ROWS + 3 * CONV_CH]
        gk = _gate_log_decay(_dot_nt(hn_s[...], wgr6_ref[...]), wgu6_ref[...], bg_ref[...])
        b = _cumsum_rows(tri_ref[...], gk)
        b_last = b[N_META - 1:N_META, :]
        st_ref[...] = _state_update(jnp.zeros((DV, QK), F32), k * jnp.exp(b_last - b), v, jnp.exp(b_last))
        u = cc * cx
        tail_ref[...] = u[N_META - SUBLANES:N_META, :]


def _wide_chunks():
    return [slice(c * WIDE_CHUNK, (c + 1) * WIDE_CHUNK) for c in range(MIX_TILE // WIDE_CHUNK)]


def _wide_states(v_s, decay_last, kp_s, st_ref):
    chunks = _wide_chunks()
    kv = [_state_increment(kp_s[rows, :] * decay_last[c], v_s[rows, :]) for c, rows in enumerate(chunks)]
    states = [st_ref[...]]
    for c in range(len(chunks)):
        states.append(states[c] * decay_last[c] + kv[c])
    st_ref[...] = states[-1]
    return states[:-1]


def _wide_scores(qp_s, kp_s, states):
    return [_dot_nt(_stack_heads(qp_s[rows, :]).astype(BF16),
                    jnp.concatenate([kp_s[rows, :].astype(BF16), states[c].astype(BF16)], axis=0))
            for c, rows in enumerate(_wide_chunks())]


def _wide_outputs(sc, v_s, o_s):
    row = lax.broadcasted_iota(jnp.int32, (HEADS * WIDE_CHUNK, WIDE_CHUNK), 0) & (WIDE_CHUNK - 1)
    col = lax.broadcasted_iota(jnp.int32, (HEADS * WIDE_CHUNK, WIDE_CHUNK), 1)
    causal = row >= col
    for c, rows in enumerate(_wide_chunks()):
        scores = jnp.where(causal, sc[c][:, 0:WIDE_CHUNK], 0.0).astype(BF16)
        for h in range(HEADS):
            hr = slice(h * WIDE_CHUNK, (h + 1) * WIDE_CHUNK)
            o_h = _dot(scores[hr], v_s[rows, h * DV:(h + 1) * DV].astype(BF16))
            o_s[rows, h * DV:(h + 1) * DV] = o_h + sc[c][hr, WIDE_CHUNK:WIDE_CHUNK + DV]


def _gla_stable(q_s, k_s, v_s, gk_s, o_s, st_ref, cmat_ref, lvl_ref):
    lvl = lvl_ref[...]

    def chunk_body(c, st):
        r0 = pl.multiple_of(c * CHUNK, CHUNK)
        q = q_s[pl.ds(r0, CHUNK), :]
        k = k_s[pl.ds(r0, CHUNK), :]
        v = v_s[pl.ds(r0, CHUNK), :]
        gk = gk_s[pl.ds(r0, CHUNK), :]
        br = _cumsum_rows(cmat_ref[...], gk)
        b = br[0:CHUNK]
        row = lax.broadcasted_iota(jnp.int32, (CHUNK, QK), 0)
        a = jnp.zeros((HEADS * CHUNK, CHUNK), F32)
        for l, w in enumerate(HALF_WIDTHS):
            ref = br[(l + 1) * CHUNK:(l + 2) * CHUNK]
            e = jnp.exp(-jnp.abs(b - ref))
            second = (row & w) != 0
            ql = jnp.where(second, q * e, 0.0)
            kl = jnp.where(second, 0.0, k * e)
            g = _dot_nt(_stack_heads(ql).astype(BF16), kl.astype(BF16))
            a = jnp.where(lvl == l, g, a)
        g = _dot_nt(_stack_heads(q).astype(BF16), k.astype(BF16))
        a = jnp.where(lvl == N_LEVELS, g, a)
        a = a.astype(BF16)
        o_inter = _dot_nt(_stack_heads(q * jnp.exp(b)).astype(BF16), st.astype(BF16))
        for h in range(HEADS):
            o_h = _dot(a[h * CHUNK:(h + 1) * CHUNK], v[:, h * DV:(h + 1) * DV].astype(BF16))
            o_s[pl.ds(r0, CHUNK), h * DV:(h + 1) * DV] = o_h + o_inter[h * CHUNK:(h + 1) * CHUNK]
        b_last = b[CHUNK - 1:CHUNK, :]
        return _state_update(st, k * jnp.exp(b_last - b), v, jnp.exp(b_last))

    st_ref[...] = lax.fori_loop(0, MIX_TILE // CHUNK, chunk_body, st_ref[...])


def _mixer_kernel(x_ref, xnext_ref, nw_ref, wa_ref, wb_ref, wgr_ref, wgu_ref, bg_ref, gnw_ref, cw_ref, wout_ref,
                  tri_ref, cmat_ref, lvl_ref, st0_ref, tail0_ref, wup32_ref, wdown32_ref,
                  h1_ref, wup16_ref, wdown16_ref,
                  st_ref, stprev_s, tail_ref, hn_s, hnext_s, grnext_s, q_s, k_s, v_s, gk_s, qp_s, kp_s, o_s,
                  u_s, gate_s, yconv_s):
    t = pl.program_id(1)

    @pl.when(t == 0)
    def _():
        st_ref[...] = st0_ref[...]
        tail_ref[...] = tail0_ref[...]

    @pl.when(pl.program_id(0) + t == 0)
    def _():
        hnext_s[...] = _rms(x_ref[0], nw_ref[...]).astype(BF16)
        grnext_s[...] = _dot_nt(hnext_s[...], wgr_ref[...])

    wup16_ref[...] = wup32_ref[...].astype(BF16)
    wdown16_ref[...] = wdown32_ref[...].astype(BF16)

    hn_s[...] = hnext_s[...]
    gk = _gate_log_decay(grnext_s[...], wgu_ref[...], bg_ref[...])
    qk = _dot_nt(hn_s[...], wa_ref[0:2 * QK, :])
    q = qk[:, 0:QK] * (DK ** -0.5)
    k = qk[:, QK:2 * QK]
    n_wide = MIX_TILE // WIDE_CHUNK
    gk_cols = jnp.concatenate([gk[c * WIDE_CHUNK:(c + 1) * WIDE_CHUNK] for c in range(n_wide)], axis=1)
    b_cols = _cumsum_rows(tri_ref[...], gk_cols)
    b = jnp.concatenate([b_cols[:, c * QK:(c + 1) * QK] for c in range(n_wide)], axis=0)
    kp = k * jnp.exp(-b)
    decay_last = [jnp.exp(b[rows.stop - 1:rows.stop, :]) for rows in _wide_chunks()]
    qp_s[...] = q * jnp.exp(b)
    kp_s[...] = kp
    n_bad = jnp.sum(jnp.where(jnp.abs(kp) <= WIDE_KEY_BOUND, 0.0, 1.0))
    wide_ok = n_bad == 0.0
    stprev_s[...] = st_ref[...]
    v_s[...] = _dot_nt(hn_s[...], wa_ref[2 * QK:2 * QK + GW, :])
    cc = _dot_nt(hn_s[...], wb_ref[CONV_CH:2 * CONV_CH, :])
    states = _wide_states(v_s, decay_last, kp_s, st_ref)
    cx = _dot_nt(hn_s[...], wb_ref[2 * CONV_CH:3 * CONV_CH, :])
    sc = _wide_scores(qp_s, kp_s, states)
    hnext = _rms(xnext_ref[0], nw_ref[...]).astype(BF16)
    hnext_s[...] = hnext
    grnext_s[...] = _dot_nt(hnext, wgr_ref[...])
    cb = _dot_nt(hn_s[...], wb_ref[0:CONV_CH, :])
    _wide_outputs(sc, v_s, o_s)
    gate_s[...] = _silu(_dot_nt(hn_s[...], wa_ref[2 * QK + GW:2 * QK + 2 * GW, :]))

    u = cc * cx
    u_s[0:SUBLANES, :] = tail_ref[...]
    u_s[SUBLANES:SUBLANES + MIX_TILE, :] = u
    tail_ref[...] = u[MIX_TILE - SUBLANES:MIX_TILE, :]
    u1 = u_s[SUBLANES - 1:SUBLANES - 1 + MIX_TILE, :]
    u2 = u_s[SUBLANES - 2:SUBLANES - 2 + MIX_TILE, :]
    yconv_s[...] = (cb * (cw_ref[0:1, :] * u2 + cw_ref[1:2, :] * u1 + cw_ref[2:3, :] * u)).astype(BF16)

    def mix_out():
        gnw = gnw_ref[...]
        y_gla = jnp.concatenate(
            [_rms(o_s[:, h * DV:(h + 1) * DV], gnw) for h in range(HEADS)], axis=1) * gate_s[...]
        mixed = (_dot(yconv_s[...], wout_ref[GW:GW + CONV_CH, :])
                 + _dot(y_gla.astype(BF16), wout_ref[0:GW, :]))
        h1_ref[0] = x_ref[0] + mixed

    mix_out()

    @pl.when(jnp.logical_not(wide_ok))
    def _():
        st_ref[...] = stprev_s[...]
        qk_again = _dot_nt(hn_s[...], wa_ref[0:2 * QK, :])
        q_s[...] = qk_again[:, 0:QK] * (DK ** -0.5)
        k_s[...] = qk_again[:, QK:2 * QK]
        gk_s[...] = _gate_log_decay(_dot_nt(hn_s[...], wgr_ref[...]), wgu_ref[...], bg_ref[...])
        _gla_stable(q_s, k_s, v_s, gk_s, o_s, st_ref, cmat_ref, lvl_ref)
        mix_out()


def _mlp_kernel(h_ref, nw_ref, wup_ref, wdown_ref, fw_ref, out_ref):
    blocks = [slice(r, r + MLP_ROWS) for r in range(0, MLP_TILE, MLP_ROWS)]
    hn = [_rms(h_ref[rows, :], nw_ref[...]).astype(BF16) for rows in blocks]
    acc = [jnp.zeros((MLP_ROWS, D_MODEL), F32) for _ in blocks]
    for f in range(0, D_FF, FF_CHUNK):
        for i in range(len(blocks)):
            up = _dot(hn[i], wup_ref[:, f:f + FF_CHUNK])
            act = jnp.square(jnp.maximum(up, 0.0)).astype(BF16)
            acc[i] = acc[i] + _dot(act, wdown_ref[f:f + FF_CHUNK, :])
    for i, rows in enumerate(blocks):
        out_ref[rows, :] = _rms(h_ref[rows, :] + acc[i], fw_ref[...])


def _tri3(n):
    tri = np.tril(np.ones((n, n), np.float32))
    return np.concatenate([tri, tri, tri], axis=1)


def _cumsum_and_ref_matrix():
    tri = np.tril(np.ones((CHUNK, CHUNK), np.float32))
    blocks = [tri]
    idx = np.arange(CHUNK)
    for w in HALF_WIDTHS:
        ref_row = (idx // (2 * w)) * (2 * w) + w - 1
        blocks.append(tri[ref_row])
    return np.concatenate(blocks, axis=0)


def _level_matrix():
    i = np.arange(CHUNK)[:, None]
    j = np.arange(CHUNK)[None, :]
    lvl = np.full((CHUNK, CHUNK), -1, np.int32)
    for l, w in enumerate(HALF_WIDTHS):
        same = (i // (2 * w)) == (j // (2 * w))
        lvl[same & ((i & w) != 0) & ((j & w) == 0)] = l
    lvl[i == j] = N_LEVELS
    return np.tile(lvl, (HEADS, 1))


def _const_spec(shape):
    return pl.BlockSpec(shape, lambda *_: (0,) * len(shape))


def kernel(x, meta_tokens, norm_mix_w, w_in, w_gate_up, b_gate, gla_norm_w, conv_w, w_out,
           norm_mlp_w, w_up, w_down, norm_final_w):
    batch, seq, _ = x.shape
    assert seq % MIX_TILE == 0 and (batch * seq) % MLP_TILE == 0
    assert norm_mix_w.shape[0] == 1, "single layer"

    bg = b_gate[0].reshape(1, QK)
    nw_mix = norm_mix_w[0].reshape(1, D_MODEL)
    gnw = gla_norm_w[0].reshape(1, DV)
    cw = conv_w[0]
    nw_mlp = norm_mlp_w[0].reshape(1, D_MODEL)
    fw = norm_final_w.reshape(1, D_MODEL)

    cmat = _cumsum_and_ref_matrix()
    cmat3 = jnp.asarray(np.concatenate([cmat, cmat, cmat], axis=1), BF16)
    tri_meta = jnp.asarray(_tri3(N_META), BF16)
    tri_wide = jnp.asarray(_tri3(WIDE_CHUNK), BF16)
    lvl = jnp.asarray(_level_matrix())

    w_in_t = jnp.swapaxes(w_in, 1, 2)[0]
    conv_row0 = GLA_ROWS + RANK
    assert GLA_ROWS == 3 * CONV_CH and GLA_ROWS % PREP_SLAB == 0 and D_MODEL % PREP_SLAB == 0
    n_slabs = GLA_ROWS // PREP_SLAB
    last_out_slab = D_MODEL // PREP_SLAB - 1
    slab = (PREP_SLAB, D_MODEL)
    wa, wb, w_gr6, wout, wgu6, st0, tail0 = pl.pallas_call(
        _prep_meta_kernel,
        grid=(n_slabs,),
        in_specs=[
            _const_spec((N_META, D_MODEL)),
            _const_spec((1, D_MODEL)),
            pl.BlockSpec(slab, lambda i: (i, 0)),
            pl.BlockSpec((pl.Element(PREP_SLAB), pl.Element(D_MODEL)),
                         lambda i: (pl.multiple_of(conv_row0 + i * PREP_SLAB, RANK), 0)),
            pl.BlockSpec((pl.Element(RANK), pl.Element(D_MODEL)), lambda i: (GLA_ROWS, 0)),
            pl.BlockSpec((1,) + slab, lambda i: (0, jnp.minimum(i, last_out_slab), 0)),
            _const_spec((1, RANK, QK)),
            _const_spec((1, QK)),
            _const_spec(tri_meta.shape),
        ],
        out_specs=(
            pl.BlockSpec(slab, lambda i: (i, 0)),
            pl.BlockSpec(slab, lambda i: (i, 0)),
            _const_spec((LANES, D_MODEL)),
            pl.BlockSpec(slab, lambda i: (jnp.minimum(i, last_out_slab), 0)),
            _const_spec((LANES, QK)),
            _const_spec((DV, QK)),
            _const_spec((SUBLANES, CONV_CH)),
        ),
        out_shape=(
            jax.ShapeDtypeStruct((GLA_ROWS, D_MODEL), BF16),
            jax.ShapeDtypeStruct((3 * CONV_CH, D_MODEL), BF16),
            jax.ShapeDtypeStruct((LANES, D_MODEL), BF16),
            jax.ShapeDtypeStruct((D_MODEL, D_MODEL), BF16),
            jax.ShapeDtypeStruct((LANES, QK), BF16),
            jax.ShapeDtypeStruct((DV, QK), F32),
            jax.ShapeDtypeStruct((SUBLANES, CONV_CH), F32),
        ),
        scratch_shapes=[
            pltpu.VMEM((N_META, D_MODEL), BF16),
            pltpu.VMEM((2 * n_slabs, N_META, PREP_SLAB), F32),
        ],
        compiler_params=pltpu.CompilerParams(
            vmem_limit_bytes=VMEM_LIMIT, dimension_semantics=("arbitrary",)),
        name="weight_prep_meta_state",
    )(meta_tokens, nw_mix, w_in_t, w_in_t, w_in_t, w_out, w_gate_up, bg, tri_meta)

    n_tiles = seq // MIX_TILE

    def next_tile(b, t):
        nxt = jnp.minimum(b * n_tiles + t + 1, batch * n_tiles - 1)
        return nxt // n_tiles, nxt % n_tiles, 0

    n_steps = batch * n_tiles
    assert D_MODEL % n_steps == 0 and D_FF % n_steps == 0
    up_slab, down_slab = D_MODEL // n_steps, D_FF // n_steps

    h1, wup, wdown = pl.pallas_call(
        _mixer_kernel,
        grid=(batch, n_tiles),
        in_specs=[
            pl.BlockSpec((1, MIX_TILE, D_MODEL), lambda b, t: (b, t, 0)),
            pl.BlockSpec((1, MIX_TILE, D_MODEL), next_tile),
            _const_spec((1, D_MODEL)),
            _const_spec(wa.shape),
            _const_spec(wb.shape),
            _const_spec(w_gr6.shape),
            _const_spec(wgu6.shape),
            _const_spec((1, QK)),
            _const_spec((1, DV)),
            _const_spec(cw.shape),
            _const_spec(wout.shape),
            _const_spec(tri_wide.shape),
            _const_spec(cmat3.shape),
            _const_spec(lvl.shape),
            _const_spec((DV, QK)),
            _const_spec((SUBLANES, CONV_CH)),
            pl.BlockSpec((up_slab, D_FF), lambda b, t: (b * n_tiles + t, 0)),
            pl.BlockSpec((down_slab, D_MODEL), lambda b, t: (b * n_tiles + t, 0)),
        ],
        out_specs=(
            pl.BlockSpec((1, MIX_TILE, D_MODEL), lambda b, t: (b, t, 0)),
            pl.BlockSpec((up_slab, D_FF), lambda b, t: (b * n_tiles + t, 0)),
            pl.BlockSpec((down_slab, D_MODEL), lambda b, t: (b * n_tiles + t, 0)),
        ),
        out_shape=(
            jax.ShapeDtypeStruct((batch, seq, D_MODEL), F32),
            jax.ShapeDtypeStruct((D_MODEL, D_FF), BF16),
            jax.ShapeDtypeStruct((D_FF, D_MODEL), BF16),
        ),
        scratch_shapes=[
            pltpu.VMEM((DV, QK), F32),
            pltpu.VMEM((DV, QK), F32),
            pltpu.VMEM((SUBLANES, CONV_CH), F32),
            pltpu.VMEM((MIX_TILE, D_MODEL), BF16),
            pltpu.VMEM((MIX_TILE, D_MODEL), BF16),
            pltpu.VMEM((MIX_TILE, LANES), F32),
            pltpu.VMEM((MIX_TILE, QK), F32),
            pltpu.VMEM((MIX_TILE, QK), F32),
            pltpu.VMEM((MIX_TILE, GW), F32),
            pltpu.VMEM((MIX_TILE, QK), F32),
            pltpu.VMEM((MIX_TILE, QK), F32),
            pltpu.VMEM((MIX_TILE, QK), F32),
            pltpu.VMEM((MIX_TILE, GW), F32),
            pltpu.VMEM((MIX_TILE + SUBLANES, CONV_CH), F32),
            pltpu.VMEM((MIX_TILE, GW), F32),
            pltpu.VMEM((MIX_TILE, CONV_CH), BF16),
        ],
        compiler_params=pltpu.CompilerParams(
            vmem_limit_bytes=VMEM_LIMIT, dimension_semantics=("arbitrary", "arbitrary")),
        name="gla_conv_mixer",
    )(x, x, nw_mix, wa, wb, w_gr6, wgu6, bg, gnw, cw, wout, tri_wide, cmat3, lvl, st0, tail0,
      w_up[0], w_down[0])

    rows = batch * seq
    out = pl.pallas_call(
        _mlp_kernel,
        grid=(rows // MLP_TILE,),
        in_specs=[
            pl.BlockSpec((MLP_TILE, D_MODEL), lambda i: (i, 0)),
            _const_spec((1, D_MODEL)),
            _const_spec(wup.shape),
            _const_spec(wdown.shape),
            _const_spec((1, D_MODEL)),
        ],
        out_specs=pl.BlockSpec((MLP_TILE, D_MODEL), lambda i: (i, 0)),
        out_shape=jax.ShapeDtypeStruct((rows, D_MODEL), F32),
        compiler_params=pltpu.CompilerParams(
            vmem_limit_bytes=VMEM_LIMIT, dimension_semantics=("arbitrary",)),
        name="relu2_mlp_final_norm",
    )(h1.reshape(rows, D_MODEL), nw_mlp, wup, wdown, fw)
    return out.reshape(batch, seq, D_MODEL)
```

```python
import numpy as np
import jax
import jax.numpy as jnp
from jax import lax
from jax.experimental import pallas as pl
from jax.experimental.pallas import tpu as pltpu

D_MODEL = 1024
N_META = 16
HEADS = 4
DK = 64
DV = 128
QK = HEADS * DK
GW = HEADS * DV
RANK = 16
GATE_NORM = 16.0
CONV_CH = 512
D_FF = 4096
GLA_ROWS = 2 * QK + 2 * GW
PROJ_WIDTH = GLA_ROWS + RANK + 3 * CONV_CH
EPS = 1e-6

LANES = 128
SUBLANES = 8
VMEM_LIMIT = 56 * 1024 * 1024

MIX_TILE = 512
WIDE_CHUNK = 128
CHUNK = 64
HALF_WIDTHS = (32, 16, 8, 4, 2, 1)
N_LEVELS = len(HALF_WIDTHS)
WIDE_KEY_BOUND = 1e18
GATE_TERMS = 6
PREP_SLAB = 512
MLP_TILE = 1024
MLP_ROWS = 512
FF_CHUNK = 1024

BF16 = jnp.bfloat16
F32 = jnp.float32


def _rms(x, w):
    ms = jnp.mean(x * x, axis=-1, keepdims=True)
    return x * lax.rsqrt(ms + EPS) * w


def _log_sigmoid(z):
    return jnp.minimum(z, 0.0) - jnp.log(1.0 + jnp.exp(-jnp.abs(z)))


def _silu(g):
    half = 0.5 * g
    return half + half * jnp.tanh(half)


def _dot(a, b):
    return jnp.dot(a, b, preferred_element_type=F32)


def _dot_nt(a, b):
    return lax.dot_general(a, b, (((1,), (1,)), ((), ())), preferred_element_type=F32)


def _split3(x):
    hi = x.astype(BF16)
    r1 = x - hi.astype(F32)
    mid = r1.astype(BF16)
    lo = (r1 - mid.astype(F32)).astype(BF16)
    return hi, mid, lo


def _stack_heads(x):
    lane_head = lax.broadcasted_iota(jnp.int32, x.shape, 1) >> (DK.bit_length() - 1)
    return jnp.concatenate([jnp.where(lane_head == h, x, 0.0) for h in range(HEADS)], axis=0)


def _rows_by_head(v):
    return jnp.concatenate([v[:, h * DV:(h + 1) * DV] for h in range(HEADS)], axis=0)


def _gate_log_decay(gr6, wgu6, bg):
    hi, mid, lo = _split3(gr6)
    group = lax.broadcasted_iota(jnp.int32, gr6.shape, 1) >> (RANK.bit_length() - 1)
    lhs = jnp.where(group < 3, hi, jnp.where(group < 5, mid, lo))
    z = _dot(lhs, wgu6) + bg
    return _log_sigmoid(z) / GATE_NORM


def _cumsum_rows(tri3, gk):
    return _dot(tri3, jnp.concatenate(_split3(gk), axis=0))


def _state_increment(kdec, v):
    return _dot(_rows_by_head(v).T.astype(BF16), _stack_heads(kdec).astype(BF16))


def _state_update(st, kdec, v, decay_last):
    return st * decay_last + _state_increment(kdec, v)


def _prep_meta_kernel(meta_ref, nw_ref, wa_ref, wb_ref, wg_ref, wout_ref, wgu_ref, bg_ref, tri_ref,
                      wa16_ref, wb16_ref, wgr6_ref, wout16_ref, wgu6_ref, st_ref, tail_ref,
                      hn_s, proj_s):
    i = pl.program_id(0)
    n_a = GLA_ROWS // PREP_SLAB

    @pl.when(i == 0)
    def _():
        hn_s[...] = _rms(meta_ref[...], nw_ref[...]).astype(BF16)
        g_hi, g_mid, g_lo = _split3(wgu_ref[0])
        zeros = jnp.zeros((LANES - GATE_TERMS * RANK, QK), BF16)
        wgu6_ref[...] = jnp.concatenate([g_hi, g_mid, g_lo, g_hi, g_mid, g_hi, zeros], axis=0)
        gr = wg_ref[...].astype(BF16)
        wgr6_ref[...] = jnp.concatenate(
            [gr] * GATE_TERMS + [jnp.zeros((LANES - GATE_TERMS * RANK, D_MODEL), BF16)], axis=0)

    wa = wa_ref[...].astype(BF16)
    wb = wb_ref[...].astype(BF16)
    wa16_ref[...] = wa
    wb16_ref[...] = wb
    wout16_ref[...] = wout_ref[0].astype(BF16)
    proj_s[i] = _dot_nt(hn_s[...], wa)
    proj_s[n_a + i] = _dot_nt(hn_s[...], wb)

    @pl.when(i == pl.num_programs(0) - 1)
    def _():
        proj = jnp.concatenate([proj_s[j] for j in range(2 * n_a)], axis=1)
        k = proj[:, QK:2 * QK]
        v = proj[:, 2 * QK:2 * QK + GW]
        cc = proj[:, GLA_ROWS + CONV_CH:GLA_ROWS + 2 * CONV_CH]
        cx = proj[:, GLA_ROWS + 2 * CONV_CH:GLA_ROWS + 3 * CONV_CH]
        gk = _gate_log_decay(_dot_nt(hn_s[...], wgr6_ref[...]), wgu6_ref[...], bg_ref[...])
        b = _cumsum_rows(tri_ref[...], gk)
        b_last = b[N_META - 1:N_META, :]
        st_ref[...] = _state_update(jnp.zeros((DV, QK), F32), k * jnp.exp(b_last - b), v, jnp.exp(b_last))
        u = cc * cx
        tail_ref[...] = u[N_META - SUBLANES:N_META, :]


def _wide_chunks():
    return [slice(c * WIDE_CHUNK, (c + 1) * WIDE_CHUNK) for c in range(MIX_TILE // WIDE_CHUNK)]


def _wide_states(v_s, decay_last, kp_s, st_ref):
    chunks = _wide_chunks()
    kv = [_state_increment(kp_s[rows, :] * decay_last[c], v_s[rows, :]) for c, rows in enumerate(chunks)]
    states = [st_ref[...]]
    for c in range(len(chunks)):
        states.append(states[c] * decay_last[c] + kv[c])
    st_ref[...] = states[-1]
    return states[:-1]


def _wide_scores(qp_s, kp_s, states):
    return [_dot_nt(_stack_heads(qp_s[rows, :]).astype(BF16),
                    jnp.concatenate([kp_s[rows, :].astype(BF16), states[c].astype(BF16)], axis=0))
            for c, rows in enumerate(_wide_chunks())]


def _wide_outputs(sc, v_s, o_s):
    row = lax.broadcasted_iota(jnp.int32, (HEADS * WIDE_CHUNK, WIDE_CHUNK), 0) & (WIDE_CHUNK - 1)
    col = lax.broadcasted_iota(jnp.int32, (HEADS * WIDE_CHUNK, WIDE_CHUNK), 1)
    causal = row >= col
    for c, rows in enumerate(_wide_chunks()):
        scores = jnp.where(causal, sc[c][:, 0:WIDE_CHUNK], 0.0).astype(BF16)
        for h in range(HEADS):
            hr = slice(h * WIDE_CHUNK, (h + 1) * WIDE_CHUNK)
            o_h = _dot(scores[hr], v_s[rows, h * DV:(h + 1) * DV].astype(BF16))
            o_s[rows, h * DV:(h + 1) * DV] = o_h + sc[c][hr, WIDE_CHUNK:WIDE_CHUNK + DV]


def _gla_stable(q_s, k_s, v_s, gk_s, o_s, st_ref, cmat_ref, lvl_ref):
    lvl = lvl_ref[...]

    def chunk_body(c, st):
        r0 = pl.multiple_of(c * CHUNK, CHUNK)
        q = q_s[pl.ds(r0, CHUNK), :]
        k = k_s[pl.ds(r0, CHUNK), :]
        v = v_s[pl.ds(r0, CHUNK), :]
        gk = gk_s[pl.ds(r0, CHUNK), :]
        br = _cumsum_rows(cmat_ref[...], gk)
        b = br[0:CHUNK]
        row = lax.broadcasted_iota(jnp.int32, (CHUNK, QK), 0)
        a = jnp.zeros((HEADS * CHUNK, CHUNK), F32)
        for l, w in enumerate(HALF_WIDTHS):
            ref = br[(l + 1) * CHUNK:(l + 2) * CHUNK]
            e = jnp.exp(-jnp.abs(b - ref))
            second = (row & w) != 0
            ql = jnp.where(second, q * e, 0.0)
            kl = jnp.where(second, 0.0, k * e)
            g = _dot_nt(_stack_heads(ql).astype(BF16), kl.astype(BF16))
            a = jnp.where(lvl == l, g, a)
        g = _dot_nt(_stack_heads(q).astype(BF16), k.astype(BF16))
        a = jnp.where(lvl == N_LEVELS, g, a)
        a = a.astype(BF16)
        o_inter = _dot_nt(_stack_heads(q * jnp.exp(b)).astype(BF16), st.astype(BF16))
        for h in range(HEADS):
            o_h = _dot(a[h * CHUNK:(h + 1) * CHUNK], v[:, h * DV:(h + 1) * DV].astype(BF16))
            o_s[pl.ds(r0, CHUNK), h * DV:(h + 1) * DV] = o_h + o_inter[h * CHUNK:(h + 1) * CHUNK]
        b_last = b[CHUNK - 1:CHUNK, :]
        return _state_update(st, k * jnp.exp(b_last - b), v, jnp.exp(b_last))

    st_ref[...] = lax.fori_loop(0, MIX_TILE // CHUNK, chunk_body, st_ref[...])


def _mixer_kernel(x_ref, xnext_ref, nw_ref, wa_ref, wb_ref, wgr_ref, wgu_ref, bg_ref, gnw_ref, cw_ref, wout_ref,
                  tri_ref, cmat_ref, lvl_ref, st0_ref, tail0_ref, wup32_ref, wdown32_ref,
                  h1_ref, wup16_ref, wdown16_ref,
                  st_ref, stprev_s, tail_ref, hn_s, hnext_s, grnext_s, q_s, k_s, v_s, gk_s, qp_s, kp_s, o_s,
                  gate_s, yconv_s):
    t = pl.program_id(1)

    @pl.when(t == 0)
    def _():
        st_ref[...] = st0_ref[...]
        tail_ref[...] = tail0_ref[...]

    @pl.when(pl.program_id(0) + t == 0)
    def _():
        hnext_s[...] = _rms(x_ref[0], nw_ref[...]).astype(BF16)
        grnext_s[...] = _dot_nt(hnext_s[...], wgr_ref[...])

    wup16_ref[...] = wup32_ref[...].astype(BF16)
    wdown16_ref[...] = wdown32_ref[...].astype(BF16)

    hn_s[...] = hnext_s[...]
    gk = _gate_log_decay(grnext_s[...], wgu_ref[...], bg_ref[...])
    qk = _dot_nt(hn_s[...], wa_ref[0:2 * QK, :])
    q = qk[:, 0:QK] * (DK ** -0.5)
    k = qk[:, QK:2 * QK]
    n_wide = MIX_TILE // WIDE_CHUNK
    gk_cols = jnp.concatenate([gk[c * WIDE_CHUNK:(c + 1) * WIDE_CHUNK] for c in range(n_wide)], axis=1)
    b_cols = _cumsum_rows(tri_ref[...], gk_cols)
    b = jnp.concatenate([b_cols[:, c * QK:(c + 1) * QK] for c in range(n_wide)], axis=0)
    kp = k * jnp.exp(-b)
    decay_last = [jnp.exp(b[rows.stop - 1:rows.stop, :]) for rows in _wide_chunks()]
    qp_s[...] = q * jnp.exp(b)
    kp_s[...] = kp
    n_bad = jnp.sum(jnp.where(jnp.abs(kp) <= WIDE_KEY_BOUND, 0.0, 1.0))
    wide_ok = n_bad == 0.0
    stprev_s[...] = st_ref[...]
    hnext = _rms(xnext_ref[0], nw_ref[...]).astype(BF16)
    hnext_s[...] = hnext
    grnext_s[...] = _dot_nt(hnext, wgr_ref[...])
    v_s[...] = _dot_nt(hn_s[...], wa_ref[2 * QK:2 * QK + GW, :])
    cc = _dot_nt(hn_s[...], wb_ref[CONV_CH:2 * CONV_CH, :])
    states = _wide_states(v_s, decay_last, kp_s, st_ref)
    cx = _dot_nt(hn_s[...], wb_ref[2 * CONV_CH:3 * CONV_CH, :])
    cb = _dot_nt(hn_s[...], wb_ref[0:CONV_CH, :])
    sc = _wide_scores(qp_s, kp_s, states)
    gate_s[...] = _silu(_dot_nt(hn_s[...], wa_ref[2 * QK + GW:2 * QK + 2 * GW, :]))
    _wide_outputs(sc, v_s, o_s)

    u = cc * cx
    first_row = lax.broadcasted_iota(jnp.int32, u.shape, 0) == 0
    u1 = jnp.where(first_row, tail_ref[SUBLANES - 1:SUBLANES, :], pltpu.roll(u, 1, axis=0))
    u2 = jnp.where(first_row, tail_ref[SUBLANES - 2:SUBLANES - 1, :], pltpu.roll(u1, 1, axis=0))
    tail_ref[...] = u[MIX_TILE - SUBLANES:MIX_TILE, :]
    yconv_s[...] = (cb * (cw_ref[0:1, :] * u2 + cw_ref[1:2, :] * u1 + cw_ref[2:3, :] * u)).astype(BF16)

    def mix_out():
        gnw = gnw_ref[...]
        y_gla = jnp.concatenate(
            [_rms(o_s[:, h * DV:(h + 1) * DV], gnw) for h in range(HEADS)], axis=1) * gate_s[...]
        mixed = (_dot(yconv_s[...], wout_ref[GW:GW + CONV_CH, :])
                 + _dot(y_gla.astype(BF16), wout_ref[0:GW, :]))
        h1_ref[0] = x_ref[0] + mixed

    mix_out()

    @pl.when(jnp.logical_not(wide_ok))
    def _():
        st_ref[...] = stprev_s[...]
        qk_again = _dot_nt(hn_s[...], wa_ref[0:2 * QK, :])
        q_s[...] = qk_again[:, 0:QK] * (DK ** -0.5)
        k_s[...] = qk_again[:, QK:2 * QK]
        gk_s[...] = _gate_log_decay(_dot_nt(hn_s[...], wgr_ref[...]), wgu_ref[...], bg_ref[...])
        _gla_stable(q_s, k_s, v_s, gk_s, o_s, st_ref, cmat_ref, lvl_ref)
        mix_out()


def _mlp_kernel(h_ref, nw_ref, wup_ref, wdown_ref, fw_ref, out_ref):
    blocks = [slice(r, r + MLP_ROWS) for r in range(0, MLP_TILE, MLP_ROWS)]
    hn = [_rms(h_ref[rows, :], nw_ref[...]).astype(BF16) for rows in blocks]
    acc = [jnp.zeros((MLP_ROWS, D_MODEL), F32) for _ in blocks]
    for f in range(0, D_FF, FF_CHUNK):
        for i in range(len(blocks)):
            up = _dot(hn[i], wup_ref[:, f:f + FF_CHUNK])
            act = jnp.square(jnp.maximum(up, 0.0)).astype(BF16)
            acc[i] = acc[i] + _dot(act, wdown_ref[f:f + FF_CHUNK, :])
    for i, rows in enumerate(blocks):
        out_ref[rows, :] = _rms(h_ref[rows, :] + acc[i], fw_ref[...])


def _tri3(n):
    tri = np.tril(np.ones((n, n), np.float32))
    return np.concatenate([tri, tri, tri], axis=1)


def _cumsum_and_ref_matrix():
    tri = np.tril(np.ones((CHUNK, CHUNK), np.float32))
    blocks = [tri]
    idx = np.arange(CHUNK)
    for w in HALF_WIDTHS:
        ref_row = (idx // (2 * w)) * (2 * w) + w - 1
        blocks.append(tri[ref_row])
    return np.concatenate(blocks, axis=0)


def _level_matrix():
    i = np.arange(CHUNK)[:, None]
    j = np.arange(CHUNK)[None, :]
    lvl = np.full((CHUNK, CHUNK), -1, np.int32)
    for l, w in enumerate(HALF_WIDTHS):
        same = (i // (2 * w)) == (j // (2 * w))
        lvl[same & ((i & w) != 0) & ((j & w) == 0)] = l
    lvl[i == j] = N_LEVELS
    return np.tile(lvl, (HEADS, 1))


def _const_spec(shape):
    return pl.BlockSpec(shape, lambda *_: (0,) * len(shape))


def kernel(x, meta_tokens, norm_mix_w, w_in, w_gate_up, b_gate, gla_norm_w, conv_w, w_out,
           norm_mlp_w, w_up, w_down, norm_final_w):
    batch, seq, _ = x.shape
    assert seq % MIX_TILE == 0 and (batch * seq) % MLP_TILE == 0
    assert norm_mix_w.shape[0] == 1, "single layer"

    bg = b_gate[0].reshape(1, QK)
    nw_mix = norm_mix_w[0].reshape(1, D_MODEL)
    gnw = gla_norm_w[0].reshape(1, DV)
    cw = conv_w[0]
    nw_mlp = norm_mlp_w[0].reshape(1, D_MODEL)
    fw = norm_final_w.reshape(1, D_MODEL)

    cmat = _cumsum_and_ref_matrix()
    cmat3 = jnp.asarray(np.concatenate([cmat, cmat, cmat], axis=1), BF16)
    tri_meta = jnp.asarray(_tri3(N_META), BF16)
    tri_wide = jnp.asarray(_tri3(WIDE_CHUNK), BF16)
    lvl = jnp.asarray(_level_matrix())

    w_in_t = jnp.swapaxes(w_in, 1, 2)[0]
    conv_row0 = GLA_ROWS + RANK
    assert GLA_ROWS == 3 * CONV_CH and GLA_ROWS % PREP_SLAB == 0 and D_MODEL % PREP_SLAB == 0
    n_slabs = GLA_ROWS // PREP_SLAB
    last_out_slab = D_MODEL // PREP_SLAB - 1
    slab = (PREP_SLAB, D_MODEL)
    wa, wb, w_gr6, wout, wgu6, st0, tail0 = pl.pallas_call(
        _prep_meta_kernel,
        grid=(n_slabs,),
        in_specs=[
            _const_spec((N_META, D_MODEL)),
            _const_spec((1, D_MODEL)),
            pl.BlockSpec(slab, lambda i: (i, 0)),
            pl.BlockSpec((pl.Element(PREP_SLAB), pl.Element(D_MODEL)),
                         lambda i: (pl.multiple_of(conv_row0 + i * PREP_SLAB, RANK), 0)),
            pl.BlockSpec((pl.Element(RANK), pl.Element(D_MODEL)), lambda i: (GLA_ROWS, 0)),
            pl.BlockSpec((1,) + slab, lambda i: (0, jnp.minimum(i, last_out_slab), 0)),
            _const_spec((1, RANK, QK)),
            _const_spec((1, QK)),
            _const_spec(tri_meta.shape),
        ],
        out_specs=(
            pl.BlockSpec(slab, lambda i: (i, 0)),
            pl.BlockSpec(slab, lambda i: (i, 0)),
            _const_spec((LANES, D_MODEL)),
            pl.BlockSpec(slab, lambda i: (jnp.minimum(i, last_out_slab), 0)),
            _const_spec((LANES, QK)),
            _const_spec((DV, QK)),
            _const_spec((SUBLANES, CONV_CH)),
        ),
        out_shape=(
            jax.ShapeDtypeStruct((GLA_ROWS, D_MODEL), BF16),
            jax.ShapeDtypeStruct((3 * CONV_CH, D_MODEL), BF16),
            jax.ShapeDtypeStruct((LANES, D_MODEL), BF16),
            jax.ShapeDtypeStruct((D_MODEL, D_MODEL), BF16),
            jax.ShapeDtypeStruct((LANES, QK), BF16),
            jax.ShapeDtypeStruct((DV, QK), F32),
            jax.ShapeDtypeStruct((SUBLANES, CONV_CH), F32),
        ),
        scratch_shapes=[
            pltpu.VMEM((N_META, D_MODEL), BF16),
            pltpu.VMEM((2 * n_slabs, N_META, PREP_SLAB), F32),
        ],
        compiler_params=pltpu.CompilerParams(
            vmem_limit_bytes=VMEM_LIMIT, dimension_semantics=("arbitrary",)),
        name="weight_prep_meta_state",
    )(meta_tokens, nw_mix, w_in_t, w_in_t, w_in_t, w_out, w_gate_up, bg, tri_meta)

    n_tiles = seq // MIX_TILE

    def next_tile(b, t):
        nxt = jnp.minimum(b * n_tiles + t + 1, batch * n_tiles - 1)
        return nxt // n_tiles, nxt % n_tiles, 0

    n_steps = batch * n_tiles
    assert D_MODEL % n_steps == 0 and D_FF % n_steps == 0
    up_slab, down_slab = D_MODEL // n_steps, D_FF // n_steps

    h1, wup, wdown = pl.pallas_call(
        _mixer_kernel,
        grid=(batch, n_tiles),
        in_specs=[
            pl.BlockSpec((1, MIX_TILE, D_MODEL), lambda b, t: (b, t, 0)),
            pl.BlockSpec((1, MIX_TILE, D_MODEL), next_tile),
            _const_spec((1, D_MODEL)),
            _const_spec(wa.shape),
            _const_spec(wb.shape),
            _const_spec(w_gr6.shape),
            _const_spec(wgu6.shape),
            _const_spec((1, QK)),
            _const_spec((1, DV)),
            _const_spec(cw.shape),
            _const_spec(wout.shape),
            _const_spec(tri_wide.shape),
            _const_spec(cmat3.shape),
            _const_spec(lvl.shape),
            _const_spec((DV, QK)),
            _const_spec((SUBLANES, CONV_CH)),
            pl.BlockSpec((up_slab, D_FF), lambda b, t: (b * n_tiles + t, 0)),
            pl.BlockSpec((down_slab, D_MODEL), lambda b, t: (b * n_tiles + t, 0)),
        ],
        out_specs=(
            pl.BlockSpec((1, MIX_TILE, D_MODEL), lambda b, t: (b, t, 0)),
            pl.BlockSpec((up_slab, D_FF), lambda b, t: (b * n_tiles + t, 0)),
            pl.BlockSpec((down_slab, D_MODEL), lambda b, t: (b * n_tiles + t, 0)),
        ),
        out_shape=(
            jax.ShapeDtypeStruct((batch, seq, D_MODEL), F32),
            jax.ShapeDtypeStruct((D_MODEL, D_FF), BF16),
            jax.ShapeDtypeStruct((D_FF, D_MODEL), BF16),
        ),
        scratch_shapes=[
            pltpu.VMEM((DV, QK), F32),
            pltpu.VMEM((DV, QK), F32),
            pltpu.VMEM((SUBLANES, CONV_CH), F32),
            pltpu.VMEM((MIX_TILE, D_MODEL), BF16),
            pltpu.VMEM((MIX_TILE, D_MODEL), BF16),
            pltpu.VMEM((MIX_TILE, LANES), F32),
            pltpu.VMEM((MIX_TILE, QK), F32),
            pltpu.VMEM((MIX_TILE, QK), F32),
            pltpu.VMEM((MIX_TILE, GW), F32),
            pltpu.VMEM((MIX_TILE, QK), F32),
            pltpu.VMEM((MIX_TILE, QK), F32),
            pltpu.VMEM((MIX_TILE, QK), F32),
            pltpu.VMEM((MIX_TILE, GW), F32),
            pltpu.VMEM((MIX_TILE, GW), F32),
            pltpu.VMEM((MIX_TILE, CONV_CH), BF16),
        ],
        compiler_params=pltpu.CompilerParams(
            vmem_limit_bytes=VMEM_LIMIT, dimension_semantics=("arbitrary", "arbitrary")),
        name="gla_conv_mixer",
    )(x, x, nw_mix, wa, wb, w_gr6, wgu6, bg, gnw, cw, wout, tri_wide, cmat3, lvl, st0, tail0,
      w_up[0], w_down[0])

    rows = batch * seq
    out = pl.pallas_call(
        _mlp_kernel,
        grid=(rows // MLP_TILE,),
        in_specs=[
            pl.BlockSpec((MLP_TILE, D_MODEL), lambda i: (i, 0)),
            _const_spec((1, D_MODEL)),
            _const_spec(wup.shape),
            _const_spec(wdown.shape),
            _const_spec((1, D_MODEL)),
        ],
        out_specs=pl.BlockSpec((MLP_TILE, D_MODEL), lambda i: (i, 0)),
        out_shape=jax.ShapeDtypeStruct((rows, D_MODEL), F32),
        compiler_params=pltpu.CompilerParams(
            vmem_limit_bytes=VMEM_LIMIT, dimension_semantics=("arbitrary",)),
        name="relu2_mlp_final_norm",
    )(h1.reshape(rows, D_MODEL), nw_mlp, wup, wdown, fw)
    return out.reshape(batch, seq, D_MODEL)
```

```python
import numpy as np
import jax
import jax.numpy as jnp
from jax import lax
from jax.experimental import pallas as pl
from jax.experimental.pallas import tpu as pltpu

D_MODEL = 1024
N_META = 16
HEADS = 4
DK = 64
DV = 128
QK = HEADS * DK
GW = HEADS * DV
RANK = 16
GATE_NORM = 16.0
CONV_CH = 512
D_FF = 4096
GLA_ROWS = 2 * QK + 2 * GW
PROJ_WIDTH = GLA_ROWS + RANK + 3 * CONV_CH
EPS = 1e-6

LANES = 128
SUBLANES = 8
VMEM_LIMIT = 56 * 1024 * 1024

MIX_TILE = 512
WIDE_CHUNK = 128
CHUNK = 64
HALF_WIDTHS = (32, 16, 8, 4, 2, 1)
N_LEVELS = len(HALF_WIDTHS)
WIDE_KEY_BOUND = 1e18
GATE_TERMS = 6
PREP_SLAB = 512
MLP_TILE = 1024
MLP_ROWS = 512
FF_CHUNK = 1024

BF16 = jnp.bfloat16
F32 = jnp.float32


def _rms(x, w):
    ms = jnp.mean(x * x, axis=-1, keepdims=True)
    return x * lax.rsqrt(ms + EPS) * w


def _log_sigmoid(z):
    return jnp.minimum(z, 0.0) - jnp.log(1.0 + jnp.exp(-jnp.abs(z)))


def _silu(g):
    half = 0.5 * g
    return half + half * jnp.tanh(half)


def _dot(a, b):
    return jnp.dot(a, b, preferred_element_type=F32)


def _dot_nt(a, b):
    return lax.dot_general(a, b, (((1,), (1,)), ((), ())), preferred_element_type=F32)


def _split3(x):
    hi = x.astype(BF16)
    r1 = x - hi.astype(F32)
    mid = r1.astype(BF16)
    lo = (r1 - mid.astype(F32)).astype(BF16)
    return hi, mid, lo


def _stack_heads(x):
    lane_head = lax.broadcasted_iota(jnp.int32, x.shape, 1) >> (DK.bit_length() - 1)
    return jnp.concatenate([jnp.where(lane_head == h, x, 0.0) for h in range(HEADS)], axis=0)


def _rows_by_head(v):
    return jnp.concatenate([v[:, h * DV:(h + 1) * DV] for h in range(HEADS)], axis=0)


def _gate_log_decay(gr6, wgu6, bg):
    hi, mid, lo = _split3(gr6)
    group = lax.broadcasted_iota(jnp.int32, gr6.shape, 1) >> (RANK.bit_length() - 1)
    lhs = jnp.where(group < 3, hi, jnp.where(group < 5, mid, lo))
    z = _dot(lhs, wgu6) + bg
    return _log_sigmoid(z) / GATE_NORM


def _cumsum_rows(tri3, gk):
    return _dot(tri3, jnp.concatenate(_split3(gk), axis=0))


def _state_increment(kdec, v):
    return _dot(_rows_by_head(v).T.astype(BF16), _stack_heads(kdec).astype(BF16))


def _state_update(st, kdec, v, decay_last):
    return st * decay_last + _state_increment(kdec, v)


def _prep_meta_kernel(meta_ref, nw_ref, wa_ref, wb_ref, wg_ref, wout_ref, wgu_ref, bg_ref, tri_ref,
                      wa16_ref, wb16_ref, wgr6_ref, wout16_ref, wgu6_ref, st_ref, tail_ref,
                      hn_s, proj_s):
    i = pl.program_id(0)
    n_a = GLA_ROWS // PREP_SLAB

    @pl.when(i == 0)
    def _():
        hn_s[...] = _rms(meta_ref[...], nw_ref[...]).astype(BF16)
        g_hi, g_mid, g_lo = _split3(wgu_ref[0])
        zeros = jnp.zeros((LANES - GATE_TERMS * RANK, QK), BF16)
        wgu6_ref[...] = jnp.concatenate([g_hi, g_mid, g_lo, g_hi, g_mid, g_hi, zeros], axis=0)
        gr = wg_ref[...].astype(BF16)
        wgr6_ref[...] = jnp.concatenate(
            [gr] * GATE_TERMS + [jnp.zeros((LANES - GATE_TERMS * RANK, D_MODEL), BF16)], axis=0)

    wa = wa_ref[...].astype(BF16)
    wb = wb_ref[...].astype(BF16)
    wa16_ref[...] = wa
    wb16_ref[...] = wb
    wout16_ref[...] = wout_ref[0].astype(BF16)
    proj_s[i] = _dot_nt(hn_s[...], wa)
    proj_s[n_a + i] = _dot_nt(hn_s[...], wb)

    @pl.when(i == pl.num_programs(0) - 1)
    def _():
        proj = jnp.concatenate([proj_s[j] for j in range(2 * n_a)], axis=1)
        k = proj[:, QK:2 * QK]
        v = proj[:, 2 * QK:2 * QK + GW]
        cc = proj[:, GLA_ROWS + CONV_CH:GLA_ROWS + 2 * CONV_CH]
        cx = proj[:, GLA_ROWS + 2 * CONV_CH:GLA_ROWS + 3 * CONV_CH]
        gk = _gate_log_decay(_dot_nt(hn_s[...], wgr6_ref[...]), wgu6_ref[...], bg_ref[...])
        b = _cumsum_rows(tri_ref[...], gk)
        b_last = b[N_META - 1:N_META, :]
        st_ref[...] = _state_update(jnp.zeros((DV, QK), F32), k * jnp.exp(b_last - b), v, jnp.exp(b_last))
        u = cc * cx
        tail_ref[...] = u[N_META - SUBLANES:N_META, :]


def _wide_chunks():
    return [slice(c * WIDE_CHUNK, (c + 1) * WIDE_CHUNK) for c in range(MIX_TILE // WIDE_CHUNK)]


def _wide_states(v_s, decay_last, kp_s, st_ref):
    chunks = _wide_chunks()
    kv = [_state_increment(kp_s[rows, :] * decay_last[c], v_s[rows, :]) for c, rows in enumerate(chunks)]
    states = [st_ref[...]]
    for c in range(len(chunks)):
        states.append(states[c] * decay_last[c] + kv[c])
    st_ref[...] = states[-1]
    return states[:-1]


def _wide_scores(qp_s, kp_s, states):
    return [_dot_nt(_stack_heads(qp_s[rows, :]).astype(BF16),
                    jnp.concatenate([kp_s[rows, :].astype(BF16), states[c].astype(BF16)], axis=0))
            for c, rows in enumerate(_wide_chunks())]


def _wide_outputs(sc, v_s, o_s):
    row = lax.broadcasted_iota(jnp.int32, (HEADS * WIDE_CHUNK, WIDE_CHUNK), 0) & (WIDE_CHUNK - 1)
    col = lax.broadcasted_iota(jnp.int32, (HEADS * WIDE_CHUNK, WIDE_CHUNK), 1)
    causal = row >= col
    for c, rows in enumerate(_wide_chunks()):
        scores = jnp.where(causal, sc[c][:, 0:WIDE_CHUNK], 0.0).astype(BF16)
        for h in range(HEADS):
            hr = slice(h * WIDE_CHUNK, (h + 1) * WIDE_CHUNK)
            o_h = _dot(scores[hr], v_s[rows, h * DV:(h + 1) * DV].astype(BF16))
            o_s[rows, h * DV:(h + 1) * DV] = o_h + sc[c][hr, WIDE_CHUNK:WIDE_CHUNK + DV]


def _gla_stable(q_s, k_s, v_s, gk_s, o_s, st_ref, cmat_ref, lvl_ref):
    lvl = lvl_ref[...]

    def chunk_body(c, st):
        r0 = pl.multiple_of(c * CHUNK, CHUNK)
        q = q_s[pl.ds(r0, CHUNK), :]
        k = k_s[pl.ds(r0, CHUNK), :]
        v = v_s[pl.ds(r0, CHUNK), :]
        gk = gk_s[pl.ds(r0, CHUNK), :]
        br = _cumsum_rows(cmat_ref[...], gk)
        b = br[0:CHUNK]
        row = lax.broadcasted_iota(jnp.int32, (CHUNK, QK), 0)
        a = jnp.zeros((HEADS * CHUNK, CHUNK), F32)
        for l, w in enumerate(HALF_WIDTHS):
            ref = br[(l + 1) * CHUNK:(l + 2) * CHUNK]
            e = jnp.exp(-jnp.abs(b - ref))
            second = (row & w) != 0
            ql = jnp.where(second, q * e, 0.0)
            kl = jnp.where(second, 0.0, k * e)
            g = _dot_nt(_stack_heads(ql).astype(BF16), kl.astype(BF16))
            a = jnp.where(lvl == l, g, a)
        g = _dot_nt(_stack_heads(q).astype(BF16), k.astype(BF16))
        a = jnp.where(lvl == N_LEVELS, g, a)
        a = a.astype(BF16)
        o_inter = _dot_nt(_stack_heads(q * jnp.exp(b)).astype(BF16), st.astype(BF16))
        for h in range(HEADS):
            o_h = _dot(a[h * CHUNK:(h + 1) * CHUNK], v[:, h * DV:(h + 1) * DV].astype(BF16))
            o_s[pl.ds(r0, CHUNK), h * DV:(h + 1) * DV] = o_h + o_inter[h * CHUNK:(h + 1) * CHUNK]
        b_last = b[CHUNK - 1:CHUNK, :]
        return _state_update(st, k * jnp.exp(b_last - b), v, jnp.exp(b_last))

    st_ref[...] = lax.fori_loop(0, MIX_TILE // CHUNK, chunk_body, st_ref[...])


def _mixer_kernel(x_ref, xnext_ref, nw_ref, wa_ref, wb_ref, wgr_ref, wgu_ref, bg_ref, gnw_ref, cw_ref, wout_ref,
                  tri_ref, cmat_ref, lvl_ref, st0_ref, tail0_ref, wup32_ref, wdown32_ref,
                  h1_ref, wup16_ref, wdown16_ref,
                  st_ref, stprev_s, tail_ref, hn_s, hnext_s, grnext_s, q_s, k_s, v_s, gk_s, qp_s, kp_s, o_s,
                  gate_s, yconv_s):
    t = pl.program_id(1)

    @pl.when(t == 0)
    def _():
        st_ref[...] = st0_ref[...]
        tail_ref[...] = tail0_ref[...]

    @pl.when(pl.program_id(0) + t == 0)
    def _():
        hnext_s[...] = _rms(x_ref[0], nw_ref[...]).astype(BF16)
        grnext_s[...] = _dot_nt(hnext_s[...], wgr_ref[...])

    wup16_ref[...] = wup32_ref[...].astype(BF16)
    wdown16_ref[...] = wdown32_ref[...].astype(BF16)

    hn_s[...] = hnext_s[...]
    gk = _gate_log_decay(grnext_s[...], wgu_ref[...], bg_ref[...])
    qk = _dot_nt(hn_s[...], wa_ref[0:2 * QK, :])
    q = qk[:, 0:QK] * (DK ** -0.5)
    k = qk[:, QK:2 * QK]
    n_wide = MIX_TILE // WIDE_CHUNK
    gk_cols = jnp.concatenate([gk[c * WIDE_CHUNK:(c + 1) * WIDE_CHUNK] for c in range(n_wide)], axis=1)
    b_cols = _cumsum_rows(tri_ref[...], gk_cols)
    b = jnp.concatenate([b_cols[:, c * QK:(c + 1) * QK] for c in range(n_wide)], axis=0)
    kp = k * jnp.exp(-b)
    decay_last = [jnp.exp(b[rows.stop - 1:rows.stop, :]) for rows in _wide_chunks()]
    qp_s[...] = q * jnp.exp(b)
    kp_s[...] = kp
    n_bad = jnp.sum(jnp.where(jnp.abs(kp) <= WIDE_KEY_BOUND, 0.0, 1.0))
    wide_ok = n_bad == 0.0
    stprev_s[...] = st_ref[...]
    v_s[...] = _dot_nt(hn_s[...], wa_ref[2 * QK:2 * QK + GW, :])
    cc = _dot_nt(hn_s[...], wb_ref[CONV_CH:2 * CONV_CH, :])
    states = _wide_states(v_s, decay_last, kp_s, st_ref)
    cx = _dot_nt(hn_s[...], wb_ref[2 * CONV_CH:3 * CONV_CH, :])
    sc = _wide_scores(qp_s, kp_s, states)
    hnext = _rms(xnext_ref[0], nw_ref[...]).astype(BF16)
    hnext_s[...] = hnext
    grnext_s[...] = _dot_nt(hnext, wgr_ref[...])
    cb = _dot_nt(hn_s[...], wb_ref[0:CONV_CH, :])
    _wide_outputs(sc, v_s, o_s)
    gate_s[...] = _silu(_dot_nt(hn_s[...], wa_ref[2 * QK + GW:2 * QK + 2 * GW, :]))

    u = cc * cx
    first_row = lax.broadcasted_iota(jnp.int32, u.shape, 0) == 0
    u1 = jnp.where(first_row, tail_ref[SUBLANES - 1:SUBLANES, :], pltpu.roll(u, 1, axis=0))
    u2 = jnp.where(first_row, tail_ref[SUBLANES - 2:SUBLANES - 1, :], pltpu.roll(u1, 1, axis=0))
    tail_ref[...] = u[MIX_TILE - SUBLANES:MIX_TILE, :]
    yconv_s[...] = (cb * (cw_ref[0:1, :] * u2 + cw_ref[1:2, :] * u1 + cw_ref[2:3, :] * u)).astype(BF16)

    def mix_out():
        gnw = gnw_ref[...]
        y_gla = jnp.concatenate(
            [_rms(o_s[:, h * DV:(h + 1) * DV], gnw) for h in range(HEADS)], axis=1) * gate_s[...]
        mixed = (_dot(yconv_s[...], wout_ref[GW:GW + CONV_CH, :])
                 + _dot(y_gla.astype(BF16), wout_ref[0:GW, :]))
        h1_ref[0] = x_ref[0] + mixed

    mix_out()

    @pl.when(jnp.logical_not(wide_ok))
    def _():
        st_ref[...] = stprev_s[...]
        qk_again = _dot_nt(hn_s[...], wa_ref[0:2 * QK, :])
        q_s[...] = qk_again[:, 0:QK] * (DK ** -0.5)
        k_s[...] = qk_again[:, QK:2 * QK]
        gk_s[...] = _gate_log_decay(_dot_nt(hn_s[...], wgr_ref[...]), wgu_ref[...], bg_ref[...])
        _gla_stable(q_s, k_s, v_s, gk_s, o_s, st_ref, cmat_ref, lvl_ref)
        mix_out()


def _mlp_weight_copies(wup_hbm, wdown_hbm, wup_s, wdown_s, sems):
    copies = []
    for c, f in enumerate(range(0, D_FF, FF_CHUNK)):
        cols = pl.ds(f, FF_CHUNK)
        copies.append(pltpu.make_async_copy(wup_hbm.at[:, cols], wup_s.at[:, cols], sems.at[2 * c]))
        copies.append(pltpu.make_async_copy(wdown_hbm.at[cols, :], wdown_s.at[cols, :], sems.at[2 * c + 1]))
    return copies


def _mlp_kernel(h_ref, nw_ref, wup_hbm, wdown_hbm, fw_ref, out_ref, wup_s, wdown_s, sems):
    def body(stage_weights):
        copies = _mlp_weight_copies(wup_hbm, wdown_hbm, wup_s, wdown_s, sems) if stage_weights else []
        for copy in copies:
            copy.start()
        blocks = [slice(r, r + MLP_ROWS) for r in range(0, MLP_TILE, MLP_ROWS)]
        hn = [_rms(h_ref[rows, :], nw_ref[...]).astype(BF16) for rows in blocks]
        acc = [jnp.zeros((MLP_ROWS, D_MODEL), F32) for _ in blocks]
        for c, f in enumerate(range(0, D_FF, FF_CHUNK)):
            for copy in copies[2 * c:2 * c + 2]:
                copy.wait()
            for i in range(len(blocks)):
                up = _dot(hn[i], wup_s[:, f:f + FF_CHUNK])
                act = jnp.square(jnp.maximum(up, 0.0)).astype(BF16)
                acc[i] = acc[i] + _dot(act, wdown_s[f:f + FF_CHUNK, :])
        for i, rows in enumerate(blocks):
            out_ref[rows, :] = _rms(h_ref[rows, :] + acc[i], fw_ref[...])

    first = pl.program_id(0) == 0

    @pl.when(first)
    def _():
        body(True)

    @pl.when(jnp.logical_not(first))
    def _():
        body(False)


def _tri3(n):
    tri = np.tril(np.ones((n, n), np.float32))
    return np.concatenate([tri, tri, tri], axis=1)


def _cumsum_and_ref_matrix():
    tri = np.tril(np.ones((CHUNK, CHUNK), np.float32))
    blocks = [tri]
    idx = np.arange(CHUNK)
    for w in HALF_WIDTHS:
        ref_row = (idx // (2 * w)) * (2 * w) + w - 1
        blocks.append(tri[ref_row])
    return np.concatenate(blocks, axis=0)


def _level_matrix():
    i = np.arange(CHUNK)[:, None]
    j = np.arange(CHUNK)[None, :]
    lvl = np.full((CHUNK, CHUNK), -1, np.int32)
    for l, w in enumerate(HALF_WIDTHS):
        same = (i // (2 * w)) == (j // (2 * w))
        lvl[same & ((i & w) != 0) & ((j & w) == 0)] = l
    lvl[i == j] = N_LEVELS
    return np.tile(lvl, (HEADS, 1))


def _const_spec(shape):
    return pl.BlockSpec(shape, lambda *_: (0,) * len(shape))


def kernel(x, meta_tokens, norm_mix_w, w_in, w_gate_up, b_gate, gla_norm_w, conv_w, w_out,
           norm_mlp_w, w_up, w_down, norm_final_w):
    batch, seq, _ = x.shape
    assert seq % MIX_TILE == 0 and (batch * seq) % MLP_TILE == 0
    assert norm_mix_w.shape[0] == 1, "single layer"

    bg = b_gate[0].reshape(1, QK)
    nw_mix = norm_mix_w[0].reshape(1, D_MODEL)
    gnw = gla_norm_w[0].reshape(1, DV)
    cw = conv_w[0]
    nw_mlp = norm_mlp_w[0].reshape(1, D_MODEL)
    fw = norm_final_w.reshape(1, D_MODEL)

    cmat = _cumsum_and_ref_matrix()
    cmat3 = jnp.asarray(np.concatenate([cmat, cmat, cmat], axis=1), BF16)
    tri_meta = jnp.asarray(_tri3(N_META), BF16)
    tri_wide = jnp.asarray(_tri3(WIDE_CHUNK), BF16)
    lvl = jnp.asarray(_level_matrix())

    w_in_t = jnp.swapaxes(w_in, 1, 2)[0]
    conv_row0 = GLA_ROWS + RANK
    assert GLA_ROWS == 3 * CONV_CH and GLA_ROWS % PREP_SLAB == 0 and D_MODEL % PREP_SLAB == 0
    n_slabs = GLA_ROWS // PREP_SLAB
    last_out_slab = D_MODEL // PREP_SLAB - 1
    slab = (PREP_SLAB, D_MODEL)
    wa, wb, w_gr6, wout, wgu6, st0, tail0 = pl.pallas_call(
        _prep_meta_kernel,
        grid=(n_slabs,),
        in_specs=[
            _const_spec((N_META, D_MODEL)),
            _const_spec((1, D_MODEL)),
            pl.BlockSpec(slab, lambda i: (i, 0)),
            pl.BlockSpec((pl.Element(PREP_SLAB), pl.Element(D_MODEL)),
                         lambda i: (pl.multiple_of(conv_row0 + i * PREP_SLAB, RANK), 0)),
            pl.BlockSpec((pl.Element(RANK), pl.Element(D_MODEL)), lambda i: (GLA_ROWS, 0)),
            pl.BlockSpec((1,) + slab, lambda i: (0, jnp.minimum(i, last_out_slab), 0)),
            _const_spec((1, RANK, QK)),
            _const_spec((1, QK)),
            _const_spec(tri_meta.shape),
        ],
        out_specs=(
            pl.BlockSpec(slab, lambda i: (i, 0)),
            pl.BlockSpec(slab, lambda i: (i, 0)),
            _const_spec((LANES, D_MODEL)),
            pl.BlockSpec(slab, lambda i: (jnp.minimum(i, last_out_slab), 0)),
            _const_spec((LANES, QK)),
            _const_spec((DV, QK)),
            _const_spec((SUBLANES, CONV_CH)),
        ),
        out_shape=(
            jax.ShapeDtypeStruct((GLA_ROWS, D_MODEL), BF16),
            jax.ShapeDtypeStruct((3 * CONV_CH, D_MODEL), BF16),
            jax.ShapeDtypeStruct((LANES, D_MODEL), BF16),
            jax.ShapeDtypeStruct((D_MODEL, D_MODEL), BF16),
            jax.ShapeDtypeStruct((LANES, QK), BF16),
            jax.ShapeDtypeStruct((DV, QK), F32),
            jax.ShapeDtypeStruct((SUBLANES, CONV_CH), F32),
        ),
        scratch_shapes=[
            pltpu.VMEM((N_META, D_MODEL), BF16),
            pltpu.VMEM((2 * n_slabs, N_META, PREP_SLAB), F32),
        ],
        compiler_params=pltpu.CompilerParams(
            vmem_limit_bytes=VMEM_LIMIT, dimension_semantics=("arbitrary",)),
        name="weight_prep_meta_state",
    )(meta_tokens, nw_mix, w_in_t, w_in_t, w_in_t, w_out, w_gate_up, bg, tri_meta)

    n_tiles = seq // MIX_TILE

    def next_tile(b, t):
        nxt = jnp.minimum(b * n_tiles + t + 1, batch * n_tiles - 1)
        return nxt // n_tiles, nxt % n_tiles, 0

    n_steps = batch * n_tiles
    assert D_MODEL % n_steps == 0 and D_FF % n_steps == 0
    up_slab, down_slab = D_MODEL // n_steps, D_FF // n_steps

    h1, wup, wdown = pl.pallas_call(
        _mixer_kernel,
        grid=(batch, n_tiles),
        in_specs=[
            pl.BlockSpec((1, MIX_TILE, D_MODEL), lambda b, t: (b, t, 0)),
            pl.BlockSpec((1, MIX_TILE, D_MODEL), next_tile),
            _const_spec((1, D_MODEL)),
            _const_spec(wa.shape),
            _const_spec(wb.shape),
            _const_spec(w_gr6.shape),
            _const_spec(wgu6.shape),
            _const_spec((1, QK)),
            _const_spec((1, DV)),
            _const_spec(cw.shape),
            _const_spec(wout.shape),
            _const_spec(tri_wide.shape),
            _const_spec(cmat3.shape),
            _const_spec(lvl.shape),
            _const_spec((DV, QK)),
            _const_spec((SUBLANES, CONV_CH)),
            pl.BlockSpec((up_slab, D_FF), lambda b, t: (b * n_tiles + t, 0)),
            pl.BlockSpec((down_slab, D_MODEL), lambda b, t: (b * n_tiles + t, 0)),
        ],
        out_specs=(
            pl.BlockSpec((1, MIX_TILE, D_MODEL), lambda b, t: (b, t, 0)),
            pl.BlockSpec((up_slab, D_FF), lambda b, t: (b * n_tiles + t, 0)),
            pl.BlockSpec((down_slab, D_MODEL), lambda b, t: (b * n_tiles + t, 0)),
        ),
        out_shape=(
            jax.ShapeDtypeStruct((batch, seq, D_MODEL), F32),
            jax.ShapeDtypeStruct((D_MODEL, D_FF), BF16),
            jax.ShapeDtypeStruct((D_FF, D_MODEL), BF16),
        ),
        scratch_shapes=[
            pltpu.VMEM((DV, QK), F32),
            pltpu.VMEM((DV, QK), F32),
            pltpu.VMEM((SUBLANES, CONV_CH), F32),
            pltpu.VMEM((MIX_TILE, D_MODEL), BF16),
            pltpu.VMEM((MIX_TILE, D_MODEL), BF16),
            pltpu.VMEM((MIX_TILE, LANES), F32),
            pltpu.VMEM((MIX_TILE, QK), F32),
            pltpu.VMEM((MIX_TILE, QK), F32),
            pltpu.VMEM((MIX_TILE, GW), F32),
            pltpu.VMEM((MIX_TILE, QK), F32),
            pltpu.VMEM((MIX_TILE, QK), F32),
            pltpu.VMEM((MIX_TILE, QK), F32),
            pltpu.VMEM((MIX_TILE, GW), F32),
            pltpu.VMEM((MIX_TILE, GW), F32),
            pltpu.VMEM((MIX_TILE, CONV_CH), BF16),
        ],
        compiler_params=pltpu.CompilerParams(
            vmem_limit_bytes=VMEM_LIMIT, dimension_semantics=("arbitrary", "arbitrary")),
        name="gla_conv_mixer",
    )(x, x, nw_mix, wa, wb, w_gr6, wgu6, bg, gnw, cw, wout, tri_wide, cmat3, lvl, st0, tail0,
      w_up[0], w_down[0])

    rows = batch * seq
    out = pl.pallas_call(
        _mlp_kernel,
        grid=(rows // MLP_TILE,),
        in_specs=[
            pl.BlockSpec((MLP_TILE, D_MODEL), lambda i: (i, 0)),
            _const_spec((1, D_MODEL)),
            pl.BlockSpec(memory_space=pl.ANY),
            pl.BlockSpec(memory_space=pl.ANY),
            _const_spec((1, D_MODEL)),
        ],
        out_specs=pl.BlockSpec((MLP_TILE, D_MODEL), lambda i: (i, 0)),
        out_shape=jax.ShapeDtypeStruct((rows, D_MODEL), F32),
        scratch_shapes=[
            pltpu.VMEM((D_MODEL, D_FF), BF16),
            pltpu.VMEM((D_FF, D_MODEL), BF16),
            pltpu.SemaphoreType.DMA((2 * (D_FF // FF_CHUNK),)),
        ],
        compiler_params=pltpu.CompilerParams(
            vmem_limit_bytes=VMEM_LIMIT, dimension_semantics=("arbitrary",)),
        name="relu2_mlp_final_norm",
    )(h1.reshape(rows, D_MODEL), nw_mlp, wup, wdown, fw)
    return out.reshape(batch, seq, D_MODEL)
```

```python
import numpy as np
import jax
import jax.numpy as jnp
from jax import lax
from jax.experimental import pallas as pl
from jax.experimental.pallas import tpu as pltpu

D_MODEL = 1024
N_META = 16
HEADS = 4
DK = 64
DV = 128
QK = HEADS * DK
GW = HEADS * DV
RANK = 16
GATE_NORM = 16.0
CONV_CH = 512
D_FF = 4096
GLA_ROWS = 2 * QK + 2 * GW
PROJ_WIDTH = GLA_ROWS + RANK + 3 * CONV_CH
EPS = 1e-6

LANES = 128
SUBLANES = 8
VMEM_LIMIT = 56 * 1024 * 1024

MIX_TILE = 512
WIDE_CHUNK = 128
CHUNK = 64
HALF_WIDTHS = (32, 16, 8, 4, 2, 1)
N_LEVELS = len(HALF_WIDTHS)
WIDE_KEY_BOUND = 1e18
GATE_TERMS = 6
PREP_SLAB = 512
MLP_TILE = 1024
MLP_ROWS = 512
FF_CHUNK = 1024

BF16 = jnp.bfloat16
F32 = jnp.float32


def _rms(x, w):
    ms = jnp.mean(x * x, axis=-1, keepdims=True)
    return x * lax.rsqrt(ms + EPS) * w


def _log_sigmoid(z):
    return jnp.minimum(z, 0.0) - jnp.log(1.0 + jnp.exp(-jnp.abs(z)))


def _silu(g):
    half = 0.5 * g
    return half + half * jnp.tanh(half)


def _dot(a, b):
    return jnp.dot(a, b, preferred_element_type=F32)


def _dot_nt(a, b):
    return lax.dot_general(a, b, (((1,), (1,)), ((), ())), preferred_element_type=F32)


def _split3(x):
    hi = x.astype(BF16)
    r1 = x - hi.astype(F32)
    mid = r1.astype(BF16)
    lo = (r1 - mid.astype(F32)).astype(BF16)
    return hi, mid, lo


def _stack_heads(x):
    lane_head = lax.broadcasted_iota(jnp.int32, x.shape, 1) >> (DK.bit_length() - 1)
    return jnp.concatenate([jnp.where(lane_head == h, x, 0.0) for h in range(HEADS)], axis=0)


def _rows_by_head(v):
    return jnp.concatenate([v[:, h * DV:(h + 1) * DV] for h in range(HEADS)], axis=0)


def _gate_log_decay(gr6, wgu6, bg):
    hi, mid, lo = _split3(gr6)
    group = lax.broadcasted_iota(jnp.int32, gr6.shape, 1) >> (RANK.bit_length() - 1)
    lhs = jnp.where(group < 3, hi, jnp.where(group < 5, mid, lo))
    z = _dot(lhs, wgu6) + bg
    return _log_sigmoid(z) / GATE_NORM


def _cumsum_rows(tri3, gk):
    return _dot(tri3, jnp.concatenate(_split3(gk), axis=0))


def _state_increment(kdec, v):
    return _dot(_rows_by_head(v).T.astype(BF16), _stack_heads(kdec).astype(BF16))


def _state_update(st, kdec, v, decay_last):
    return st * decay_last + _state_increment(kdec, v)


def _prep_meta_kernel(meta_ref, nw_ref, wa_ref, wb_ref, wg_ref, wout_ref, wgu_ref, bg_ref, tri_ref,
                      wa16_ref, wb16_ref, wgr6_ref, wout16_ref, wgu6_ref, st_ref, tail_ref,
                      hn_s, proj_s):
    i = pl.program_id(0)
    n_a = GLA_ROWS // PREP_SLAB

    @pl.when(i == 0)
    def _():
        hn_s[...] = _rms(meta_ref[...], nw_ref[...]).astype(BF16)
        g_hi, g_mid, g_lo = _split3(wgu_ref[0])
        zeros = jnp.zeros((LANES - GATE_TERMS * RANK, QK), BF16)
        wgu6_ref[...] = jnp.concatenate([g_hi, g_mid, g_lo, g_hi, g_mid, g_hi, zeros], axis=0)
        gr = wg_ref[...].astype(BF16)
        wgr6_ref[...] = jnp.concatenate(
            [gr] * GATE_TERMS + [jnp.zeros((LANES - GATE_TERMS * RANK, D_MODEL), BF16)], axis=0)

    wa = wa_ref[...].astype(BF16)
    wb = wb_ref[...].astype(BF16)
    wa16_ref[...] = wa
    wb16_ref[...] = wb
    wout16_ref[...] = wout_ref[0].astype(BF16)
    proj_s[i] = _dot_nt(hn_s[...], wa)
    proj_s[n_a + i] = _dot_nt(hn_s[...], wb)

    @pl.when(i == pl.num_programs(0) - 1)
    def _():
        proj = jnp.concatenate([proj_s[j] for j in range(2 * n_a)], axis=1)
        k = proj[:, QK:2 * QK]
        v = proj[:, 2 * QK:2 * QK + GW]
        cc = proj[:, GLA_ROWS + CONV_CH:GLA_ROWS + 2 * CONV_CH]
        cx = proj[:, GLA_ROWS + 2 * CONV_CH:GLA_ROWS + 3 * CONV_CH]
        gk = _gate_log_decay(_dot_nt(hn_s[...], wgr6_ref[...]), wgu6_ref[...], bg_ref[...])
        b = _cumsum_rows(tri_ref[...], gk)
        b_last = b[N_META - 1:N_META, :]
        st_ref[...] = _state_update(jnp.zeros((DV, QK), F32), k * jnp.exp(b_last - b), v, jnp.exp(b_last))
        u = cc * cx
        tail_ref[...] = u[N_META - SUBLANES:N_META, :]


def _wide_chunks():
    return [slice(c * WIDE_CHUNK, (c + 1) * WIDE_CHUNK) for c in range(MIX_TILE // WIDE_CHUNK)]


def _wide_states(v_s, decay_last, kp_s, st_ref):
    chunks = _wide_chunks()
    kv = [_state_increment(kp_s[rows, :] * decay_last[c], v_s[rows, :]) for c, rows in enumerate(chunks)]
    states = [st_ref[...]]
    for c in range(len(chunks)):
        states.append(states[c] * decay_last[c] + kv[c])
    st_ref[...] = states[-1]
    return states[:-1]


def _wide_scores(qp_s, kp_s, states):
    return [_dot_nt(_stack_heads(qp_s[rows, :]).astype(BF16),
                    jnp.concatenate([kp_s[rows, :].astype(BF16), states[c].astype(BF16)], axis=0))
            for c, rows in enumerate(_wide_chunks())]


def _wide_outputs(sc, v_s, o_s):
    row = lax.broadcasted_iota(jnp.int32, (HEADS * WIDE_CHUNK, WIDE_CHUNK), 0) & (WIDE_CHUNK - 1)
    col = lax.broadcasted_iota(jnp.int32, (HEADS * WIDE_CHUNK, WIDE_CHUNK), 1)
    causal = row >= col
    for c, rows in enumerate(_wide_chunks()):
        scores = jnp.where(causal, sc[c][:, 0:WIDE_CHUNK], 0.0).astype(BF16)
        for h in range(HEADS):
            hr = slice(h * WIDE_CHUNK, (h + 1) * WIDE_CHUNK)
            o_h = _dot(scores[hr], v_s[rows, h * DV:(h + 1) * DV].astype(BF16))
            o_s[rows, h * DV:(h + 1) * DV] = o_h + sc[c][hr, WIDE_CHUNK:WIDE_CHUNK + DV]


def _gla_stable(q_s, k_s, v_s, gk_s, o_s, st_ref, cmat_ref, lvl_ref):
    lvl = lvl_ref[...]

    def chunk_body(c, st):
        r0 = pl.multiple_of(c * CHUNK, CHUNK)
        q = q_s[pl.ds(r0, CHUNK), :]
        k = k_s[pl.ds(r0, CHUNK), :]
        v = v_s[pl.ds(r0, CHUNK), :]
        gk = gk_s[pl.ds(r0, CHUNK), :]
        br = _cumsum_rows(cmat_ref[...], gk)
        b = br[0:CHUNK]
        row = lax.broadcasted_iota(jnp.int32, (CHUNK, QK), 0)
        a = jnp.zeros((HEADS * CHUNK, CHUNK), F32)
        for l, w in enumerate(HALF_WIDTHS):
            ref = br[(l + 1) * CHUNK:(l + 2) * CHUNK]
            e = jnp.exp(-jnp.abs(b - ref))
            second = (row & w) != 0
            ql = jnp.where(second, q * e, 0.0)
            kl = jnp.where(second, 0.0, k * e)
            g = _dot_nt(_stack_heads(ql).astype(BF16), kl.astype(BF16))
            a = jnp.where(lvl == l, g, a)
        g = _dot_nt(_stack_heads(q).astype(BF16), k.astype(BF16))
        a = jnp.where(lvl == N_LEVELS, g, a)
        a = a.astype(BF16)
        o_inter = _dot_nt(_stack_heads(q * jnp.exp(b)).astype(BF16), st.astype(BF16))
        for h in range(HEADS):
            o_h = _dot(a[h * CHUNK:(h + 1) * CHUNK], v[:, h * DV:(h + 1) * DV].astype(BF16))
            o_s[pl.ds(r0, CHUNK), h * DV:(h + 1) * DV] = o_h + o_inter[h * CHUNK:(h + 1) * CHUNK]
        b_last = b[CHUNK - 1:CHUNK, :]
        return _state_update(st, k * jnp.exp(b_last - b), v, jnp.exp(b_last))

    st_ref[...] = lax.fori_loop(0, MIX_TILE // CHUNK, chunk_body, st_ref[...])


def _mixer_kernel(x_ref, xnext_ref, nw_ref, wa_ref, wb_ref, wgr_ref, wgu_ref, bg_ref, gnw_ref, cw_ref, wout_ref,
                  tri_ref, cmat_ref, lvl_ref, st0_ref, tail0_ref, wup32_ref, wdown32_ref,
                  h1_ref, wup16_ref, wdown16_ref,
                  st_ref, stprev_s, tail_ref, hn_s, hnext_s, grnext_s, q_s, k_s, v_s, gk_s, qp_s, kp_s, o_s,
                  gate_s, yconv_s):
    t = pl.program_id(1)

    @pl.when(t == 0)
    def _():
        st_ref[...] = st0_ref[...]
        tail_ref[...] = tail0_ref[...]

    @pl.when(pl.program_id(0) + t == 0)
    def _():
        hnext_s[...] = _rms(x_ref[0], nw_ref[...]).astype(BF16)
        grnext_s[...] = _dot_nt(hnext_s[...], wgr_ref[...])

    wup16_ref[...] = wup32_ref[...].astype(BF16)
    wdown16_ref[...] = wdown32_ref[...].astype(BF16)

    hn_s[...] = hnext_s[...]
    gk = _gate_log_decay(grnext_s[...], wgu_ref[...], bg_ref[...])
    qk = _dot_nt(hn_s[...], wa_ref[0:2 * QK, :])
    q = qk[:, 0:QK] * (DK ** -0.5)
    k = qk[:, QK:2 * QK]
    n_wide = MIX_TILE // WIDE_CHUNK
    gk_cols = jnp.concatenate([gk[c * WIDE_CHUNK:(c + 1) * WIDE_CHUNK] for c in range(n_wide)], axis=1)
    b_cols = _cumsum_rows(tri_ref[...], gk_cols)
    b = jnp.concatenate([b_cols[:, c * QK:(c + 1) * QK] for c in range(n_wide)], axis=0)
    kp = k * jnp.exp(-b)
    decay_last = [jnp.exp(b[rows.stop - 1:rows.stop, :]) for rows in _wide_chunks()]
    qp_s[...] = q * jnp.exp(b)
    kp_s[...] = kp
    n_bad = jnp.sum(jnp.where(jnp.abs(kp) <= WIDE_KEY_BOUND, 0.0, 1.0))
    wide_ok = n_bad == 0.0
    stprev_s[...] = st_ref[...]
    v_s[...] = _dot_nt(hn_s[...], wa_ref[2 * QK:2 * QK + GW, :])
    cc = _dot_nt(hn_s[...], wb_ref[CONV_CH:2 * CONV_CH, :])
    states = _wide_states(v_s, decay_last, kp_s, st_ref)
    cx = _dot_nt(hn_s[...], wb_ref[2 * CONV_CH:3 * CONV_CH, :])
    sc = _wide_scores(qp_s, kp_s, states)
    hnext = _rms(xnext_ref[0], nw_ref[...]).astype(BF16)
    hnext_s[...] = hnext
    grnext_s[...] = _dot_nt(hnext, wgr_ref[...])
    cb = _dot_nt(hn_s[...], wb_ref[0:CONV_CH, :])
    _wide_outputs(sc, v_s, o_s)
    gate_s[...] = _silu(_dot_nt(hn_s[...], wa_ref[2 * QK + GW:2 * QK + 2 * GW, :]))

    u = cc * cx
    first_row = lax.broadcasted_iota(jnp.int32, u.shape, 0) == 0
    u1 = jnp.where(first_row, tail_ref[SUBLANES - 1:SUBLANES, :], pltpu.roll(u, 1, axis=0))
    u2 = jnp.where(first_row, tail_ref[SUBLANES - 2:SUBLANES - 1, :], pltpu.roll(u1, 1, axis=0))
    tail_ref[...] = u[MIX_TILE - SUBLANES:MIX_TILE, :]
    yconv_s[...] = (cb * (cw_ref[0:1, :] * u2 + cw_ref[1:2, :] * u1 + cw_ref[2:3, :] * u)).astype(BF16)

    def mix_out():
        gnw = gnw_ref[...]
        y_gla = jnp.concatenate(
            [_rms(o_s[:, h * DV:(h + 1) * DV], gnw) for h in range(HEADS)], axis=1) * gate_s[...]
        mixed = (_dot(yconv_s[...], wout_ref[GW:GW + CONV_CH, :])
                 + _dot(y_gla.astype(BF16), wout_ref[0:GW, :]))
        h1_ref[0] = x_ref[0] + mixed

    mix_out()

    @pl.when(jnp.logical_not(wide_ok))
    def _():
        st_ref[...] = stprev_s[...]
        qk_again = _dot_nt(hn_s[...], wa_ref[0:2 * QK, :])
        q_s[...] = qk_again[:, 0:QK] * (DK ** -0.5)
        k_s[...] = qk_again[:, QK:2 * QK]
        gk_s[...] = _gate_log_decay(_dot_nt(hn_s[...], wgr_ref[...]), wgu_ref[...], bg_ref[...])
        _gla_stable(q_s, k_s, v_s, gk_s, o_s, st_ref, cmat_ref, lvl_ref)
        mix_out()


def _mlp_kernel(h_ref, nw_ref, wup_ref, wdown_ref, fw_ref, out_ref):
    blocks = [slice(r, r + MLP_ROWS) for r in range(0, MLP_TILE, MLP_ROWS)]
    hn = [_rms(h_ref[rows, :], nw_ref[...]).astype(BF16) for rows in blocks]
    acc = [jnp.zeros((MLP_ROWS, D_MODEL), F32) for _ in blocks]
    for f in range(0, D_FF, FF_CHUNK):
        for i in range(len(blocks)):
            up = _dot(hn[i], wup_ref[:, f:f + FF_CHUNK])
            act = jnp.square(jnp.maximum(up, 0.0)).astype(BF16)
            acc[i] = acc[i] + _dot(act, wdown_ref[f:f + FF_CHUNK, :])
    for i, rows in enumerate(blocks):
        out_ref[rows, :] = _rms(h_ref[rows, :] + acc[i], fw_ref[...])


def _tri3(n):
    tri = np.tril(np.ones((n, n), np.float32))
    return np.concatenate([tri, tri, tri], axis=1)


def _cumsum_and_ref_matrix():
    tri = np.tril(np.ones((CHUNK, CHUNK), np.float32))
    blocks = [tri]
    idx = np.arange(CHUNK)
    for w in HALF_WIDTHS:
        ref_row = (idx // (2 * w)) * (2 * w) + w - 1
        blocks.append(tri[ref_row])
    return np.concatenate(blocks, axis=0)


def _level_matrix():
    i = np.arange(CHUNK)[:, None]
    j = np.arange(CHUNK)[None, :]
    lvl = np.full((CHUNK, CHUNK), -1, np.int32)
    for l, w in enumerate(HALF_WIDTHS):
        same = (i // (2 * w)) == (j // (2 * w))
        lvl[same & ((i & w) != 0) & ((j & w) == 0)] = l
    lvl[i == j] = N_LEVELS
    return np.tile(lvl, (HEADS, 1))


def _const_spec(shape):
    return pl.BlockSpec(shape, lambda *_: (0,) * len(shape))


def kernel(x, meta_tokens, norm_mix_w, w_in, w_gate_up, b_gate, gla_norm_w, conv_w, w_out,
           norm_mlp_w, w_up, w_down, norm_final_w):
    batch, seq, _ = x.shape
    assert seq % MIX_TILE == 0 and (batch * seq) % MLP_TILE == 0
    assert norm_mix_w.shape[0] == 1, "single layer"

    bg = b_gate[0].reshape(1, QK)
    nw_mix = norm_mix_w[0].reshape(1, D_MODEL)
    gnw = gla_norm_w[0].reshape(1, DV)
    cw = conv_w[0]
    nw_mlp = norm_mlp_w[0].reshape(1, D_MODEL)
    fw = norm_final_w.reshape(1, D_MODEL)

    cmat = _cumsum_and_ref_matrix()
    cmat3 = jnp.asarray(np.concatenate([cmat, cmat, cmat], axis=1), BF16)
    tri_meta = jnp.asarray(_tri3(N_META), BF16)
    tri_wide = jnp.asarray(_tri3(WIDE_CHUNK), BF16)
    lvl = jnp.asarray(_level_matrix())

    w_in_t = jnp.swapaxes(w_in, 1, 2)[0]
    conv_row0 = GLA_ROWS + RANK
    assert GLA_ROWS == 3 * CONV_CH and GLA_ROWS % PREP_SLAB == 0 and D_MODEL % PREP_SLAB == 0
    n_slabs = GLA_ROWS // PREP_SLAB
    last_out_slab = D_MODEL // PREP_SLAB - 1
    slab = (PREP_SLAB, D_MODEL)
    wa, wb, w_gr6, wout, wgu6, st0, tail0 = pl.pallas_call(
        _prep_meta_kernel,
        grid=(n_slabs,),
        in_specs=[
            _const_spec((N_META, D_MODEL)),
            _const_spec((1, D_MODEL)),
            pl.BlockSpec(slab, lambda i: (i, 0)),
            pl.BlockSpec((pl.Element(PREP_SLAB), pl.Element(D_MODEL)),
                         lambda i: (pl.multiple_of(conv_row0 + i * PREP_SLAB, RANK), 0)),
            pl.BlockSpec((pl.Element(RANK), pl.Element(D_MODEL)), lambda i: (GLA_ROWS, 0)),
            pl.BlockSpec((1,) + slab, lambda i: (0, jnp.minimum(i, last_out_slab), 0)),
            _const_spec((1, RANK, QK)),
            _const_spec((1, QK)),
            _const_spec(tri_meta.shape),
        ],
        out_specs=(
            pl.BlockSpec(slab, lambda i: (i, 0)),
            pl.BlockSpec(slab, lambda i: (i, 0)),
            _const_spec((LANES, D_MODEL)),
            pl.BlockSpec(slab, lambda i: (jnp.minimum(i, last_out_slab), 0)),
            _const_spec((LANES, QK)),
            _const_spec((DV, QK)),
            _const_spec((SUBLANES, CONV_CH)),
        ),
        out_shape=(
            jax.ShapeDtypeStruct((GLA_ROWS, D_MODEL), BF16),
            jax.ShapeDtypeStruct((3 * CONV_CH, D_MODEL), BF16),
            jax.ShapeDtypeStruct((LANES, D_MODEL), BF16),
            jax.ShapeDtypeStruct((D_MODEL, D_MODEL), BF16),
            jax.ShapeDtypeStruct((LANES, QK), BF16),
            jax.ShapeDtypeStruct((DV, QK), F32),
            jax.ShapeDtypeStruct((SUBLANES, CONV_CH), F32),
        ),
        scratch_shapes=[
            pltpu.VMEM((N_META, D_MODEL), BF16),
            pltpu.VMEM((2 * n_slabs, N_META, PREP_SLAB), F32),
        ],
        compiler_params=pltpu.CompilerParams(
            vmem_limit_bytes=VMEM_LIMIT, dimension_semantics=("arbitrary",)),
        name="weight_prep_meta_state",
    )(meta_tokens, nw_mix, w_in_t, w_in_t, w_in_t, w_out, w_gate_up, bg, tri_meta)

    n_tiles = seq // MIX_TILE

    def next_tile(b, t):
        nxt = jnp.minimum(b * n_tiles + t + 1, batch * n_tiles - 1)
        return nxt // n_tiles, nxt % n_tiles, 0

    n_steps = batch * n_tiles
    assert D_MODEL % n_steps == 0 and D_FF % n_steps == 0
    up_slab, down_slab = D_MODEL // n_steps, D_FF // n_steps

    h1, wup, wdown = pl.pallas_call(
        _mixer_kernel,
        grid=(batch, n_tiles),
        in_specs=[
            pl.BlockSpec((1, MIX_TILE, D_MODEL), lambda b, t: (b, t, 0)),
            pl.BlockSpec((1, MIX_TILE, D_MODEL), next_tile),
            _const_spec((1, D_MODEL)),
            _const_spec(wa.shape),
            _const_spec(wb.shape),
            _const_spec(w_gr6.shape),
            _const_spec(wgu6.shape),
            _const_spec((1, QK)),
            _const_spec((1, DV)),
            _const_spec(cw.shape),
            _const_spec(wout.shape),
            _const_spec(tri_wide.shape),
            _const_spec(cmat3.shape),
            _const_spec(lvl.shape),
            _const_spec((DV, QK)),
            _const_spec((SUBLANES, CONV_CH)),
            pl.BlockSpec((up_slab, D_FF), lambda b, t: (b * n_tiles + t, 0)),
            pl.BlockSpec((down_slab, D_MODEL), lambda b, t: (b * n_tiles + t, 0)),
        ],
        out_specs=(
            pl.BlockSpec((1, MIX_TILE, D_MODEL), lambda b, t: (b, t, 0)),
            pl.BlockSpec((up_slab, D_FF), lambda b, t: (b * n_tiles + t, 0)),
            pl.BlockSpec((down_slab, D_MODEL), lambda b, t: (b * n_tiles + t, 0)),
        ),
        out_shape=(
            jax.ShapeDtypeStruct((batch, seq, D_MODEL), F32),
            jax.ShapeDtypeStruct((D_MODEL, D_FF), BF16),
            jax.ShapeDtypeStruct((D_FF, D_MODEL), BF16),
        ),
        scratch_shapes=[
            pltpu.VMEM((DV, QK), F32),
            pltpu.VMEM((DV, QK), F32),
            pltpu.VMEM((SUBLANES, CONV_CH), F32),
            pltpu.VMEM((MIX_TILE, D_MODEL), BF16),
            pltpu.VMEM((MIX_TILE, D_MODEL), BF16),
            pltpu.VMEM((MIX_TILE, LANES), F32),
            pltpu.VMEM((MIX_TILE, QK), F32),
            pltpu.VMEM((MIX_TILE, QK), F32),
            pltpu.VMEM((MIX_TILE, GW), F32),
            pltpu.VMEM((MIX_TILE, QK), F32),
            pltpu.VMEM((MIX_TILE, QK), F32),
            pltpu.VMEM((MIX_TILE, QK), F32),
            pltpu.VMEM((MIX_TILE, GW), F32),
            pltpu.VMEM((MIX_TILE, GW), F32),
            pltpu.VMEM((MIX_TILE, CONV_CH), BF16),
        ],
        compiler_params=pltpu.CompilerParams(
            vmem_limit_bytes=VMEM_LIMIT, dimension_semantics=("arbitrary", "arbitrary")),
        name="gla_conv_mixer",
    )(x, x, nw_mix, wa, wb, w_gr6, wgu6, bg, gnw, cw, wout, tri_wide, cmat3, lvl, st0, tail0,
      w_up[0], w_down[0])

    rows = batch * seq
    out = pl.pallas_call(
        _mlp_kernel,
        grid=(rows // MLP_TILE,),
        in_specs=[
            pl.BlockSpec((MLP_TILE, D_MODEL), lambda i: (i, 0)),
            _const_spec((1, D_MODEL)),
            _const_spec(wup.shape),
            _const_spec(wdown.shape),
            _const_spec((1, D_MODEL)),
        ],
        out_specs=pl.BlockSpec((MLP_TILE, D_MODEL), lambda i: (i, 0)),
        out_shape=jax.ShapeDtypeStruct((rows, D_MODEL), F32),
        compiler_params=pltpu.CompilerParams(
            vmem_limit_bytes=VMEM_LIMIT, dimension_semantics=("arbitrary",)),
        name="relu2_mlp_final_norm",
    )(h1.reshape(rows, D_MODEL), nw_mlp, wup, wdown, fw)
    return out.reshape(batch, seq, D_MODEL)
```

```python
import numpy as np
import jax
import jax.numpy as jnp
from jax import lax
from jax.experimental import pallas as pl
from jax.experimental.pallas import tpu as pltpu

D_MODEL = 1024
N_META = 16
HEADS = 4
DK = 64
DV = 128
QK = HEADS * DK
GW = HEADS * DV
RANK = 16
GATE_NORM = 16.0
CONV_CH = 512
D_FF = 4096
GLA_ROWS = 2 * QK + 2 * GW
PROJ_WIDTH = GLA_ROWS + RANK + 3 * CONV_CH
EPS = 1e-6

LANES = 128
SUBLANES = 8
VMEM_LIMIT = 56 * 1024 * 1024

MIX_TILE = 512
WIDE_CHUNK = 128
CHUNK = 64
HALF_WIDTHS = (32, 16, 8, 4, 2, 1)
N_LEVELS = len(HALF_WIDTHS)
WIDE_KEY_BOUND = 1e18
GATE_TERMS = 6
PREP_SLAB = 512
MLP_TILE = 1024
MLP_ROWS = 256
FF_CHUNK = 1024

BF16 = jnp.bfloat16
F32 = jnp.float32


def _rms(x, w):
    ms = jnp.mean(x * x, axis=-1, keepdims=True)
    return x * lax.rsqrt(ms + EPS) * w


def _log_sigmoid(z):
    return jnp.minimum(z, 0.0) - jnp.log(1.0 + jnp.exp(-jnp.abs(z)))


def _silu(g):
    half = 0.5 * g
    return half + half * jnp.tanh(half)


def _dot(a, b):
    return jnp.dot(a, b, preferred_element_type=F32)


def _dot_nt(a, b):
    return lax.dot_general(a, b, (((1,), (1,)), ((), ())), preferred_element_type=F32)


def _split3(x):
    hi = x.astype(BF16)
    r1 = x - hi.astype(F32)
    mid = r1.astype(BF16)
    lo = (r1 - mid.astype(F32)).astype(BF16)
    return hi, mid, lo


def _stack_heads(x):
    lane_head = lax.broadcasted_iota(jnp.int32, x.shape, 1) >> (DK.bit_length() - 1)
    return jnp.concatenate([jnp.where(lane_head == h, x, 0.0) for h in range(HEADS)], axis=0)


def _rows_by_head(v):
    return jnp.concatenate([v[:, h * DV:(h + 1) * DV] for h in range(HEADS)], axis=0)


def _gate_log_decay(gr6, wgu6, bg):
    hi, mid, lo = _split3(gr6)
    group = lax.broadcasted_iota(jnp.int32, gr6.shape, 1) >> (RANK.bit_length() - 1)
    lhs = jnp.where(group < 3, hi, jnp.where(group < 5, mid, lo))
    z = _dot(lhs, wgu6) + bg
    return _log_sigmoid(z) / GATE_NORM


def _cumsum_rows(tri2, gk):
    hi, mid, _ = _split3(gk)
    return _dot(tri2, jnp.concatenate([hi, mid], axis=0))


def _state_increment(kdec, v):
    return _dot(_rows_by_head(v).T.astype(BF16), _stack_heads(kdec).astype(BF16))


def _state_update(st, kdec, v, decay_last):
    return st * decay_last + _state_increment(kdec, v)


def _prep_meta_kernel(meta_ref, nw_ref, wa_ref, wb_ref, wg_ref, wout_ref, wgu_ref, bg_ref, tri_ref,
                      wa16_ref, wb16_ref, wgr6_ref, wout16_ref, wgu6_ref, st_ref, tail_ref,
                      hn_s, proj_s):
    i = pl.program_id(0)
    n_a = GLA_ROWS // PREP_SLAB

    @pl.when(i == 0)
    def _():
        hn_s[...] = _rms(meta_ref[...], nw_ref[...]).astype(BF16)
        g_hi, g_mid, g_lo = _split3(wgu_ref[0])
        zeros = jnp.zeros((LANES - GATE_TERMS * RANK, QK), BF16)
        wgu6_ref[...] = jnp.concatenate([g_hi, g_mid, g_lo, g_hi, g_mid, g_hi, zeros], axis=0)
        gr = wg_ref[...].astype(BF16)
        wgr6_ref[...] = jnp.concatenate(
            [gr] * GATE_TERMS + [jnp.zeros((LANES - GATE_TERMS * RANK, D_MODEL), BF16)], axis=0)

    wa = wa_ref[...].astype(BF16)
    wb = wb_ref[...].astype(BF16)
    wa16_ref[...] = wa
    wb16_ref[...] = wb
    wout16_ref[...] = wout_ref[0].astype(BF16)
    proj_s[i] = _dot_nt(hn_s[...], wa)
    proj_s[n_a + i] = _dot_nt(hn_s[...], wb)

    @pl.when(i == pl.num_programs(0) - 1)
    def _():
        proj = jnp.concatenate([proj_s[j] for j in range(2 * n_a)], axis=1)
        k = proj[:, QK:2 * QK]
        v = proj[:, 2 * QK:2 * QK + GW]
        cc = proj[:, GLA_ROWS + CONV_CH:GLA_ROWS + 2 * CONV_CH]
        cx = proj[:, GLA_ROWS + 2 * CONV_CH:GLA_ROWS + 3 * CONV_CH]
        gk = _gate_log_decay(_dot_nt(hn_s[...], wgr6_ref[...]), wgu6_ref[...], bg_ref[...])
        b = _cumsum_rows(tri_ref[...], gk)
        b_last = b[N_META - 1:N_META, :]
        st_ref[...] = _state_update(jnp.zeros((DV, QK), F32), k * jnp.exp(b_last - b), v, jnp.exp(b_last))
        u = cc * cx
        tail_ref[...] = u[N_META - SUBLANES:N_META, :]


def _wide_chunks():
    return [slice(c * WIDE_CHUNK, (c + 1) * WIDE_CHUNK) for c in range(MIX_TILE // WIDE_CHUNK)]


def _wide_states(v_s, decay_last, kp_s, st_ref):
    chunks = _wide_chunks()
    kv = [_state_increment(kp_s[rows, :] * decay_last[c], v_s[rows, :]) for c, rows in enumerate(chunks)]
    states = [st_ref[...]]
    for c in range(len(chunks)):
        states.append(states[c] * decay_last[c] + kv[c])
    st_ref[...] = states[-1]
    return states[:-1]


def _wide_scores(qp_s, kp_s, states):
    return [_dot_nt(_stack_heads(qp_s[rows, :]).astype(BF16),
                    jnp.concatenate([kp_s[rows, :].astype(BF16), states[c].astype(BF16)], axis=0))
            for c, rows in enumerate(_wide_chunks())]


def _wide_outputs(sc, v_s, o_s):
    row = lax.broadcasted_iota(jnp.int32, (HEADS * WIDE_CHUNK, WIDE_CHUNK), 0) & (WIDE_CHUNK - 1)
    col = lax.broadcasted_iota(jnp.int32, (HEADS * WIDE_CHUNK, WIDE_CHUNK), 1)
    causal = row >= col
    for c, rows in enumerate(_wide_chunks()):
        scores = jnp.where(causal, sc[c][:, 0:WIDE_CHUNK], 0.0).astype(BF16)
        for h in range(HEADS):
            hr = slice(h * WIDE_CHUNK, (h + 1) * WIDE_CHUNK)
            o_h = _dot(scores[hr], v_s[rows, h * DV:(h + 1) * DV].astype(BF16))
            o_s[rows, h * DV:(h + 1) * DV] = o_h + sc[c][hr, WIDE_CHUNK:WIDE_CHUNK + DV]


def _gla_stable(q_s, k_s, v_s, gk_s, o_s, st_ref, cmat_ref, lvl_ref):
    lvl = lvl_ref[...]

    def chunk_body(c, st):
        r0 = pl.multiple_of(c * CHUNK, CHUNK)
        q = q_s[pl.ds(r0, CHUNK), :]
        k = k_s[pl.ds(r0, CHUNK), :]
        v = v_s[pl.ds(r0, CHUNK), :]
        gk = gk_s[pl.ds(r0, CHUNK), :]
        br = _cumsum_rows(cmat_ref[...], gk)
        b = br[0:CHUNK]
        row = lax.broadcasted_iota(jnp.int32, (CHUNK, QK), 0)
        a = jnp.zeros((HEADS * CHUNK, CHUNK), F32)
        for l, w in enumerate(HALF_WIDTHS):
            ref = br[(l + 1) * CHUNK:(l + 2) * CHUNK]
            e = jnp.exp(-jnp.abs(b - ref))
            second = (row & w) != 0
            ql = jnp.where(second, q * e, 0.0)
            kl = jnp.where(second, 0.0, k * e)
            g = _dot_nt(_stack_heads(ql).astype(BF16), kl.astype(BF16))
            a = jnp.where(lvl == l, g, a)
        g = _dot_nt(_stack_heads(q).astype(BF16), k.astype(BF16))
        a = jnp.where(lvl == N_LEVELS, g, a)
        a = a.astype(BF16)
        o_inter = _dot_nt(_stack_heads(q * jnp.exp(b)).astype(BF16), st.astype(BF16))
        for h in range(HEADS):
            o_h = _dot(a[h * CHUNK:(h + 1) * CHUNK], v[:, h * DV:(h + 1) * DV].astype(BF16))
            o_s[pl.ds(r0, CHUNK), h * DV:(h + 1) * DV] = o_h + o_inter[h * CHUNK:(h + 1) * CHUNK]
        b_last = b[CHUNK - 1:CHUNK, :]
        return _state_update(st, k * jnp.exp(b_last - b), v, jnp.exp(b_last))

    st_ref[...] = lax.fori_loop(0, MIX_TILE // CHUNK, chunk_body, st_ref[...])


def _mixer_kernel(x_ref, xnext_ref, nw_ref, wa_ref, wb_ref, wgr_ref, wgu_ref, bg_ref, gnw_ref, cw_ref, wout_ref,
                  tri_ref, cmat_ref, lvl_ref, st0_ref, tail0_ref, wup32_ref, wdown32_ref,
                  h1_ref, wup16_ref, wdown16_ref,
                  st_ref, stprev_s, tail_ref, hn_s, hnext_s, grnext_s, q_s, k_s, v_s, gk_s, qp_s, kp_s, o_s,
                  gate_s, yconv_s):
    t = pl.program_id(1)

    @pl.when(t == 0)
    def _():
        st_ref[...] = st0_ref[...]
        tail_ref[...] = tail0_ref[...]

    @pl.when(pl.program_id(0) + t == 0)
    def _():
        hnext_s[...] = _rms(x_ref[0], nw_ref[...]).astype(BF16)
        grnext_s[...] = _dot_nt(hnext_s[...], wgr_ref[...])

    wup16_ref[...] = wup32_ref[...].astype(BF16)
    wdown16_ref[...] = wdown32_ref[...].astype(BF16)

    hn_s[...] = hnext_s[...]
    gk = _gate_log_decay(grnext_s[...], wgu_ref[...], bg_ref[...])
    qk = _dot_nt(hn_s[...], wa_ref[0:2 * QK, :])
    q = qk[:, 0:QK] * (DK ** -0.5)
    k = qk[:, QK:2 * QK]
    n_wide = MIX_TILE // WIDE_CHUNK
    gk_cols = jnp.concatenate([gk[c * WIDE_CHUNK:(c + 1) * WIDE_CHUNK] for c in range(n_wide)], axis=1)
    b_cols = _cumsum_rows(tri_ref[...], gk_cols)
    b = jnp.concatenate([b_cols[:, c * QK:(c + 1) * QK] for c in range(n_wide)], axis=0)
    kp = k * jnp.exp(-b)
    decay_last = [jnp.exp(b[rows.stop - 1:rows.stop, :]) for rows in _wide_chunks()]
    qp_s[...] = q * jnp.exp(b)
    kp_s[...] = kp
    n_bad = jnp.sum(jnp.where(jnp.abs(kp) <= WIDE_KEY_BOUND, 0.0, 1.0))
    wide_ok = n_bad == 0.0
    stprev_s[...] = st_ref[...]
    v_s[...] = _dot_nt(hn_s[...], wa_ref[2 * QK:2 * QK + GW, :])
    cc = _dot_nt(hn_s[...], wb_ref[CONV_CH:2 * CONV_CH, :])
    states = _wide_states(v_s, decay_last, kp_s, st_ref)
    cx = _dot_nt(hn_s[...], wb_ref[2 * CONV_CH:3 * CONV_CH, :])
    sc = _wide_scores(qp_s, kp_s, states)
    hnext = _rms(xnext_ref[0], nw_ref[...]).astype(BF16)
    hnext_s[...] = hnext
    grnext_s[...] = _dot_nt(hnext, wgr_ref[...])
    cb = _dot_nt(hn_s[...], wb_ref[0:CONV_CH, :])
    _wide_outputs(sc, v_s, o_s)
    gate_s[...] = _silu(_dot_nt(hn_s[...], wa_ref[2 * QK + GW:2 * QK + 2 * GW, :]))

    u = cc * cx
    first_row = lax.broadcasted_iota(jnp.int32, u.shape, 0) == 0
    u1 = jnp.where(first_row, tail_ref[SUBLANES - 1:SUBLANES, :], pltpu.roll(u, 1, axis=0))
    u2 = jnp.where(first_row, tail_ref[SUBLANES - 2:SUBLANES - 1, :], pltpu.roll(u1, 1, axis=0))
    tail_ref[...] = u[MIX_TILE - SUBLANES:MIX_TILE, :]
    yconv_s[...] = (cb * (cw_ref[0:1, :] * u2 + cw_ref[1:2, :] * u1 + cw_ref[2:3, :] * u)).astype(BF16)

    def mix_out():
        gnw = gnw_ref[...]
        y_gla = jnp.concatenate(
            [_rms(o_s[:, h * DV:(h + 1) * DV], gnw) for h in range(HEADS)], axis=1) * gate_s[...]
        mixed = (_dot(yconv_s[...], wout_ref[GW:GW + CONV_CH, :])
                 + _dot(y_gla.astype(BF16), wout_ref[0:GW, :]))
        h1_ref[0] = x_ref[0] + mixed

    mix_out()

    @pl.when(jnp.logical_not(wide_ok))
    def _():
        st_ref[...] = stprev_s[...]
        qk_again = _dot_nt(hn_s[...], wa_ref[0:2 * QK, :])
        q_s[...] = qk_again[:, 0:QK] * (DK ** -0.5)
        k_s[...] = qk_again[:, QK:2 * QK]
        gk_s[...] = _gate_log_decay(_dot_nt(hn_s[...], wgr_ref[...]), wgu_ref[...], bg_ref[...])
        _gla_stable(q_s, k_s, v_s, gk_s, o_s, st_ref, cmat_ref, lvl_ref)
        mix_out()


def _mlp_kernel(h_ref, nw_ref, wup_ref, wdown_ref, fw_ref, out_ref):
    blocks = [slice(r, r + MLP_ROWS) for r in range(0, MLP_TILE, MLP_ROWS)]
    hn = [_rms(h_ref[rows, :], nw_ref[...]).astype(BF16) for rows in blocks]
    acc = [jnp.zeros((MLP_ROWS, D_MODEL), F32) for _ in blocks]
    for f in range(0, D_FF, FF_CHUNK):
        for i in range(len(blocks)):
            up = _dot(hn[i], wup_ref[:, f:f + FF_CHUNK])
            act = jnp.square(jnp.maximum(up, 0.0)).astype(BF16)
            acc[i] = acc[i] + _dot(act, wdown_ref[f:f + FF_CHUNK, :])
    for i, rows in enumerate(blocks):
        out_ref[rows, :] = _rms(h_ref[rows, :] + acc[i], fw_ref[...])


def _tri2(n):
    tri = np.tril(np.ones((n, n), np.float32))
    return np.concatenate([tri, tri], axis=1)


def _cumsum_and_ref_matrix():
    tri = np.tril(np.ones((CHUNK, CHUNK), np.float32))
    blocks = [tri]
    idx = np.arange(CHUNK)
    for w in HALF_WIDTHS:
        ref_row = (idx // (2 * w)) * (2 * w) + w - 1
        blocks.append(tri[ref_row])
    return np.concatenate(blocks, axis=0)


def _level_matrix():
    i = np.arange(CHUNK)[:, None]
    j = np.arange(CHUNK)[None, :]
    lvl = np.full((CHUNK, CHUNK), -1, np.int32)
    for l, w in enumerate(HALF_WIDTHS):
        same = (i // (2 * w)) == (j // (2 * w))
        lvl[same & ((i & w) != 0) & ((j & w) == 0)] = l
    lvl[i == j] = N_LEVELS
    return np.tile(lvl, (HEADS, 1))


def _const_spec(shape):
    return pl.BlockSpec(shape, lambda *_: (0,) * len(shape))


def kernel(x, meta_tokens, norm_mix_w, w_in, w_gate_up, b_gate, gla_norm_w, conv_w, w_out,
           norm_mlp_w, w_up, w_down, norm_final_w):
    batch, seq, _ = x.shape
    assert seq % MIX_TILE == 0 and (batch * seq) % MLP_TILE == 0
    assert norm_mix_w.shape[0] == 1, "single layer"

    bg = b_gate[0].reshape(1, QK)
    nw_mix = norm_mix_w[0].reshape(1, D_MODEL)
    gnw = gla_norm_w[0].reshape(1, DV)
    cw = conv_w[0]
    nw_mlp = norm_mlp_w[0].reshape(1, D_MODEL)
    fw = norm_final_w.reshape(1, D_MODEL)

    cmat = _cumsum_and_ref_matrix()
    cmat2 = jnp.asarray(np.concatenate([cmat, cmat], axis=1), BF16)
    tri_meta = jnp.asarray(_tri2(N_META), BF16)
    tri_wide = jnp.asarray(_tri2(WIDE_CHUNK), BF16)
    lvl = jnp.asarray(_level_matrix())

    w_in_t = jnp.swapaxes(w_in, 1, 2)[0]
    conv_row0 = GLA_ROWS + RANK
    assert GLA_ROWS == 3 * CONV_CH and GLA_ROWS % PREP_SLAB == 0 and D_MODEL % PREP_SLAB == 0
    n_slabs = GLA_ROWS // PREP_SLAB
    last_out_slab = D_MODEL // PREP_SLAB - 1
    slab = (PREP_SLAB, D_MODEL)
    wa, wb, w_gr6, wout, wgu6, st0, tail0 = pl.pallas_call(
        _prep_meta_kernel,
        grid=(n_slabs,),
        in_specs=[
            _const_spec((N_META, D_MODEL)),
            _const_spec((1, D_MODEL)),
            pl.BlockSpec(slab, lambda i: (i, 0)),
            pl.BlockSpec((pl.Element(PREP_SLAB), pl.Element(D_MODEL)),
                         lambda i: (pl.multiple_of(conv_row0 + i * PREP_SLAB, RANK), 0)),
            pl.BlockSpec((pl.Element(RANK), pl.Element(D_MODEL)), lambda i: (GLA_ROWS, 0)),
            pl.BlockSpec((1,) + slab, lambda i: (0, jnp.minimum(i, last_out_slab), 0)),
            _const_spec((1, RANK, QK)),
            _const_spec((1, QK)),
            _const_spec(tri_meta.shape),
        ],
        out_specs=(
            pl.BlockSpec(slab, lambda i: (i, 0)),
            pl.BlockSpec(slab, lambda i: (i, 0)),
            _const_spec((LANES, D_MODEL)),
            pl.BlockSpec(slab, lambda i: (jnp.minimum(i, last_out_slab), 0)),
            _const_spec((LANES, QK)),
            _const_spec((DV, QK)),
            _const_spec((SUBLANES, CONV_CH)),
        ),
        out_shape=(
            jax.ShapeDtypeStruct((GLA_ROWS, D_MODEL), BF16),
            jax.ShapeDtypeStruct((3 * CONV_CH, D_MODEL), BF16),
            jax.ShapeDtypeStruct((LANES, D_MODEL), BF16),
            jax.ShapeDtypeStruct((D_MODEL, D_MODEL), BF16),
            jax.ShapeDtypeStruct((LANES, QK), BF16),
            jax.ShapeDtypeStruct((DV, QK), F32),
            jax.ShapeDtypeStruct((SUBLANES, CONV_CH), F32),
        ),
        scratch_shapes=[
            pltpu.VMEM((N_META, D_MODEL), BF16),
            pltpu.VMEM((2 * n_slabs, N_META, PREP_SLAB), F32),
        ],
        compiler_params=pltpu.CompilerParams(
            vmem_limit_bytes=VMEM_LIMIT, dimension_semantics=("arbitrary",)),
        name="weight_prep_meta_state",
    )(meta_tokens, nw_mix, w_in_t, w_in_t, w_in_t, w_out, w_gate_up, bg, tri_meta)

    n_tiles = seq // MIX_TILE

    def next_tile(b, t):
        nxt = jnp.minimum(b * n_tiles + t + 1, batch * n_tiles - 1)
        return nxt // n_tiles, nxt % n_tiles, 0

    n_steps = batch * n_tiles
    assert D_MODEL % n_steps == 0 and D_FF % n_steps == 0
    up_slab, down_slab = D_MODEL // n_steps, D_FF // n_steps

    h1, wup, wdown = pl.pallas_call(
        _mixer_kernel,
        grid=(batch, n_tiles),
        in_specs=[
            pl.BlockSpec((1, MIX_TILE, D_MODEL), lambda b, t: (b, t, 0)),
            pl.BlockSpec((1, MIX_TILE, D_MODEL), next_tile),
            _const_spec((1, D_MODEL)),
            _const_spec(wa.shape),
            _const_spec(wb.shape),
            _const_spec(w_gr6.shape),
            _const_spec(wgu6.shape),
            _const_spec((1, QK)),
            _const_spec((1, DV)),
            _const_spec(cw.shape),
            _const_spec(wout.shape),
            _const_spec(tri_wide.shape),
            _const_spec(cmat2.shape),
            _const_spec(lvl.shape),
            _const_spec((DV, QK)),
            _const_spec((SUBLANES, CONV_CH)),
            pl.BlockSpec((up_slab, D_FF), lambda b, t: (b * n_tiles + t, 0)),
            pl.BlockSpec((down_slab, D_MODEL), lambda b, t: (b * n_tiles + t, 0)),
        ],
        out_specs=(
            pl.BlockSpec((1, MIX_TILE, D_MODEL), lambda b, t: (b, t, 0)),
            pl.BlockSpec((up_slab, D_FF), lambda b, t: (b * n_tiles + t, 0)),
            pl.BlockSpec((down_slab, D_MODEL), lambda b, t: (b * n_tiles + t, 0)),
        ),
        out_shape=(
            jax.ShapeDtypeStruct((batch, seq, D_MODEL), F32),
            jax.ShapeDtypeStruct((D_MODEL, D_FF), BF16),
            jax.ShapeDtypeStruct((D_FF, D_MODEL), BF16),
        ),
        scratch_shapes=[
            pltpu.VMEM((DV, QK), F32),
            pltpu.VMEM((DV, QK), F32),
            pltpu.VMEM((SUBLANES, CONV_CH), F32),
            pltpu.VMEM((MIX_TILE, D_MODEL), BF16),
            pltpu.VMEM((MIX_TILE, D_MODEL), BF16),
            pltpu.VMEM((MIX_TILE, LANES), F32),
            pltpu.VMEM((MIX_TILE, QK), F32),
            pltpu.VMEM((MIX_TILE, QK), F32),
            pltpu.VMEM((MIX_TILE, GW), F32),
            pltpu.VMEM((MIX_TILE, QK), F32),
            pltpu.VMEM((MIX_TILE, QK), F32),
            pltpu.VMEM((MIX_TILE, QK), F32),
            pltpu.VMEM((MIX_TILE, GW), F32),
            pltpu.VMEM((MIX_TILE, GW), F32),
            pltpu.VMEM((MIX_TILE, CONV_CH), BF16),
        ],
        compiler_params=pltpu.CompilerParams(
            vmem_limit_bytes=VMEM_LIMIT, dimension_semantics=("arbitrary", "arbitrary")),
        name="gla_conv_mixer",
    )(x, x, nw_mix, wa, wb, w_gr6, wgu6, bg, gnw, cw, wout, tri_wide, cmat2, lvl, st0, tail0,
      w_up[0], w_down[0])

    rows = batch * seq
    out = pl.pallas_call(
        _mlp_kernel,
        grid=(rows // MLP_TILE,),
        in_specs=[
            pl.BlockSpec((MLP_TILE, D_MODEL), lambda i: (i, 0)),
            _const_spec((1, D_MODEL)),
            _const_spec(wup.shape),
            _const_spec(wdown.shape),
            _const_spec((1, D_MODEL)),
        ],
        out_specs=pl.BlockSpec((MLP_TILE, D_MODEL), lambda i: (i, 0)),
        out_shape=jax.ShapeDtypeStruct((rows, D_MODEL), F32),
        compiler_params=pltpu.CompilerParams(
            vmem_limit_bytes=VMEM_LIMIT, dimension_semantics=("arbitrary",)),
        name="relu2_mlp_final_norm",
    )(h1.reshape(rows, D_MODEL), nw_mlp, wup, wdown, fw)
    return out.reshape(batch, seq, D_MODEL)
```

```python
import numpy as np
import jax
import jax.numpy as jnp
from jax import lax
from jax.experimental import pallas as pl
from jax.experimental.pallas import tpu as pltpu

D_MODEL = 1024
N_META = 16
HEADS = 4
DK = 64
DV = 128
QK = HEADS * DK
GW = HEADS * DV
RANK = 16
GATE_NORM = 16.0
CONV_CH = 512
D_FF = 4096
GLA_ROWS = 2 * QK + 2 * GW
PROJ_WIDTH = GLA_ROWS + RANK + 3 * CONV_CH
EPS = 1e-6

LANES = 128
SUBLANES = 8
VMEM_LIMIT = 56 * 1024 * 1024

MIX_TILE = 512
WIDE_CHUNK = 128
CHUNK = 64
HALF_WIDTHS = (32, 16, 8, 4, 2, 1)
N_LEVELS = len(HALF_WIDTHS)
WIDE_KEY_BOUND = 1e18
GATE_TERMS = 6
PREP_SLAB = 512
MLP_TILE = 1024
MLP_ROWS = 256
FF_CHUNK = 1024

BF16 = jnp.bfloat16
F32 = jnp.float32


def _rms(x, w):
    ms = jnp.mean(x * x, axis=-1, keepdims=True)
    return x * lax.rsqrt(ms + EPS) * w


def _log_sigmoid(z):
    return jnp.minimum(z, 0.0) - jnp.log(1.0 + jnp.exp(-jnp.abs(z)))


def _silu(g):
    half = 0.5 * g
    return half + half * jnp.tanh(half)


def _dot(a, b):
    return jnp.dot(a, b, preferred_element_type=F32)


def _dot_nt(a, b):
    return lax.dot_general(a, b, (((1,), (1,)), ((), ())), preferred_element_type=F32)


def _split3(x):
    hi = x.astype(BF16)
    r1 = x - hi.astype(F32)
    mid = r1.astype(BF16)
    lo = (r1 - mid.astype(F32)).astype(BF16)
    return hi, mid, lo


def _stack_heads(x):
    lane_head = lax.broadcasted_iota(jnp.int32, x.shape, 1) >> (DK.bit_length() - 1)
    return jnp.concatenate([jnp.where(lane_head == h, x, 0.0) for h in range(HEADS)], axis=0)


def _rows_by_head(v):
    return jnp.concatenate([v[:, h * DV:(h + 1) * DV] for h in range(HEADS)], axis=0)


def _gate_log_decay(gr6, wgu6, bg):
    hi, mid, lo = _split3(gr6)
    group = lax.broadcasted_iota(jnp.int32, gr6.shape, 1) >> (RANK.bit_length() - 1)
    lhs = jnp.where(group < 3, hi, jnp.where(group < 5, mid, lo))
    z = _dot(lhs, wgu6) + bg
    return _log_sigmoid(z) / GATE_NORM


def _cumsum_rows(tri2, gk):
    hi, mid, _ = _split3(gk)
    return _dot(tri2, jnp.concatenate([hi, mid], axis=0))


def _state_increment(kdec, v):
    return _dot(_rows_by_head(v).T.astype(BF16), _stack_heads(kdec).astype(BF16))


def _state_update(st, kdec, v, decay_last):
    return st * decay_last + _state_increment(kdec, v)


def _prep_meta_kernel(meta_ref, nw_ref, wa_ref, wb_ref, wg_ref, wout_ref, wgu_ref, bg_ref, tri_ref,
                      wa16_ref, wb16_ref, wgr6_ref, wout16_ref, wgu6_ref, st_ref, tail_ref,
                      hn_s, proj_s):
    i = pl.program_id(0)
    n_a = GLA_ROWS // PREP_SLAB

    @pl.when(i == 0)
    def _():
        hn_s[...] = _rms(meta_ref[...], nw_ref[...]).astype(BF16)
        g_hi, g_mid, g_lo = _split3(wgu_ref[0])
        zeros = jnp.zeros((LANES - GATE_TERMS * RANK, QK), BF16)
        wgu6_ref[...] = jnp.concatenate([g_hi, g_mid, g_lo, g_hi, g_mid, g_hi, zeros], axis=0)
        gr = wg_ref[...].astype(BF16)
        wgr6_ref[...] = jnp.concatenate(
            [gr] * GATE_TERMS + [jnp.zeros((LANES - GATE_TERMS * RANK, D_MODEL), BF16)], axis=0)

    wa = wa_ref[...].astype(BF16)
    wb = wb_ref[...].astype(BF16)
    wa16_ref[...] = wa
    wb16_ref[...] = wb
    wout16_ref[...] = wout_ref[0].astype(BF16)
    proj_s[i] = _dot_nt(hn_s[...], wa)
    proj_s[n_a + i] = _dot_nt(hn_s[...], wb)

    @pl.when(i == pl.num_programs(0) - 1)
    def _():
        proj = jnp.concatenate([proj_s[j] for j in range(2 * n_a)], axis=1)
        k = proj[:, QK:2 * QK]
        v = proj[:, 2 * QK:2 * QK + GW]
        cc = proj[:, GLA_ROWS + CONV_CH:GLA_ROWS + 2 * CONV_CH]
        cx = proj[:, GLA_ROWS + 2 * CONV_CH:GLA_ROWS + 3 * CONV_CH]
        gk = _gate_log_decay(_dot_nt(hn_s[...], wgr6_ref[...]), wgu6_ref[...], bg_ref[...])
        b = _cumsum_rows(tri_ref[...], gk)
        b_last = b[N_META - 1:N_META, :]
        st_ref[...] = _state_update(jnp.zeros((DV, QK), F32), k * jnp.exp(b_last - b), v, jnp.exp(b_last))
        u = cc * cx
        tail_ref[...] = u[N_META - SUBLANES:N_META, :]


def _wide_chunks():
    return [slice(c * WIDE_CHUNK, (c + 1) * WIDE_CHUNK) for c in range(MIX_TILE // WIDE_CHUNK)]


def _wide_states(v_s, decay_last, kp_s, st_ref):
    chunks = _wide_chunks()
    kv = [_state_increment(kp_s[rows, :] * decay_last[c], v_s[rows, :]) for c, rows in enumerate(chunks)]
    states = [st_ref[...]]
    for c in range(len(chunks)):
        states.append(states[c] * decay_last[c] + kv[c])
    st_ref[...] = states[-1]
    return states[:-1]


def _wide_scores(qp_s, kp_s, states):
    return [_dot_nt(_stack_heads(qp_s[rows, :]).astype(BF16),
                    jnp.concatenate([kp_s[rows, :].astype(BF16), states[c].astype(BF16)], axis=0))
            for c, rows in enumerate(_wide_chunks())]


def _wide_outputs(sc, v_s, o_s):
    row = lax.broadcasted_iota(jnp.int32, (HEADS * WIDE_CHUNK, WIDE_CHUNK), 0) & (WIDE_CHUNK - 1)
    col = lax.broadcasted_iota(jnp.int32, (HEADS * WIDE_CHUNK, WIDE_CHUNK), 1)
    causal = row >= col
    for c, rows in enumerate(_wide_chunks()):
        scores = jnp.where(causal, sc[c][:, 0:WIDE_CHUNK], 0.0).astype(BF16)
        for h in range(HEADS):
            hr = slice(h * WIDE_CHUNK, (h + 1) * WIDE_CHUNK)
            o_h = _dot(scores[hr], v_s[rows, h * DV:(h + 1) * DV].astype(BF16))
            o_s[rows, h * DV:(h + 1) * DV] = o_h + sc[c][hr, WIDE_CHUNK:WIDE_CHUNK + DV]


def _gla_stable(q_s, k_s, v_s, gk_s, o_s, st_ref, cmat_ref, lvl_ref):
    lvl = lvl_ref[...]

    def chunk_body(c, st):
        r0 = pl.multiple_of(c * CHUNK, CHUNK)
        q = q_s[pl.ds(r0, CHUNK), :]
        k = k_s[pl.ds(r0, CHUNK), :]
        v = v_s[pl.ds(r0, CHUNK), :]
        gk = gk_s[pl.ds(r0, CHUNK), :]
        br = _cumsum_rows(cmat_ref[...], gk)
        b = br[0:CHUNK]
        row = lax.broadcasted_iota(jnp.int32, (CHUNK, QK), 0)
        a = jnp.zeros((HEADS * CHUNK, CHUNK), F32)
        for l, w in enumerate(HALF_WIDTHS):
            ref = br[(l + 1) * CHUNK:(l + 2) * CHUNK]
            e = jnp.exp(-jnp.abs(b - ref))
            second = (row & w) != 0
            ql = jnp.where(second, q * e, 0.0)
            kl = jnp.where(second, 0.0, k * e)
            g = _dot_nt(_stack_heads(ql).astype(BF16), kl.astype(BF16))
            a = jnp.where(lvl == l, g, a)
        g = _dot_nt(_stack_heads(q).astype(BF16), k.astype(BF16))
        a = jnp.where(lvl == N_LEVELS, g, a)
        a = a.astype(BF16)
        o_inter = _dot_nt(_stack_heads(q * jnp.exp(b)).astype(BF16), st.astype(BF16))
        for h in range(HEADS):
            o_h = _dot(a[h * CHUNK:(h + 1) * CHUNK], v[:, h * DV:(h + 1) * DV].astype(BF16))
            o_s[pl.ds(r0, CHUNK), h * DV:(h + 1) * DV] = o_h + o_inter[h * CHUNK:(h + 1) * CHUNK]
        b_last = b[CHUNK - 1:CHUNK, :]
        return _state_update(st, k * jnp.exp(b_last - b), v, jnp.exp(b_last))

    st_ref[...] = lax.fori_loop(0, MIX_TILE // CHUNK, chunk_body, st_ref[...])


def _mixer_kernel(x_ref, xnext_ref, nw_ref, wa_ref, wb_ref, wgr_ref, wgu_ref, bg_ref, gnw_ref, cw_ref, wout_ref,
                  tri_ref, cmat_ref, lvl_ref, st0_ref, tail0_ref, wup32_ref, wdown32_ref,
                  h1_ref, wup16_ref, wdown16_ref,
                  st_ref, stprev_s, tail_ref, hn_a, hn_b, gr_a, gr_b, q_s, k_s, v_s, gk_s, qp_s, kp_s, o_s,
                  gate_s, yconv_s):
    pair = pl.program_id(1)

    @pl.when(pair == 0)
    def _():
        st_ref[...] = st0_ref[...]
        tail_ref[...] = tail0_ref[...]

    @pl.when(pl.program_id(0) + pair == 0)
    def _():
        hn_a[...] = _rms(x_ref[0, 0:MIX_TILE, :], nw_ref[...]).astype(BF16)
        gr_a[...] = _dot_nt(hn_a[...], wgr_ref[...])

    wup16_ref[...] = wup32_ref[...].astype(BF16)
    wdown16_ref[...] = wdown32_ref[...].astype(BF16)

    def run_tile(rows, load_next_x, hn_s, gr_s, hn_next_s, gr_next_s):
        gk = _gate_log_decay(gr_s[...], wgu_ref[...], bg_ref[...])
        qk = _dot_nt(hn_s[...], wa_ref[0:2 * QK, :])
        q = qk[:, 0:QK] * (DK ** -0.5)
        k = qk[:, QK:2 * QK]
        n_wide = MIX_TILE // WIDE_CHUNK
        gk_cols = jnp.concatenate([gk[c * WIDE_CHUNK:(c + 1) * WIDE_CHUNK] for c in range(n_wide)], axis=1)
        b_cols = _cumsum_rows(tri_ref[...], gk_cols)
        b = jnp.concatenate([b_cols[:, c * QK:(c + 1) * QK] for c in range(n_wide)], axis=0)
        kp = k * jnp.exp(-b)
        decay_last = [jnp.exp(b[r.stop - 1:r.stop, :]) for r in _wide_chunks()]
        qp_s[...] = q * jnp.exp(b)
        kp_s[...] = kp
        n_bad = jnp.sum(jnp.where(jnp.abs(kp) <= WIDE_KEY_BOUND, 0.0, 1.0))
        wide_ok = n_bad == 0.0
        stprev_s[...] = st_ref[...]
        v_s[...] = _dot_nt(hn_s[...], wa_ref[2 * QK:2 * QK + GW, :])
        cc = _dot_nt(hn_s[...], wb_ref[CONV_CH:2 * CONV_CH, :])
        states = _wide_states(v_s, decay_last, kp_s, st_ref)
        cx = _dot_nt(hn_s[...], wb_ref[2 * CONV_CH:3 * CONV_CH, :])
        sc = _wide_scores(qp_s, kp_s, states)
        hnext = _rms(load_next_x(), nw_ref[...]).astype(BF16)
        hn_next_s[...] = hnext
        gr_next_s[...] = _dot_nt(hnext, wgr_ref[...])
        cb = _dot_nt(hn_s[...], wb_ref[0:CONV_CH, :])
        _wide_outputs(sc, v_s, o_s)
        gate_s[...] = _silu(_dot_nt(hn_s[...], wa_ref[2 * QK + GW:2 * QK + 2 * GW, :]))

        u = cc * cx
        first_row = lax.broadcasted_iota(jnp.int32, u.shape, 0) == 0
        u1 = jnp.where(first_row, tail_ref[SUBLANES - 1:SUBLANES, :], pltpu.roll(u, 1, axis=0))
        u2 = jnp.where(first_row, tail_ref[SUBLANES - 2:SUBLANES - 1, :], pltpu.roll(u1, 1, axis=0))
        tail_ref[...] = u[MIX_TILE - SUBLANES:MIX_TILE, :]
        yconv_s[...] = (cb * (cw_ref[0:1, :] * u2 + cw_ref[1:2, :] * u1 + cw_ref[2:3, :] * u)).astype(BF16)

        def mix_out():
            gnw = gnw_ref[...]
            y_gla = jnp.concatenate(
                [_rms(o_s[:, h * DV:(h + 1) * DV], gnw) for h in range(HEADS)], axis=1) * gate_s[...]
            mixed = (_dot(yconv_s[...], wout_ref[GW:GW + CONV_CH, :])
                     + _dot(y_gla.astype(BF16), wout_ref[0:GW, :]))
            h1_ref[0, rows, :] = x_ref[0, rows, :] + mixed

        mix_out()

        @pl.when(jnp.logical_not(wide_ok))
        def _():
            st_ref[...] = stprev_s[...]
            qk_again = _dot_nt(hn_s[...], wa_ref[0:2 * QK, :])
            q_s[...] = qk_again[:, 0:QK] * (DK ** -0.5)
            k_s[...] = qk_again[:, QK:2 * QK]
            gk_s[...] = _gate_log_decay(_dot_nt(hn_s[...], wgr_ref[...]), wgu_ref[...], bg_ref[...])
            _gla_stable(q_s, k_s, v_s, gk_s, o_s, st_ref, cmat_ref, lvl_ref)
            mix_out()

    first, second = slice(0, MIX_TILE), slice(MIX_TILE, 2 * MIX_TILE)
    run_tile(first, lambda: x_ref[0, second, :], hn_a, gr_a, hn_b, gr_b)
    run_tile(second, lambda: xnext_ref[0], hn_b, gr_b, hn_a, gr_a)


def _mlp_kernel(h_ref, nw_ref, wup_ref, wdown_ref, fw_ref, out_ref):
    blocks = [slice(r, r + MLP_ROWS) for r in range(0, MLP_TILE, MLP_ROWS)]
    hn = [_rms(h_ref[rows, :], nw_ref[...]).astype(BF16) for rows in blocks]
    acc = [jnp.zeros((MLP_ROWS, D_MODEL), F32) for _ in blocks]
    for f in range(0, D_FF, FF_CHUNK):
        for i in range(len(blocks)):
            up = _dot(hn[i], wup_ref[:, f:f + FF_CHUNK])
            act = jnp.square(jnp.maximum(up, 0.0)).astype(BF16)
            acc[i] = acc[i] + _dot(act, wdown_ref[f:f + FF_CHUNK, :])
    for i, rows in enumerate(blocks):
        out_ref[rows, :] = _rms(h_ref[rows, :] + acc[i], fw_ref[...])


def _tri2(n):
    tri = np.tril(np.ones((n, n), np.float32))
    return np.concatenate([tri, tri], axis=1)


def _cumsum_and_ref_matrix():
    tri = np.tril(np.ones((CHUNK, CHUNK), np.float32))
    blocks = [tri]
    idx = np.arange(CHUNK)
    for w in HALF_WIDTHS:
        ref_row = (idx // (2 * w)) * (2 * w) + w - 1
        blocks.append(tri[ref_row])
    return np.concatenate(blocks, axis=0)


def _level_matrix():
    i = np.arange(CHUNK)[:, None]
    j = np.arange(CHUNK)[None, :]
    lvl = np.full((CHUNK, CHUNK), -1, np.int32)
    for l, w in enumerate(HALF_WIDTHS):
        same = (i // (2 * w)) == (j // (2 * w))
        lvl[same & ((i & w) != 0) & ((j & w) == 0)] = l
    lvl[i == j] = N_LEVELS
    return np.tile(lvl, (HEADS, 1))


def _const_spec(shape):
    return pl.BlockSpec(shape, lambda *_: (0,) * len(shape))


def kernel(x, meta_tokens, norm_mix_w, w_in, w_gate_up, b_gate, gla_norm_w, conv_w, w_out,
           norm_mlp_w, w_up, w_down, norm_final_w):
    batch, seq, _ = x.shape
    assert seq % (2 * MIX_TILE) == 0 and (batch * seq) % MLP_TILE == 0
    assert norm_mix_w.shape[0] == 1, "single layer"

    bg = b_gate[0].reshape(1, QK)
    nw_mix = norm_mix_w[0].reshape(1, D_MODEL)
    gnw = gla_norm_w[0].reshape(1, DV)
    cw = conv_w[0]
    nw_mlp = norm_mlp_w[0].reshape(1, D_MODEL)
    fw = norm_final_w.reshape(1, D_MODEL)

    cmat = _cumsum_and_ref_matrix()
    cmat2 = jnp.asarray(np.concatenate([cmat, cmat], axis=1), BF16)
    tri_meta = jnp.asarray(_tri2(N_META), BF16)
    tri_wide = jnp.asarray(_tri2(WIDE_CHUNK), BF16)
    lvl = jnp.asarray(_level_matrix())

    w_in_t = jnp.swapaxes(w_in, 1, 2)[0]
    conv_row0 = GLA_ROWS + RANK
    assert GLA_ROWS == 3 * CONV_CH and GLA_ROWS % PREP_SLAB == 0 and D_MODEL % PREP_SLAB == 0
    n_slabs = GLA_ROWS // PREP_SLAB
    last_out_slab = D_MODEL // PREP_SLAB - 1
    slab = (PREP_SLAB, D_MODEL)
    wa, wb, w_gr6, wout, wgu6, st0, tail0 = pl.pallas_call(
        _prep_meta_kernel,
        grid=(n_slabs,),
        in_specs=[
            _const_spec((N_META, D_MODEL)),
            _const_spec((1, D_MODEL)),
            pl.BlockSpec(slab, lambda i: (i, 0)),
            pl.BlockSpec((pl.Element(PREP_SLAB), pl.Element(D_MODEL)),
                         lambda i: (pl.multiple_of(conv_row0 + i * PREP_SLAB, RANK), 0)),
            pl.BlockSpec((pl.Element(RANK), pl.Element(D_MODEL)), lambda i: (GLA_ROWS, 0)),
            pl.BlockSpec((1,) + slab, lambda i: (0, jnp.minimum(i, last_out_slab), 0)),
            _const_spec((1, RANK, QK)),
            _const_spec((1, QK)),
            _const_spec(tri_meta.shape),
        ],
        out_specs=(
            pl.BlockSpec(slab, lambda i: (i, 0)),
            pl.BlockSpec(slab, lambda i: (i, 0)),
            _const_spec((LANES, D_MODEL)),
            pl.BlockSpec(slab, lambda i: (jnp.minimum(i, last_out_slab), 0)),
            _const_spec((LANES, QK)),
            _const_spec((DV, QK)),
            _const_spec((SUBLANES, CONV_CH)),
        ),
        out_shape=(
            jax.ShapeDtypeStruct((GLA_ROWS, D_MODEL), BF16),
            jax.ShapeDtypeStruct((3 * CONV_CH, D_MODEL), BF16),
            jax.ShapeDtypeStruct((LANES, D_MODEL), BF16),
            jax.ShapeDtypeStruct((D_MODEL, D_MODEL), BF16),
            jax.ShapeDtypeStruct((LANES, QK), BF16),
            jax.ShapeDtypeStruct((DV, QK), F32),
            jax.ShapeDtypeStruct((SUBLANES, CONV_CH), F32),
        ),
        scratch_shapes=[
            pltpu.VMEM((N_META, D_MODEL), BF16),
            pltpu.VMEM((2 * n_slabs, N_META, PREP_SLAB), F32),
        ],
        compiler_params=pltpu.CompilerParams(
            vmem_limit_bytes=VMEM_LIMIT, dimension_semantics=("arbitrary",)),
        name="weight_prep_meta_state",
    )(meta_tokens, nw_mix, w_in_t, w_in_t, w_in_t, w_out, w_gate_up, bg, tri_meta)

    n_tiles = seq // MIX_TILE
    n_pairs = n_tiles // 2
    assert n_tiles % 2 == 0

    def tile_after_pair(b, p):
        nxt = jnp.minimum(2 * (b * n_pairs + p) + 2, batch * n_tiles - 1)
        return nxt // n_tiles, nxt % n_tiles, 0

    n_steps = batch * n_pairs
    assert D_MODEL % n_steps == 0 and D_FF % n_steps == 0
    up_slab, down_slab = D_MODEL // n_steps, D_FF // n_steps

    h1, wup, wdown = pl.pallas_call(
        _mixer_kernel,
        grid=(batch, n_pairs),
        in_specs=[
            pl.BlockSpec((1, 2 * MIX_TILE, D_MODEL), lambda b, p: (b, p, 0)),
            pl.BlockSpec((1, MIX_TILE, D_MODEL), tile_after_pair),
            _const_spec((1, D_MODEL)),
            _const_spec(wa.shape),
            _const_spec(wb.shape),
            _const_spec(w_gr6.shape),
            _const_spec(wgu6.shape),
            _const_spec((1, QK)),
            _const_spec((1, DV)),
            _const_spec(cw.shape),
            _const_spec(wout.shape),
            _const_spec(tri_wide.shape),
            _const_spec(cmat2.shape),
            _const_spec(lvl.shape),
            _const_spec((DV, QK)),
            _const_spec((SUBLANES, CONV_CH)),
            pl.BlockSpec((up_slab, D_FF), lambda b, p: (b * n_pairs + p, 0)),
            pl.BlockSpec((down_slab, D_MODEL), lambda b, p: (b * n_pairs + p, 0)),
        ],
        out_specs=(
            pl.BlockSpec((1, 2 * MIX_TILE, D_MODEL), lambda b, p: (b, p, 0)),
            pl.BlockSpec((up_slab, D_FF), lambda b, p: (b * n_pairs + p, 0)),
            pl.BlockSpec((down_slab, D_MODEL), lambda b, p: (b * n_pairs + p, 0)),
        ),
        out_shape=(
            jax.ShapeDtypeStruct((batch, seq, D_MODEL), F32),
            jax.ShapeDtypeStruct((D_MODEL, D_FF), BF16),
            jax.ShapeDtypeStruct((D_FF, D_MODEL), BF16),
        ),
        scratch_shapes=[
            pltpu.VMEM((DV, QK), F32),
            pltpu.VMEM((DV, QK), F32),
            pltpu.VMEM((SUBLANES, CONV_CH), F32),
            pltpu.VMEM((MIX_TILE, D_MODEL), BF16),
            pltpu.VMEM((MIX_TILE, D_MODEL), BF16),
            pltpu.VMEM((MIX_TILE, LANES), F32),
            pltpu.VMEM((MIX_TILE, LANES), F32),
            pltpu.VMEM((MIX_TILE, QK), F32),
            pltpu.VMEM((MIX_TILE, QK), F32),
            pltpu.VMEM((MIX_TILE, GW), F32),
            pltpu.VMEM((MIX_TILE, QK), F32),
            pltpu.VMEM((MIX_TILE, QK), F32),
            pltpu.VMEM((MIX_TILE, QK), F32),
            pltpu.VMEM((MIX_TILE, GW), F32),
            pltpu.VMEM((MIX_TILE, GW), F32),
            pltpu.VMEM((MIX_TILE, CONV_CH), BF16),
        ],
        compiler_params=pltpu.CompilerParams(
            vmem_limit_bytes=VMEM_LIMIT, dimension_semantics=("arbitrary", "arbitrary")),
        name="gla_conv_mixer",
    )(x, x, nw_mix, wa, wb, w_gr6, wgu6, bg, gnw, cw, wout, tri_wide, cmat2, lvl, st0, tail0,
      w_up[0], w_down[0])

    rows = batch * seq
    out = pl.pallas_call(
        _mlp_kernel,
        grid=(rows // MLP_TILE,),
        in_specs=[
            pl.BlockSpec((MLP_TILE, D_MODEL), lambda i: (i, 0)),
            _const_spec((1, D_MODEL)),
            _const_spec(wup.shape),
            _const_spec(wdown.shape),
            _const_spec((1, D_MODEL)),
        ],
        out_specs=pl.BlockSpec((MLP_TILE, D_MODEL), lambda i: (i, 0)),
        out_shape=jax.ShapeDtypeStruct((rows, D_MODEL), F32),
        compiler_params=pltpu.CompilerParams(
            vmem_limit_bytes=VMEM_LIMIT, dimension_semantics=("arbitrary",)),
        name="relu2_mlp_final_norm",
    )(h1.reshape(rows, D_MODEL), nw_mlp, wup, wdown, fw)
    return out.reshape(batch, seq, D_MODEL)
```

```python
import numpy as np
import jax
import jax.numpy as jnp
from jax import lax
from jax.experimental import pallas as pl
from jax.experimental.pallas import tpu as pltpu

D_MODEL = 1024
N_META = 16
HEADS = 4
DK = 64
DV = 128
QK = HEADS * DK
GW = HEADS * DV
RANK = 16
GATE_NORM = 16.0
CONV_CH = 512
D_FF = 4096
GLA_ROWS = 2 * QK + 2 * GW
PROJ_WIDTH = GLA_ROWS + RANK + 3 * CONV_CH
EPS = 1e-6

LANES = 128
SUBLANES = 8
VMEM_LIMIT = 56 * 1024 * 1024

MIX_TILE = 512
WIDE_CHUNK = 128
CHUNK = 64
HALF_WIDTHS = (32, 16, 8, 4, 2, 1)
N_LEVELS = len(HALF_WIDTHS)
WIDE_KEY_BOUND = 1e18
GATE_TERMS = 6
PREP_SLAB = 512
MLP_TILE = 1024
MLP_ROWS = 256
FF_CHUNK = 1024

BF16 = jnp.bfloat16
F32 = jnp.float32


def _rms(x, w):
    ms = jnp.mean(x * x, axis=-1, keepdims=True)
    return x * lax.rsqrt(ms + EPS) * w


def _log_sigmoid(z):
    return jnp.minimum(z, 0.0) - jnp.log(1.0 + jnp.exp(-jnp.abs(z)))


def _silu(g):
    half = 0.5 * g
    return half + half * jnp.tanh(half)


def _dot(a, b):
    return jnp.dot(a, b, preferred_element_type=F32)


def _dot_nt(a, b):
    return lax.dot_general(a, b, (((1,), (1,)), ((), ())), preferred_element_type=F32)


def _split3(x):
    hi = x.astype(BF16)
    r1 = x - hi.astype(F32)
    mid = r1.astype(BF16)
    lo = (r1 - mid.astype(F32)).astype(BF16)
    return hi, mid, lo


def _stack_heads(x):
    lane_head = lax.broadcasted_iota(jnp.int32, x.shape, 1) >> (DK.bit_length() - 1)
    return jnp.concatenate([jnp.where(lane_head == h, x, 0.0) for h in range(HEADS)], axis=0)


def _rows_by_head(v):
    return jnp.concatenate([v[:, h * DV:(h + 1) * DV] for h in range(HEADS)], axis=0)


def _gate_log_decay(gr6, wgu6, bg):
    hi, mid, lo = _split3(gr6)
    group = lax.broadcasted_iota(jnp.int32, gr6.shape, 1) >> (RANK.bit_length() - 1)
    lhs = jnp.where(group < 3, hi, jnp.where(group < 5, mid, lo))
    z = _dot(lhs, wgu6) + bg
    return _log_sigmoid(z) / GATE_NORM


def _cumsum_rows(tri2, gk):
    hi, mid, _ = _split3(gk)
    return _dot(tri2, jnp.concatenate([hi, mid], axis=0))


def _state_increment(kdec, v):
    return _dot(_rows_by_head(v).T.astype(BF16), _stack_heads(kdec).astype(BF16))


def _state_update(st, kdec, v, decay_last):
    return st * decay_last + _state_increment(kdec, v)


def _weight_slabs(win_hbm, wout_hbm, wa_ref, wb_ref, wout_ref):
    conv_row0 = GLA_ROWS + RANK
    slabs = []
    for r in range(0, GLA_ROWS, PREP_SLAB):
        slabs.append((win_hbm.at[pl.ds(r, PREP_SLAB), :], wa_ref.at[pl.ds(r, PREP_SLAB), :]))
    for r in range(0, 3 * CONV_CH, PREP_SLAB):
        slabs.append((win_hbm.at[pl.ds(conv_row0 + r, PREP_SLAB), :], wb_ref.at[pl.ds(r, PREP_SLAB), :]))
    for r in range(0, D_MODEL, PREP_SLAB):
        slabs.append((wout_hbm.at[pl.ds(r, PREP_SLAB), :], wout_ref.at[pl.ds(r, PREP_SLAB), :]))
    return slabs


def _stage_weights(meta_ref, nw_ref, win_hbm, wg_ref, wout_hbm, wgu32_ref, bg_ref, trim_ref,
                   wa_ref, wb_ref, wgr_ref, wgu_ref, wout_ref, st0_ref, tail0_ref, stage_s, sems):
    slabs = _weight_slabs(win_hbm, wout_hbm, wa_ref, wb_ref, wout_ref)
    n_slots = stage_s.shape[0]
    copies = [pltpu.make_async_copy(src, stage_s.at[i % n_slots], sems.at[i % n_slots])
              for i, (src, _) in enumerate(slabs)]
    for copy in copies[:n_slots]:
        copy.start()
    hn = _rms(meta_ref[...], nw_ref[...]).astype(BF16)
    g_hi, g_mid, g_lo = _split3(wgu32_ref[0])
    zeros = jnp.zeros((LANES - GATE_TERMS * RANK, QK), BF16)
    wgu_ref[...] = jnp.concatenate([g_hi, g_mid, g_lo, g_hi, g_mid, g_hi, zeros], axis=0)
    gr = wg_ref[...].astype(BF16)
    wgr_ref[...] = jnp.concatenate(
        [gr] * GATE_TERMS + [jnp.zeros((LANES - GATE_TERMS * RANK, D_MODEL), BF16)], axis=0)
    n_proj = (GLA_ROWS + 3 * CONV_CH) // PREP_SLAB
    proj = []
    for i, (_, dst) in enumerate(slabs):
        copies[i].wait()
        dst[...] = stage_s[i % n_slots].astype(BF16)
        if i + n_slots < len(copies):
            copies[i + n_slots].start()
        if i < n_proj:
            proj.append(_dot_nt(hn, dst[...]))
    proj = jnp.concatenate(proj, axis=1)
    k = proj[:, QK:2 * QK]
    v = proj[:, 2 * QK:2 * QK + GW]
    cc = proj[:, GLA_ROWS + CONV_CH:GLA_ROWS + 2 * CONV_CH]
    cx = proj[:, GLA_ROWS + 2 * CONV_CH:GLA_ROWS + 3 * CONV_CH]
    gk = _gate_log_decay(_dot_nt(hn, wgr_ref[...]), wgu_ref[...], bg_ref[...])
    b = _cumsum_rows(trim_ref[...], gk)
    b_last = b[N_META - 1:N_META, :]
    st0_ref[...] = _state_update(jnp.zeros((DV, QK), F32), k * jnp.exp(b_last - b), v, jnp.exp(b_last))
    u = cc * cx
    tail0_ref[...] = u[N_META - SUBLANES:N_META, :]


def _wide_chunks():
    return [slice(c * WIDE_CHUNK, (c + 1) * WIDE_CHUNK) for c in range(MIX_TILE // WIDE_CHUNK)]


def _wide_states(v_s, decay_last, kp_s, st_ref):
    chunks = _wide_chunks()
    kv = [_state_increment(kp_s[rows, :] * decay_last[c], v_s[rows, :]) for c, rows in enumerate(chunks)]
    states = [st_ref[...]]
    for c in range(len(chunks)):
        states.append(states[c] * decay_last[c] + kv[c])
    st_ref[...] = states[-1]
    return states[:-1]


def _wide_scores(qp_s, kp_s, states):
    return [_dot_nt(_stack_heads(qp_s[rows, :]).astype(BF16),
                    jnp.concatenate([kp_s[rows, :].astype(BF16), states[c].astype(BF16)], axis=0))
            for c, rows in enumerate(_wide_chunks())]


def _wide_outputs(sc, v_s, o_s):
    row = lax.broadcasted_iota(jnp.int32, (HEADS * WIDE_CHUNK, WIDE_CHUNK), 0) & (WIDE_CHUNK - 1)
    col = lax.broadcasted_iota(jnp.int32, (HEADS * WIDE_CHUNK, WIDE_CHUNK), 1)
    causal = row >= col
    for c, rows in enumerate(_wide_chunks()):
        scores = jnp.where(causal, sc[c][:, 0:WIDE_CHUNK], 0.0).astype(BF16)
        for h in range(HEADS):
            hr = slice(h * WIDE_CHUNK, (h + 1) * WIDE_CHUNK)
            o_h = _dot(scores[hr], v_s[rows, h * DV:(h + 1) * DV].astype(BF16))
            o_s[rows, h * DV:(h + 1) * DV] = o_h + sc[c][hr, WIDE_CHUNK:WIDE_CHUNK + DV]


def _gla_stable(q_s, k_s, v_s, gk_s, o_s, st_ref, cmat_ref, lvl_ref):
    lvl = lvl_ref[...]

    def chunk_body(c, st):
        r0 = pl.multiple_of(c * CHUNK, CHUNK)
        q = q_s[pl.ds(r0, CHUNK), :]
        k = k_s[pl.ds(r0, CHUNK), :]
        v = v_s[pl.ds(r0, CHUNK), :]
        gk = gk_s[pl.ds(r0, CHUNK), :]
        br = _cumsum_rows(cmat_ref[...], gk)
        b = br[0:CHUNK]
        row = lax.broadcasted_iota(jnp.int32, (CHUNK, QK), 0)
        a = jnp.zeros((HEADS * CHUNK, CHUNK), F32)
        for l, w in enumerate(HALF_WIDTHS):
            ref = br[(l + 1) * CHUNK:(l + 2) * CHUNK]
            e = jnp.exp(-jnp.abs(b - ref))
            second = (row & w) != 0
            ql = jnp.where(second, q * e, 0.0)
            kl = jnp.where(second, 0.0, k * e)
            g = _dot_nt(_stack_heads(ql).astype(BF16), kl.astype(BF16))
            a = jnp.where(lvl == l, g, a)
        g = _dot_nt(_stack_heads(q).astype(BF16), k.astype(BF16))
        a = jnp.where(lvl == N_LEVELS, g, a)
        a = a.astype(BF16)
        o_inter = _dot_nt(_stack_heads(q * jnp.exp(b)).astype(BF16), st.astype(BF16))
        for h in range(HEADS):
            o_h = _dot(a[h * CHUNK:(h + 1) * CHUNK], v[:, h * DV:(h + 1) * DV].astype(BF16))
            o_s[pl.ds(r0, CHUNK), h * DV:(h + 1) * DV] = o_h + o_inter[h * CHUNK:(h + 1) * CHUNK]
        b_last = b[CHUNK - 1:CHUNK, :]
        return _state_update(st, k * jnp.exp(b_last - b), v, jnp.exp(b_last))

    st_ref[...] = lax.fori_loop(0, MIX_TILE // CHUNK, chunk_body, st_ref[...])


def _mixer_kernel(x_ref, xnext_ref, meta_ref, nw_ref, win_hbm, wg_ref, wout_hbm, wgu32_ref, bg_ref, gnw_ref,
                  cw_ref, tri_ref, trim_ref, cmat_ref, lvl_ref, wup32_ref, wdown32_ref,
                  h1_ref, wup16_ref, wdown16_ref,
                  wa_ref, wb_ref, wgr_ref, wgu_ref, wout_ref, st0_ref, tail0_ref, stage_s, sems,
                  st_ref, stprev_s, tail_ref, hn_s, hnext_s, grnext_s, q_s, k_s, v_s, gk_s, qp_s, kp_s, o_s,
                  gate_s, yconv_s):
    t = pl.program_id(1)

    @pl.when(pl.program_id(0) + t == 0)
    def _():
        _stage_weights(meta_ref, nw_ref, win_hbm, wg_ref, wout_hbm, wgu32_ref, bg_ref, trim_ref,
                       wa_ref, wb_ref, wgr_ref, wgu_ref, wout_ref, st0_ref, tail0_ref, stage_s, sems)
        hnext_s[...] = _rms(x_ref[0], nw_ref[...]).astype(BF16)
        grnext_s[...] = _dot_nt(hnext_s[...], wgr_ref[...])

    @pl.when(t == 0)
    def _():
        st_ref[...] = st0_ref[...]
        tail_ref[...] = tail0_ref[...]

    wup16_ref[...] = wup32_ref[...].astype(BF16)
    wdown16_ref[...] = wdown32_ref[...].astype(BF16)

    hn_s[...] = hnext_s[...]
    gk = _gate_log_decay(grnext_s[...], wgu_ref[...], bg_ref[...])
    qk = _dot_nt(hn_s[...], wa_ref[0:2 * QK, :])
    q = qk[:, 0:QK] * (DK ** -0.5)
    k = qk[:, QK:2 * QK]
    n_wide = MIX_TILE // WIDE_CHUNK
    gk_cols = jnp.concatenate([gk[c * WIDE_CHUNK:(c + 1) * WIDE_CHUNK] for c in range(n_wide)], axis=1)
    b_cols = _cumsum_rows(tri_ref[...], gk_cols)
    b = jnp.concatenate([b_cols[:, c * QK:(c + 1) * QK] for c in range(n_wide)], axis=0)
    kp = k * jnp.exp(-b)
    decay_last = [jnp.exp(b[rows.stop - 1:rows.stop, :]) for rows in _wide_chunks()]
    qp_s[...] = q * jnp.exp(b)
    kp_s[...] = kp
    n_bad = jnp.sum(jnp.where(jnp.abs(kp) <= WIDE_KEY_BOUND, 0.0, 1.0))
    wide_ok = n_bad == 0.0
    stprev_s[...] = st_ref[...]
    v_s[...] = _dot_nt(hn_s[...], wa_ref[2 * QK:2 * QK + GW, :])
    cc = _dot_nt(hn_s[...], wb_ref[CONV_CH:2 * CONV_CH, :])
    states = _wide_states(v_s, decay_last, kp_s, st_ref)
    cx = _dot_nt(hn_s[...], wb_ref[2 * CONV_CH:3 * CONV_CH, :])
    sc = _wide_scores(qp_s, kp_s, states)
    hnext = _rms(xnext_ref[0], nw_ref[...]).astype(BF16)
    hnext_s[...] = hnext
    grnext_s[...] = _dot_nt(hnext, wgr_ref[...])
    cb = _dot_nt(hn_s[...], wb_ref[0:CONV_CH, :])
    _wide_outputs(sc, v_s, o_s)
    gate_s[...] = _silu(_dot_nt(hn_s[...], wa_ref[2 * QK + GW:2 * QK + 2 * GW, :]))

    u = cc * cx
    first_row = lax.broadcasted_iota(jnp.int32, u.shape, 0) == 0
    u1 = jnp.where(first_row, tail_ref[SUBLANES - 1:SUBLANES, :], pltpu.roll(u, 1, axis=0))
    u2 = jnp.where(first_row, tail_ref[SUBLANES - 2:SUBLANES - 1, :], pltpu.roll(u1, 1, axis=0))
    tail_ref[...] = u[MIX_TILE - SUBLANES:MIX_TILE, :]
    yconv_s[...] = (cb * (cw_ref[0:1, :] * u2 + cw_ref[1:2, :] * u1 + cw_ref[2:3, :] * u)).astype(BF16)

    def mix_out():
        gnw = gnw_ref[...]
        y_gla = jnp.concatenate(
            [_rms(o_s[:, h * DV:(h + 1) * DV], gnw) for h in range(HEADS)], axis=1) * gate_s[...]
        mixed = (_dot(yconv_s[...], wout_ref[GW:GW + CONV_CH, :])
                 + _dot(y_gla.astype(BF16), wout_ref[0:GW, :]))
        h1_ref[0] = x_ref[0] + mixed

    mix_out()

    @pl.when(jnp.logical_not(wide_ok))
    def _():
        st_ref[...] = stprev_s[...]
        qk_again = _dot_nt(hn_s[...], wa_ref[0:2 * QK, :])
        q_s[...] = qk_again[:, 0:QK] * (DK ** -0.5)
        k_s[...] = qk_again[:, QK:2 * QK]
        gk_s[...] = _gate_log_decay(_dot_nt(hn_s[...], wgr_ref[...]), wgu_ref[...], bg_ref[...])
        _gla_stable(q_s, k_s, v_s, gk_s, o_s, st_ref, cmat_ref, lvl_ref)
        mix_out()


def _mlp_kernel(h_ref, nw_ref, wup_ref, wdown_ref, fw_ref, out_ref):
    blocks = [slice(r, r + MLP_ROWS) for r in range(0, MLP_TILE, MLP_ROWS)]
    hn = [_rms(h_ref[rows, :], nw_ref[...]).astype(BF16) for rows in blocks]
    acc = [jnp.zeros((MLP_ROWS, D_MODEL), F32) for _ in blocks]
    for f in range(0, D_FF, FF_CHUNK):
        for i in range(len(blocks)):
            up = _dot(hn[i], wup_ref[:, f:f + FF_CHUNK])
            act = jnp.square(jnp.maximum(up, 0.0)).astype(BF16)
            acc[i] = acc[i] + _dot(act, wdown_ref[f:f + FF_CHUNK, :])
    for i, rows in enumerate(blocks):
        out_ref[rows, :] = _rms(h_ref[rows, :] + acc[i], fw_ref[...])


def _tri2(n):
    tri = np.tril(np.ones((n, n), np.float32))
    return np.concatenate([tri, tri], axis=1)


def _cumsum_and_ref_matrix():
    tri = np.tril(np.ones((CHUNK, CHUNK), np.float32))
    blocks = [tri]
    idx = np.arange(CHUNK)
    for w in HALF_WIDTHS:
        ref_row = (idx // (2 * w)) * (2 * w) + w - 1
        blocks.append(tri[ref_row])
    return np.concatenate(blocks, axis=0)


def _level_matrix():
    i = np.arange(CHUNK)[:, None]
    j = np.arange(CHUNK)[None, :]
    lvl = np.full((CHUNK, CHUNK), -1, np.int32)
    for l, w in enumerate(HALF_WIDTHS):
        same = (i // (2 * w)) == (j // (2 * w))
        lvl[same & ((i & w) != 0) & ((j & w) == 0)] = l
    lvl[i == j] = N_LEVELS
    return np.tile(lvl, (HEADS, 1))


def _const_spec(shape):
    return pl.BlockSpec(shape, lambda *_: (0,) * len(shape))


def kernel(x, meta_tokens, norm_mix_w, w_in, w_gate_up, b_gate, gla_norm_w, conv_w, w_out,
           norm_mlp_w, w_up, w_down, norm_final_w):
    batch, seq, _ = x.shape
    assert seq % MIX_TILE == 0 and (batch * seq) % MLP_TILE == 0
    assert norm_mix_w.shape[0] == 1, "single layer"

    bg = b_gate[0].reshape(1, QK)
    nw_mix = norm_mix_w[0].reshape(1, D_MODEL)
    gnw = gla_norm_w[0].reshape(1, DV)
    cw = conv_w[0]
    nw_mlp = norm_mlp_w[0].reshape(1, D_MODEL)
    fw = norm_final_w.reshape(1, D_MODEL)

    cmat = _cumsum_and_ref_matrix()
    cmat2 = jnp.asarray(np.concatenate([cmat, cmat], axis=1), BF16)
    tri_meta = jnp.asarray(_tri2(N_META), BF16)
    tri_wide = jnp.asarray(_tri2(WIDE_CHUNK), BF16)
    lvl = jnp.asarray(_level_matrix())

    w_in_t = jnp.swapaxes(w_in, 1, 2)[0]
    assert GLA_ROWS % PREP_SLAB == 0 and (3 * CONV_CH) % PREP_SLAB == 0 and D_MODEL % PREP_SLAB == 0

    n_tiles = seq // MIX_TILE

    def next_tile(b, t):
        nxt = jnp.minimum(b * n_tiles + t + 1, batch * n_tiles - 1)
        return nxt // n_tiles, nxt % n_tiles, 0

    n_steps = batch * n_tiles
    assert D_MODEL % n_steps == 0 and D_FF % n_steps == 0
    up_slab, down_slab = D_MODEL // n_steps, D_FF // n_steps

    h1, wup, wdown = pl.pallas_call(
        _mixer_kernel,
        grid=(batch, n_tiles),
        in_specs=[
            pl.BlockSpec((1, MIX_TILE, D_MODEL), lambda b, t: (b, t, 0)),
            pl.BlockSpec((1, MIX_TILE, D_MODEL), next_tile),
            _const_spec((N_META, D_MODEL)),
            _const_spec((1, D_MODEL)),
            pl.BlockSpec(memory_space=pl.ANY),
            pl.BlockSpec((pl.Element(RANK), pl.Element(D_MODEL)), lambda b, t: (GLA_ROWS, 0)),
            pl.BlockSpec(memory_space=pl.ANY),
            _const_spec((1, RANK, QK)),
            _const_spec((1, QK)),
            _const_spec((1, DV)),
            _const_spec(cw.shape),
            _const_spec(tri_wide.shape),
            _const_spec(tri_meta.shape),
            _const_spec(cmat2.shape),
            _const_spec(lvl.shape),
            pl.BlockSpec((up_slab, D_FF), lambda b, t: (b * n_tiles + t, 0)),
            pl.BlockSpec((down_slab, D_MODEL), lambda b, t: (b * n_tiles + t, 0)),
        ],
        out_specs=(
            pl.BlockSpec((1, MIX_TILE, D_MODEL), lambda b, t: (b, t, 0)),
            pl.BlockSpec((up_slab, D_FF), lambda b, t: (b * n_tiles + t, 0)),
            pl.BlockSpec((down_slab, D_MODEL), lambda b, t: (b * n_tiles + t, 0)),
        ),
        out_shape=(
            jax.ShapeDtypeStruct((batch, seq, D_MODEL), F32),
            jax.ShapeDtypeStruct((D_MODEL, D_FF), BF16),
            jax.ShapeDtypeStruct((D_FF, D_MODEL), BF16),
        ),
        scratch_shapes=[
            pltpu.VMEM((GLA_ROWS, D_MODEL), BF16),
            pltpu.VMEM((3 * CONV_CH, D_MODEL), BF16),
            pltpu.VMEM((LANES, D_MODEL), BF16),
            pltpu.VMEM((LANES, QK), BF16),
            pltpu.VMEM((D_MODEL, D_MODEL), BF16),
            pltpu.VMEM((DV, QK), F32),
            pltpu.VMEM((SUBLANES, CONV_CH), F32),
            pltpu.VMEM((2, PREP_SLAB, D_MODEL), F32),
            pltpu.SemaphoreType.DMA((2,)),
            pltpu.VMEM((DV, QK), F32),
            pltpu.VMEM((DV, QK), F32),
            pltpu.VMEM((SUBLANES, CONV_CH), F32),
            pltpu.VMEM((MIX_TILE, D_MODEL), BF16),
            pltpu.VMEM((MIX_TILE, D_MODEL), BF16),
            pltpu.VMEM((MIX_TILE, LANES), F32),
            pltpu.VMEM((MIX_TILE, QK), F32),
            pltpu.VMEM((MIX_TILE, QK), F32),
            pltpu.VMEM((MIX_TILE, GW), F32),
            pltpu.VMEM((MIX_TILE, QK), F32),
            pltpu.VMEM((MIX_TILE, QK), F32),
            pltpu.VMEM((MIX_TILE, QK), F32),
            pltpu.VMEM((MIX_TILE, GW), F32),
            pltpu.VMEM((MIX_TILE, GW), F32),
            pltpu.VMEM((MIX_TILE, CONV_CH), BF16),
        ],
        compiler_params=pltpu.CompilerParams(
            vmem_limit_bytes=VMEM_LIMIT, dimension_semantics=("arbitrary", "arbitrary")),
        name="gla_conv_mixer",
    )(x, x, meta_tokens, nw_mix, w_in_t, w_in_t, w_out[0], w_gate_up, bg, gnw, cw, tri_wide, tri_meta, cmat2, lvl,
      w_up[0], w_down[0])

    rows = batch * seq
    out = pl.pallas_call(
        _mlp_kernel,
        grid=(rows // MLP_TILE,),
        in_specs=[
            pl.BlockSpec((MLP_TILE, D_MODEL), lambda i: (i, 0)),
            _const_spec((1, D_MODEL)),
            _const_spec(wup.shape),
            _const_spec(wdown.shape),
            _const_spec((1, D_MODEL)),
        ],
        out_specs=pl.BlockSpec((MLP_TILE, D_MODEL), lambda i: (i, 0)),
        out_shape=jax.ShapeDtypeStruct((rows, D_MODEL), F32),
        compiler_params=pltpu.CompilerParams(
            vmem_limit_bytes=VMEM_LIMIT, dimension_semantics=("arbitrary",)),
        name="relu2_mlp_final_norm",
    )(h1.reshape(rows, D_MODEL), nw_mlp, wup, wdown, fw)
    return out.reshape(batch, seq, D_MODEL)
```

```python
import numpy as np
import jax
import jax.numpy as jnp
from jax import lax
from jax.experimental import pallas as pl
from jax.experimental.pallas import tpu as pltpu

D_MODEL = 1024
N_META = 16
HEADS = 4
DK = 64
DV = 128
QK = HEADS * DK
GW = HEADS * DV
RANK = 16
GATE_NORM = 16.0
CONV_CH = 512
D_FF = 4096
GLA_ROWS = 2 * QK + 2 * GW
PROJ_WIDTH = GLA_ROWS + RANK + 3 * CONV_CH
EPS = 1e-6

LANES = 128
SUBLANES = 8
VMEM_LIMIT = 56 * 1024 * 1024

MIX_TILE = 512
WIDE_CHUNK = 128
CHUNK = 64
HALF_WIDTHS = (32, 16, 8, 4, 2, 1)
N_LEVELS = len(HALF_WIDTHS)
WIDE_KEY_BOUND = 1e18
GATE_TERMS = 6
PREP_SLAB = 512
STAGE_SLOTS = 4
MLP_TILE = 1024
MLP_ROWS = 256
FF_CHUNK = 1024

BF16 = jnp.bfloat16
F32 = jnp.float32


def _rms(x, w):
    ms = jnp.mean(x * x, axis=-1, keepdims=True)
    return x * lax.rsqrt(ms + EPS) * w


def _log_sigmoid(z):
    return jnp.minimum(z, 0.0) - jnp.log(1.0 + jnp.exp(-jnp.abs(z)))


def _silu(g):
    half = 0.5 * g
    return half + half * jnp.tanh(half)


def _dot(a, b):
    return jnp.dot(a, b, preferred_element_type=F32)


def _dot_nt(a, b):
    return lax.dot_general(a, b, (((1,), (1,)), ((), ())), preferred_element_type=F32)


def _split3(x):
    hi = x.astype(BF16)
    r1 = x - hi.astype(F32)
    mid = r1.astype(BF16)
    lo = (r1 - mid.astype(F32)).astype(BF16)
    return hi, mid, lo


def _stack_heads(x):
    lane_head = lax.broadcasted_iota(jnp.int32, x.shape, 1) >> (DK.bit_length() - 1)
    return jnp.concatenate([jnp.where(lane_head == h, x, 0.0) for h in range(HEADS)], axis=0)


def _rows_by_head(v):
    return jnp.concatenate([v[:, h * DV:(h + 1) * DV] for h in range(HEADS)], axis=0)


def _gate_log_decay(gr6, wgu6, bg):
    hi, mid, lo = _split3(gr6)
    group = lax.broadcasted_iota(jnp.int32, gr6.shape, 1) >> (RANK.bit_length() - 1)
    lhs = jnp.where(group < 3, hi, jnp.where(group < 5, mid, lo))
    z = _dot(lhs, wgu6) + bg
    return _log_sigmoid(z) / GATE_NORM


def _cumsum_rows(tri2, gk):
    hi, mid, _ = _split3(gk)
    return _dot(tri2, jnp.concatenate([hi, mid], axis=0))


def _state_increment(kdec, v):
    return _dot(_rows_by_head(v).T.astype(BF16), _stack_heads(kdec).astype(BF16))


def _state_update(st, kdec, v, decay_last):
    return st * decay_last + _state_increment(kdec, v)


def _weight_slabs(win_hbm, wout_hbm, wa_ref, wb_ref, wout_ref):
    conv_row0 = GLA_ROWS + RANK
    slabs = []
    for r in range(0, GLA_ROWS, PREP_SLAB):
        slabs.append((win_hbm.at[pl.ds(r, PREP_SLAB), :], wa_ref.at[pl.ds(r, PREP_SLAB), :]))
    for r in range(0, 3 * CONV_CH, PREP_SLAB):
        slabs.append((win_hbm.at[pl.ds(conv_row0 + r, PREP_SLAB), :], wb_ref.at[pl.ds(r, PREP_SLAB), :]))
    for r in range(0, D_MODEL, PREP_SLAB):
        slabs.append((wout_hbm.at[pl.ds(r, PREP_SLAB), :], wout_ref.at[pl.ds(r, PREP_SLAB), :]))
    return slabs


def _stage_weights(meta_ref, nw_ref, win_hbm, wg_ref, wout_hbm, wgu32_ref, bg_ref, trim_ref,
                   wa_ref, wb_ref, wgr_ref, wgu_ref, wout_ref, st0_ref, tail0_ref, stage_s, sems, first_tile):
    slabs = _weight_slabs(win_hbm, wout_hbm, wa_ref, wb_ref, wout_ref)
    n_slots = stage_s.shape[0]
    copies = [pltpu.make_async_copy(src, stage_s.at[i % n_slots], sems.at[i % n_slots])
              for i, (src, _) in enumerate(slabs)]
    for copy in copies[:n_slots]:
        copy.start()
    hn = _rms(meta_ref[...], nw_ref[...]).astype(BF16)
    g_hi, g_mid, g_lo = _split3(wgu32_ref[0])
    zeros = jnp.zeros((LANES - GATE_TERMS * RANK, QK), BF16)
    wgu_ref[...] = jnp.concatenate([g_hi, g_mid, g_lo, g_hi, g_mid, g_hi, zeros], axis=0)
    gr = wg_ref[...].astype(BF16)
    wgr_ref[...] = jnp.concatenate(
        [gr] * GATE_TERMS + [jnp.zeros((LANES - GATE_TERMS * RANK, D_MODEL), BF16)], axis=0)
    gk = _gate_log_decay(_dot_nt(hn, wgr_ref[...]), wgu_ref[...], bg_ref[...])
    b = _cumsum_rows(trim_ref[...], gk)
    b_last = b[N_META - 1:N_META, :]
    first_tile()
    n_proj = (GLA_ROWS + 3 * CONV_CH) // PREP_SLAB
    proj = []
    for i, (_, dst) in enumerate(slabs):
        copies[i].wait()
        dst[...] = stage_s[i % n_slots].astype(BF16)
        if i + n_slots < len(copies):
            copies[i + n_slots].start()
        if i < n_proj:
            proj.append(_dot_nt(hn, dst[...]))
        if i == n_proj - 1:
            proj = jnp.concatenate(proj, axis=1)
            k = proj[:, QK:2 * QK]
            v = proj[:, 2 * QK:2 * QK + GW]
            cc = proj[:, GLA_ROWS + CONV_CH:GLA_ROWS + 2 * CONV_CH]
            cx = proj[:, GLA_ROWS + 2 * CONV_CH:GLA_ROWS + 3 * CONV_CH]
            st0_ref[...] = _state_update(jnp.zeros((DV, QK), F32), k * jnp.exp(b_last - b), v, jnp.exp(b_last))
            u = cc * cx
            tail0_ref[...] = u[N_META - SUBLANES:N_META, :]


def _wide_chunks():
    return [slice(c * WIDE_CHUNK, (c + 1) * WIDE_CHUNK) for c in range(MIX_TILE // WIDE_CHUNK)]


def _wide_states(v_s, decay_last, kp_s, st_ref):
    chunks = _wide_chunks()
    kv = [_state_increment(kp_s[rows, :] * decay_last[c], v_s[rows, :]) for c, rows in enumerate(chunks)]
    states = [st_ref[...]]
    for c in range(len(chunks)):
        states.append(states[c] * decay_last[c] + kv[c])
    st_ref[...] = states[-1]
    return states[:-1]


def _wide_scores(qp_s, kp_s, states):
    return [_dot_nt(_stack_heads(qp_s[rows, :]).astype(BF16),
                    jnp.concatenate([kp_s[rows, :].astype(BF16), states[c].astype(BF16)], axis=0))
            for c, rows in enumerate(_wide_chunks())]


def _wide_outputs(sc, v_s, o_s):
    row = lax.broadcasted_iota(jnp.int32, (HEADS * WIDE_CHUNK, WIDE_CHUNK), 0) & (WIDE_CHUNK - 1)
    col = lax.broadcasted_iota(jnp.int32, (HEADS * WIDE_CHUNK, WIDE_CHUNK), 1)
    causal = row >= col
    for c, rows in enumerate(_wide_chunks()):
        scores = jnp.where(causal, sc[c][:, 0:WIDE_CHUNK], 0.0).astype(BF16)
        for h in range(HEADS):
            hr = slice(h * WIDE_CHUNK, (h + 1) * WIDE_CHUNK)
            o_h = _dot(scores[hr], v_s[rows, h * DV:(h + 1) * DV].astype(BF16))
            o_s[rows, h * DV:(h + 1) * DV] = o_h + sc[c][hr, WIDE_CHUNK:WIDE_CHUNK + DV]


def _gla_stable(q_s, k_s, v_s, gk_s, o_s, st_ref, cmat_ref, lvl_ref):
    lvl = lvl_ref[...]

    def chunk_body(c, st):
        r0 = pl.multiple_of(c * CHUNK, CHUNK)
        q = q_s[pl.ds(r0, CHUNK), :]
        k = k_s[pl.ds(r0, CHUNK), :]
        v = v_s[pl.ds(r0, CHUNK), :]
        gk = gk_s[pl.ds(r0, CHUNK), :]
        br = _cumsum_rows(cmat_ref[...], gk)
        b = br[0:CHUNK]
        row = lax.broadcasted_iota(jnp.int32, (CHUNK, QK), 0)
        a = jnp.zeros((HEADS * CHUNK, CHUNK), F32)
        for l, w in enumerate(HALF_WIDTHS):
            ref = br[(l + 1) * CHUNK:(l + 2) * CHUNK]
            e = jnp.exp(-jnp.abs(b - ref))
            second = (row & w) != 0
            ql = jnp.where(second, q * e, 0.0)
            kl = jnp.where(second, 0.0, k * e)
            g = _dot_nt(_stack_heads(ql).astype(BF16), kl.astype(BF16))
            a = jnp.where(lvl == l, g, a)
        g = _dot_nt(_stack_heads(q).astype(BF16), k.astype(BF16))
        a = jnp.where(lvl == N_LEVELS, g, a)
        a = a.astype(BF16)
        o_inter = _dot_nt(_stack_heads(q * jnp.exp(b)).astype(BF16), st.astype(BF16))
        for h in range(HEADS):
            o_h = _dot(a[h * CHUNK:(h + 1) * CHUNK], v[:, h * DV:(h + 1) * DV].astype(BF16))
            o_s[pl.ds(r0, CHUNK), h * DV:(h + 1) * DV] = o_h + o_inter[h * CHUNK:(h + 1) * CHUNK]
        b_last = b[CHUNK - 1:CHUNK, :]
        return _state_update(st, k * jnp.exp(b_last - b), v, jnp.exp(b_last))

    st_ref[...] = lax.fori_loop(0, MIX_TILE // CHUNK, chunk_body, st_ref[...])


def _mixer_kernel(x_ref, xnext_ref, meta_ref, nw_ref, win_hbm, wg_ref, wout_hbm, wgu32_ref, bg_ref, gnw_ref,
                  cw_ref, tri_ref, trim_ref, cmat_ref, lvl_ref, wup32_ref, wdown32_ref,
                  h1_ref, wup16_ref, wdown16_ref,
                  wa_ref, wb_ref, wgr_ref, wgu_ref, wout_ref, st0_ref, tail0_ref, stage_s, sems,
                  st_ref, stprev_s, tail_ref, hn_s, hnext_s, grnext_s, q_s, k_s, v_s, gk_s, qp_s, kp_s, o_s,
                  gate_s, yconv_s):
    t = pl.program_id(1)

    @pl.when(pl.program_id(0) + t == 0)
    def _():
        def first_tile():
            hnext_s[...] = _rms(x_ref[0], nw_ref[...]).astype(BF16)
            grnext_s[...] = _dot_nt(hnext_s[...], wgr_ref[...])

        _stage_weights(meta_ref, nw_ref, win_hbm, wg_ref, wout_hbm, wgu32_ref, bg_ref, trim_ref,
                       wa_ref, wb_ref, wgr_ref, wgu_ref, wout_ref, st0_ref, tail0_ref, stage_s, sems, first_tile)

    @pl.when(t == 0)
    def _():
        st_ref[...] = st0_ref[...]
        tail_ref[...] = tail0_ref[...]

    wup16_ref[...] = wup32_ref[...].astype(BF16)
    wdown16_ref[...] = wdown32_ref[...].astype(BF16)

    hn_s[...] = hnext_s[...]
    gk = _gate_log_decay(grnext_s[...], wgu_ref[...], bg_ref[...])
    qk = _dot_nt(hn_s[...], wa_ref[0:2 * QK, :])
    q = qk[:, 0:QK] * (DK ** -0.5)
    k = qk[:, QK:2 * QK]
    n_wide = MIX_TILE // WIDE_CHUNK
    gk_cols = jnp.concatenate([gk[c * WIDE_CHUNK:(c + 1) * WIDE_CHUNK] for c in range(n_wide)], axis=1)
    b_cols = _cumsum_rows(tri_ref[...], gk_cols)
    b = jnp.concatenate([b_cols[:, c * QK:(c + 1) * QK] for c in range(n_wide)], axis=0)
    kp = k * jnp.exp(-b)
    decay_last = [jnp.exp(b[rows.stop - 1:rows.stop, :]) for rows in _wide_chunks()]
    qp_s[...] = q * jnp.exp(b)
    kp_s[...] = kp
    n_bad = jnp.sum(jnp.where(jnp.abs(kp) <= WIDE_KEY_BOUND, 0.0, 1.0))
    wide_ok = n_bad == 0.0
    stprev_s[...] = st_ref[...]
    v_s[...] = _dot_nt(hn_s[...], wa_ref[2 * QK:2 * QK + GW, :])
    cc = _dot_nt(hn_s[...], wb_ref[CONV_CH:2 * CONV_CH, :])
    states = _wide_states(v_s, decay_last, kp_s, st_ref)
    cx = _dot_nt(hn_s[...], wb_ref[2 * CONV_CH:3 * CONV_CH, :])
    sc = _wide_scores(qp_s, kp_s, states)
    hnext = _rms(xnext_ref[0], nw_ref[...]).astype(BF16)
    hnext_s[...] = hnext
    grnext_s[...] = _dot_nt(hnext, wgr_ref[...])
    cb = _dot_nt(hn_s[...], wb_ref[0:CONV_CH, :])
    _wide_outputs(sc, v_s, o_s)
    gate_s[...] = _silu(_dot_nt(hn_s[...], wa_ref[2 * QK + GW:2 * QK + 2 * GW, :]))

    u = cc * cx
    first_row = lax.broadcasted_iota(jnp.int32, u.shape, 0) == 0
    u1 = jnp.where(first_row, tail_ref[SUBLANES - 1:SUBLANES, :], pltpu.roll(u, 1, axis=0))
    u2 = jnp.where(first_row, tail_ref[SUBLANES - 2:SUBLANES - 1, :], pltpu.roll(u1, 1, axis=0))
    tail_ref[...] = u[MIX_TILE - SUBLANES:MIX_TILE, :]
    yconv_s[...] = (cb * (cw_ref[0:1, :] * u2 + cw_ref[1:2, :] * u1 + cw_ref[2:3, :] * u)).astype(BF16)

    def mix_out():
        gnw = gnw_ref[...]
        y_gla = jnp.concatenate(
            [_rms(o_s[:, h * DV:(h + 1) * DV], gnw) for h in range(HEADS)], axis=1) * gate_s[...]
        mixed = (_dot(yconv_s[...], wout_ref[GW:GW + CONV_CH, :])
                 + _dot(y_gla.astype(BF16), wout_ref[0:GW, :]))
        h1_ref[0] = x_ref[0] + mixed

    mix_out()

    @pl.when(jnp.logical_not(wide_ok))
    def _():
        st_ref[...] = stprev_s[...]
        qk_again = _dot_nt(hn_s[...], wa_ref[0:2 * QK, :])
        q_s[...] = qk_again[:, 0:QK] * (DK ** -0.5)
        k_s[...] = qk_again[:, QK:2 * QK]
        gk_s[...] = _gate_log_decay(_dot_nt(hn_s[...], wgr_ref[...]), wgu_ref[...], bg_ref[...])
        _gla_stable(q_s, k_s, v_s, gk_s, o_s, st_ref, cmat_ref, lvl_ref)
        mix_out()


def _mlp_kernel(h_ref, nw_ref, wup_ref, wdown_ref, fw_ref, out_ref):
    blocks = [slice(r, r + MLP_ROWS) for r in range(0, MLP_TILE, MLP_ROWS)]
    hn = [_rms(h_ref[rows, :], nw_ref[...]).astype(BF16) for rows in blocks]
    acc = [jnp.zeros((MLP_ROWS, D_MODEL), F32) for _ in blocks]
    for f in range(0, D_FF, FF_CHUNK):
        for i in range(len(blocks)):
            up = _dot(hn[i], wup_ref[:, f:f + FF_CHUNK])
            act = jnp.square(jnp.maximum(up, 0.0)).astype(BF16)
            acc[i] = acc[i] + _dot(act, wdown_ref[f:f + FF_CHUNK, :])
    for i, rows in enumerate(blocks):
        out_ref[rows, :] = _rms(h_ref[rows, :] + acc[i], fw_ref[...])


def _tri2(n):
    tri = np.tril(np.ones((n, n), np.float32))
    return np.concatenate([tri, tri], axis=1)


def _cumsum_and_ref_matrix():
    tri = np.tril(np.ones((CHUNK, CHUNK), np.float32))
    blocks = [tri]
    idx = np.arange(CHUNK)
    for w in HALF_WIDTHS:
        ref_row = (idx // (2 * w)) * (2 * w) + w - 1
        blocks.append(tri[ref_row])
    return np.concatenate(blocks, axis=0)


def _level_matrix():
    i = np.arange(CHUNK)[:, None]
    j = np.arange(CHUNK)[None, :]
    lvl = np.full((CHUNK, CHUNK), -1, np.int32)
    for l, w in enumerate(HALF_WIDTHS):
        same = (i // (2 * w)) == (j // (2 * w))
        lvl[same & ((i & w) != 0) & ((j & w) == 0)] = l
    lvl[i == j] = N_LEVELS
    return np.tile(lvl, (HEADS, 1))


def _const_spec(shape):
    return pl.BlockSpec(shape, lambda *_: (0,) * len(shape))


def kernel(x, meta_tokens, norm_mix_w, w_in, w_gate_up, b_gate, gla_norm_w, conv_w, w_out,
           norm_mlp_w, w_up, w_down, norm_final_w):
    batch, seq, _ = x.shape
    assert seq % MIX_TILE == 0 and (batch * seq) % MLP_TILE == 0
    assert norm_mix_w.shape[0] == 1, "single layer"

    bg = b_gate[0].reshape(1, QK)
    nw_mix = norm_mix_w[0].reshape(1, D_MODEL)
    gnw = gla_norm_w[0].reshape(1, DV)
    cw = conv_w[0]
    nw_mlp = norm_mlp_w[0].reshape(1, D_MODEL)
    fw = norm_final_w.reshape(1, D_MODEL)

    cmat = _cumsum_and_ref_matrix()
    cmat2 = jnp.asarray(np.concatenate([cmat, cmat], axis=1), BF16)
    tri_meta = jnp.asarray(_tri2(N_META), BF16)
    tri_wide = jnp.asarray(_tri2(WIDE_CHUNK), BF16)
    lvl = jnp.asarray(_level_matrix())

    w_in_t = jnp.swapaxes(w_in, 1, 2)[0]
    assert GLA_ROWS % PREP_SLAB == 0 and (3 * CONV_CH) % PREP_SLAB == 0 and D_MODEL % PREP_SLAB == 0

    n_tiles = seq // MIX_TILE

    def next_tile(b, t):
        nxt = jnp.minimum(b * n_tiles + t + 1, batch * n_tiles - 1)
        return nxt // n_tiles, nxt % n_tiles, 0

    n_steps = batch * n_tiles
    assert D_MODEL % n_steps == 0 and D_FF % n_steps == 0
    up_slab, down_slab = D_MODEL // n_steps, D_FF // n_steps

    h1, wup, wdown = pl.pallas_call(
        _mixer_kernel,
        grid=(batch, n_tiles),
        in_specs=[
            pl.BlockSpec((1, MIX_TILE, D_MODEL), lambda b, t: (b, t, 0)),
            pl.BlockSpec((1, MIX_TILE, D_MODEL), next_tile),
            _const_spec((N_META, D_MODEL)),
            _const_spec((1, D_MODEL)),
            pl.BlockSpec(memory_space=pl.ANY),
            pl.BlockSpec((pl.Element(RANK), pl.Element(D_MODEL)), lambda b, t: (GLA_ROWS, 0)),
            pl.BlockSpec(memory_space=pl.ANY),
            _const_spec((1, RANK, QK)),
            _const_spec((1, QK)),
            _const_spec((1, DV)),
            _const_spec(cw.shape),
            _const_spec(tri_wide.shape),
            _const_spec(tri_meta.shape),
            _const_spec(cmat2.shape),
            _const_spec(lvl.shape),
            pl.BlockSpec((up_slab, D_FF), lambda b, t: (b * n_tiles + t, 0)),
            pl.BlockSpec((down_slab, D_MODEL), lambda b, t: (b * n_tiles + t, 0)),
        ],
        out_specs=(
            pl.BlockSpec((1, MIX_TILE, D_MODEL), lambda b, t: (b, t, 0)),
            pl.BlockSpec((up_slab, D_FF), lambda b, t: (b * n_tiles + t, 0)),
            pl.BlockSpec((down_slab, D_MODEL), lambda b, t: (b * n_tiles + t, 0)),
        ),
        out_shape=(
            jax.ShapeDtypeStruct((batch, seq, D_MODEL), F32),
            jax.ShapeDtypeStruct((D_MODEL, D_FF), BF16),
            jax.ShapeDtypeStruct((D_FF, D_MODEL), BF16),
        ),
        scratch_shapes=[
            pltpu.VMEM((GLA_ROWS, D_MODEL), BF16),
            pltpu.VMEM((3 * CONV_CH, D_MODEL), BF16),
            pltpu.VMEM((LANES, D_MODEL), BF16),
            pltpu.VMEM((LANES, QK), BF16),
            pltpu.VMEM((D_MODEL, D_MODEL), BF16),
            pltpu.VMEM((DV, QK), F32),
            pltpu.VMEM((SUBLANES, CONV_CH), F32),
            pltpu.VMEM((STAGE_SLOTS, PREP_SLAB, D_MODEL), F32),
            pltpu.SemaphoreType.DMA((STAGE_SLOTS,)),
            pltpu.VMEM((DV, QK), F32),
            pltpu.VMEM((DV, QK), F32),
            pltpu.VMEM((SUBLANES, CONV_CH), F32),
            pltpu.VMEM((MIX_TILE, D_MODEL), BF16),
            pltpu.VMEM((MIX_TILE, D_MODEL), BF16),
            pltpu.VMEM((MIX_TILE, LANES), F32),
            pltpu.VMEM((MIX_TILE, QK), F32),
            pltpu.VMEM((MIX_TILE, QK), F32),
            pltpu.VMEM((MIX_TILE, GW), F32),
            pltpu.VMEM((MIX_TILE, QK), F32),
            pltpu.VMEM((MIX_TILE, QK), F32),
            pltpu.VMEM((MIX_TILE, QK), F32),
            pltpu.VMEM((MIX_TILE, GW), F32),
            pltpu.VMEM((MIX_TILE, GW), F32),
            pltpu.VMEM((MIX_TILE, CONV_CH), BF16),
        ],
        compiler_params=pltpu.CompilerParams(
            vmem_limit_bytes=VMEM_LIMIT, dimension_semantics=("arbitrary", "arbitrary")),
        name="gla_conv_mixer",
    )(x, x, meta_tokens, nw_mix, w_in_t, w_in_t, w_out[0], w_gate_up, bg, gnw, cw, tri_wide, tri_meta, cmat2, lvl,
      w_up[0], w_down[0])

    rows = batch * seq
    out = pl.pallas_call(
        _mlp_kernel,
        grid=(rows // MLP_TILE,),
        in_specs=[
            pl.BlockSpec((MLP_TILE, D_MODEL), lambda i: (i, 0)),
            _const_spec((1, D_MODEL)),
            _const_spec(wup.shape),
            _const_spec(wdown.shape),
            _const_spec((1, D_MODEL)),
        ],
        out_specs=pl.BlockSpec((MLP_TILE, D_MODEL), lambda i: (i, 0)),
        out_shape=jax.ShapeDtypeStruct((rows, D_MODEL), F32),
        compiler_params=pltpu.CompilerParams(
            vmem_limit_bytes=VMEM_LIMIT, dimension_semantics=("arbitrary",)),
        name="relu2_mlp_final_norm",
    )(h1.reshape(rows, D_MODEL), nw_mlp, wup, wdown, fw)
    return out.reshape(batch, seq, D_MODEL)
```

```python
import numpy as np
import jax
import jax.numpy as jnp
from jax import lax
from jax.experimental import pallas as pl
from jax.experimental.pallas import tpu as pltpu

D_MODEL = 1024
N_META = 16
HEADS = 4
DK = 64
DV = 128
QK = HEADS * DK
GW = HEADS * DV
RANK = 16
GATE_NORM = 16.0
CONV_CH = 512
CONV_TAPS = 3
D_FF = 4096
GLA_ROWS = 2 * QK + 2 * GW
PROJ_WIDTH = GLA_ROWS + RANK + 3 * CONV_CH
EPS = 1e-6

LANES = 128
SUBLANES = 8
VMEM_LIMIT = 56 * 1024 * 1024

MIX_TILE = 512
WIDE_CHUNK = 128
CHUNK = 64
HALF_WIDTHS = (32, 16, 8, 4, 2, 1)
N_LEVELS = len(HALF_WIDTHS)
WIDE_KEY_BOUND = 1e18
GATE_TERMS = 6
PREP_SLAB = 512
STAGE_SLOTS = 4
MLP_TILE = 1024
MLP_ROWS = 256
FF_CHUNK = 1024

BF16 = jnp.bfloat16
F32 = jnp.float32


def _rms(x, w):
    ms = jnp.mean(x * x, axis=-1, keepdims=True)
    return x * lax.rsqrt(ms + EPS) * w


def _log_sigmoid(z):
    return jnp.minimum(z, 0.0) - jnp.log(1.0 + jnp.exp(-jnp.abs(z)))


def _silu(g):
    half = 0.5 * g
    return half + half * jnp.tanh(half)


def _dot(a, b):
    return jnp.dot(a, b, preferred_element_type=F32)


def _dot_nt(a, b):
    return lax.dot_general(a, b, (((1,), (1,)), ((), ())), preferred_element_type=F32)


def _split3(x):
    hi = x.astype(BF16)
    r1 = x - hi.astype(F32)
    mid = r1.astype(BF16)
    lo = (r1 - mid.astype(F32)).astype(BF16)
    return hi, mid, lo


def _stack_heads(x):
    lane_head = lax.broadcasted_iota(jnp.int32, x.shape, 1) >> (DK.bit_length() - 1)
    return jnp.concatenate([jnp.where(lane_head == h, x, 0.0) for h in range(HEADS)], axis=0)


def _rows_by_head(v):
    return jnp.concatenate([v[:, h * DV:(h + 1) * DV] for h in range(HEADS)], axis=0)


def _gate_log_decay(gr6, wgu6, bg):
    hi, mid, lo = _split3(gr6)
    group = lax.broadcasted_iota(jnp.int32, gr6.shape, 1) >> (RANK.bit_length() - 1)
    lhs = jnp.where(group < 3, hi, jnp.where(group < 5, mid, lo))
    z = _dot(lhs, wgu6) + bg
    return _log_sigmoid(z) / GATE_NORM


def _cumsum_rows(tri2, gk):
    hi, mid, _ = _split3(gk)
    return _dot(tri2, jnp.concatenate([hi, mid], axis=0))


def _state_increment(kdec, v):
    return _dot(_rows_by_head(v).T.astype(BF16), _stack_heads(kdec).astype(BF16))


def _state_update(st, kdec, v, decay_last):
    return st * decay_last + _state_increment(kdec, v)


def _weight_slabs(win_hbm, wout_hbm, wa_ref, wb_ref, wout_ref):
    conv_row0 = GLA_ROWS + RANK
    slabs = []
    for r in range(0, GLA_ROWS, PREP_SLAB):
        slabs.append((win_hbm.at[pl.ds(r, PREP_SLAB), :], wa_ref.at[pl.ds(r, PREP_SLAB), :]))
    for r in range(0, 3 * CONV_CH, PREP_SLAB):
        slabs.append((win_hbm.at[pl.ds(conv_row0 + r, PREP_SLAB), :], wb_ref.at[pl.ds(r, PREP_SLAB), :]))
    for r in range(0, D_MODEL, PREP_SLAB):
        slabs.append((wout_hbm.at[pl.ds(r, PREP_SLAB), :], wout_ref.at[pl.ds(r, PREP_SLAB), :]))
    return slabs


def _stage_weights(meta_ref, nw_ref, win_hbm, wg_ref, wout_hbm, wgu32_ref, bg_ref, trim_ref,
                   wa_ref, wb_ref, wgr_ref, wgu_ref, wout_ref, st0_ref, tail0_ref, stage_s, sems, first_tile):
    slabs = _weight_slabs(win_hbm, wout_hbm, wa_ref, wb_ref, wout_ref)
    n_slots = stage_s.shape[0]
    copies = [pltpu.make_async_copy(src, stage_s.at[i % n_slots], sems.at[i % n_slots])
              for i, (src, _) in enumerate(slabs)]
    for copy in copies[:n_slots]:
        copy.start()
    hn = _rms(meta_ref[...], nw_ref[...]).astype(BF16)
    g_hi, g_mid, g_lo = _split3(wgu32_ref[0])
    zeros = jnp.zeros((LANES - GATE_TERMS * RANK, QK), BF16)
    wgu_ref[...] = jnp.concatenate([g_hi, g_mid, g_lo, g_hi, g_mid, g_hi, zeros], axis=0)
    gr = wg_ref[...].astype(BF16)
    wgr_ref[...] = jnp.concatenate(
        [gr] * GATE_TERMS + [jnp.zeros((LANES - GATE_TERMS * RANK, D_MODEL), BF16)], axis=0)
    gk = _gate_log_decay(_dot_nt(hn, wgr_ref[...]), wgu_ref[...], bg_ref[...])
    b = _cumsum_rows(trim_ref[...], gk)
    b_last = b[N_META - 1:N_META, :]
    first_tile()
    n_proj = (GLA_ROWS + 3 * CONV_CH) // PREP_SLAB
    proj = []
    for i, (_, dst) in enumerate(slabs):
        copies[i].wait()
        dst[...] = stage_s[i % n_slots].astype(BF16)
        if i + n_slots < len(copies):
            copies[i + n_slots].start()
        if i < n_proj:
            proj.append(_dot_nt(hn, dst[...]))
        if i == n_proj - 1:
            proj = jnp.concatenate(proj, axis=1)
            k = proj[:, QK:2 * QK]
            v = proj[:, 2 * QK:2 * QK + GW]
            cc = proj[:, GLA_ROWS + CONV_CH:GLA_ROWS + 2 * CONV_CH]
            cx = proj[:, GLA_ROWS + 2 * CONV_CH:GLA_ROWS + 3 * CONV_CH]
            st0_ref[...] = _state_update(jnp.zeros((DV, QK), F32), k * jnp.exp(b_last - b), v, jnp.exp(b_last))
            u = cc * cx
            tail0_ref[...] = u[N_META - SUBLANES:N_META, :]


def _wide_chunks():
    return [slice(c * WIDE_CHUNK, (c + 1) * WIDE_CHUNK) for c in range(MIX_TILE // WIDE_CHUNK)]


def _wide_states(v_s, decay_last, kp_s, st_ref):
    chunks = _wide_chunks()
    kv = [_state_increment(kp_s[rows, :] * decay_last[c], v_s[rows, :]) for c, rows in enumerate(chunks)]
    states = [st_ref[...]]
    for c in range(len(chunks)):
        states.append(states[c] * decay_last[c] + kv[c])
    st_ref[...] = states[-1]
    return states[:-1]


def _wide_scores(qp_s, kp_s, states):
    return [_dot_nt(_stack_heads(qp_s[rows, :]).astype(BF16),
                    jnp.concatenate([kp_s[rows, :].astype(BF16), states[c].astype(BF16)], axis=0))
            for c, rows in enumerate(_wide_chunks())]


def _wide_outputs(sc, v_s, o_s):
    row = lax.broadcasted_iota(jnp.int32, (HEADS * WIDE_CHUNK, WIDE_CHUNK), 0) & (WIDE_CHUNK - 1)
    col = lax.broadcasted_iota(jnp.int32, (HEADS * WIDE_CHUNK, WIDE_CHUNK), 1)
    causal = row >= col
    for c, rows in enumerate(_wide_chunks()):
        scores = jnp.where(causal, sc[c][:, 0:WIDE_CHUNK], 0.0).astype(BF16)
        for h in range(HEADS):
            hr = slice(h * WIDE_CHUNK, (h + 1) * WIDE_CHUNK)
            o_h = _dot(scores[hr], v_s[rows, h * DV:(h + 1) * DV].astype(BF16))
            o_s[rows, h * DV:(h + 1) * DV] = o_h + sc[c][hr, WIDE_CHUNK:WIDE_CHUNK + DV]


def _gla_stable(q_s, k_s, v_s, gk_s, o_s, st_ref, cmat_ref, lvl_ref):
    lvl = lvl_ref[...]

    def chunk_body(c, st):
        r0 = pl.multiple_of(c * CHUNK, CHUNK)
        q = q_s[pl.ds(r0, CHUNK), :]
        k = k_s[pl.ds(r0, CHUNK), :]
        v = v_s[pl.ds(r0, CHUNK), :]
        gk = gk_s[pl.ds(r0, CHUNK), :]
        br = _cumsum_rows(cmat_ref[...], gk)
        b = br[0:CHUNK]
        row = lax.broadcasted_iota(jnp.int32, (CHUNK, QK), 0)
        a = jnp.zeros((HEADS * CHUNK, CHUNK), F32)
        for l, w in enumerate(HALF_WIDTHS):
            ref = br[(l + 1) * CHUNK:(l + 2) * CHUNK]
            e = jnp.exp(-jnp.abs(b - ref))
            second = (row & w) != 0
            ql = jnp.where(second, q * e, 0.0)
            kl = jnp.where(second, 0.0, k * e)
            g = _dot_nt(_stack_heads(ql).astype(BF16), kl.astype(BF16))
            a = jnp.where(lvl == l, g, a)
        g = _dot_nt(_stack_heads(q).astype(BF16), k.astype(BF16))
        a = jnp.where(lvl == N_LEVELS, g, a)
        a = a.astype(BF16)
        o_inter = _dot_nt(_stack_heads(q * jnp.exp(b)).astype(BF16), st.astype(BF16))
        for h in range(HEADS):
            o_h = _dot(a[h * CHUNK:(h + 1) * CHUNK], v[:, h * DV:(h + 1) * DV].astype(BF16))
            o_s[pl.ds(r0, CHUNK), h * DV:(h + 1) * DV] = o_h + o_inter[h * CHUNK:(h + 1) * CHUNK]
        b_last = b[CHUNK - 1:CHUNK, :]
        return _state_update(st, k * jnp.exp(b_last - b), v, jnp.exp(b_last))

    st_ref[...] = lax.fori_loop(0, MIX_TILE // CHUNK, chunk_body, st_ref[...])


def _mixer_kernel(x_ref, xnext_ref, meta_ref, nw_ref, win_hbm, wg_ref, wout_hbm, wgu32_ref, bg_ref, gnw_ref,
                  cw_ref, tri_ref, trim_ref, cmat_ref, lvl_ref, wup32_ref, wdown32_ref,
                  h1_ref, wup16_ref, wdown16_ref,
                  wa_ref, wb_ref, wgr_ref, wgu_ref, wout_ref, st0_ref, tail0_ref, stage_s, sems,
                  st_ref, stprev_s, tail_ref, hn_s, hnext_s, grnext_s, q_s, k_s, v_s, gk_s, qp_s, kp_s, o_s,
                  gate_s, yconv_s):
    t = pl.program_id(1)

    @pl.when(pl.program_id(0) + t == 0)
    def _():
        def first_tile():
            hnext_s[...] = _rms(x_ref[0], nw_ref[...]).astype(BF16)
            grnext_s[...] = _dot_nt(hnext_s[...], wgr_ref[...])

        _stage_weights(meta_ref, nw_ref, win_hbm, wg_ref, wout_hbm, wgu32_ref, bg_ref, trim_ref,
                       wa_ref, wb_ref, wgr_ref, wgu_ref, wout_ref, st0_ref, tail0_ref, stage_s, sems, first_tile)

    @pl.when(t == 0)
    def _():
        st_ref[...] = st0_ref[...]
        tail_ref[...] = tail0_ref[...]

    wup16_ref[...] = wup32_ref[...].astype(BF16)
    wdown16_ref[...] = wdown32_ref[...].astype(BF16)

    hn_s[...] = hnext_s[...]
    gk = _gate_log_decay(grnext_s[...], wgu_ref[...], bg_ref[...])
    qk = _dot_nt(hn_s[...], wa_ref[0:2 * QK, :])
    q = qk[:, 0:QK] * (DK ** -0.5)
    k = qk[:, QK:2 * QK]
    n_wide = MIX_TILE // WIDE_CHUNK
    gk_cols = jnp.concatenate([gk[c * WIDE_CHUNK:(c + 1) * WIDE_CHUNK] for c in range(n_wide)], axis=1)
    b_cols = _cumsum_rows(tri_ref[...], gk_cols)
    b = jnp.concatenate([b_cols[:, c * QK:(c + 1) * QK] for c in range(n_wide)], axis=0)
    kp = k * jnp.exp(-b)
    decay_last = [jnp.exp(b[rows.stop - 1:rows.stop, :]) for rows in _wide_chunks()]
    qp_s[...] = q * jnp.exp(b)
    kp_s[...] = kp
    n_bad = jnp.sum(jnp.where(jnp.abs(kp) <= WIDE_KEY_BOUND, 0.0, 1.0))
    wide_ok = n_bad == 0.0
    stprev_s[...] = st_ref[...]
    v_s[...] = _dot_nt(hn_s[...], wa_ref[2 * QK:2 * QK + GW, :])
    cc = _dot_nt(hn_s[...], wb_ref[CONV_CH:2 * CONV_CH, :])
    states = _wide_states(v_s, decay_last, kp_s, st_ref)
    cx = _dot_nt(hn_s[...], wb_ref[2 * CONV_CH:3 * CONV_CH, :])
    sc = _wide_scores(qp_s, kp_s, states)
    hnext = _rms(xnext_ref[0], nw_ref[...]).astype(BF16)
    hnext_s[...] = hnext
    grnext_s[...] = _dot_nt(hnext, wgr_ref[...])
    cb = _dot_nt(hn_s[...], wb_ref[0:CONV_CH, :])
    _wide_outputs(sc, v_s, o_s)
    gate_s[...] = _silu(_dot_nt(hn_s[...], wa_ref[2 * QK + GW:2 * QK + 2 * GW, :]))

    u = cc * cx
    first_row = lax.broadcasted_iota(jnp.int32, u.shape, 0) == 0
    u1 = jnp.where(first_row, tail_ref[SUBLANES - 1:SUBLANES, :], pltpu.roll(u, 1, axis=0))
    u2 = jnp.where(first_row, tail_ref[SUBLANES - 2:SUBLANES - 1, :], pltpu.roll(u1, 1, axis=0))
    tail_ref[...] = u[MIX_TILE - SUBLANES:MIX_TILE, :]
    w0, w1, w2 = (cw_ref[:, tap * CONV_CH:(tap + 1) * CONV_CH] for tap in range(CONV_TAPS))
    yconv_s[...] = (cb * (w0 * u2 + w1 * u1 + w2 * u)).astype(BF16)

    def mix_out():
        gnw = gnw_ref[...]
        y_gla = jnp.concatenate(
            [_rms(o_s[:, h * DV:(h + 1) * DV], gnw) for h in range(HEADS)], axis=1) * gate_s[...]
        mixed = (_dot(yconv_s[...], wout_ref[GW:GW + CONV_CH, :])
                 + _dot(y_gla.astype(BF16), wout_ref[0:GW, :]))
        h1_ref[0] = x_ref[0] + mixed

    mix_out()

    @pl.when(jnp.logical_not(wide_ok))
    def _():
        st_ref[...] = stprev_s[...]
        qk_again = _dot_nt(hn_s[...], wa_ref[0:2 * QK, :])
        q_s[...] = qk_again[:, 0:QK] * (DK ** -0.5)
        k_s[...] = qk_again[:, QK:2 * QK]
        gk_s[...] = _gate_log_decay(_dot_nt(hn_s[...], wgr_ref[...]), wgu_ref[...], bg_ref[...])
        _gla_stable(q_s, k_s, v_s, gk_s, o_s, st_ref, cmat_ref, lvl_ref)
        mix_out()


def _mlp_kernel(h_ref, nw_ref, wup_ref, wdown_ref, fw_ref, out_ref):
    blocks = [slice(r, r + MLP_ROWS) for r in range(0, MLP_TILE, MLP_ROWS)]
    hn = [_rms(h_ref[rows, :], nw_ref[...]).astype(BF16) for rows in blocks]
    acc = [jnp.zeros((MLP_ROWS, D_MODEL), F32) for _ in blocks]
    for f in range(0, D_FF, FF_CHUNK):
        for i in range(len(blocks)):
            up = _dot(hn[i], wup_ref[:, f:f + FF_CHUNK])
            act = jnp.square(jnp.maximum(up, 0.0)).astype(BF16)
            acc[i] = acc[i] + _dot(act, wdown_ref[f:f + FF_CHUNK, :])
    for i, rows in enumerate(blocks):
        out_ref[rows, :] = _rms(h_ref[rows, :] + acc[i], fw_ref[...])


def _tri2(n):
    tri = np.tril(np.ones((n, n), np.float32))
    return np.concatenate([tri, tri], axis=1)


def _cumsum_and_ref_matrix():
    tri = np.tril(np.ones((CHUNK, CHUNK), np.float32))
    blocks = [tri]
    idx = np.arange(CHUNK)
    for w in HALF_WIDTHS:
        ref_row = (idx // (2 * w)) * (2 * w) + w - 1
        blocks.append(tri[ref_row])
    return np.concatenate(blocks, axis=0)


def _level_matrix():
    i = np.arange(CHUNK)[:, None]
    j = np.arange(CHUNK)[None, :]
    lvl = np.full((CHUNK, CHUNK), -1, np.int32)
    for l, w in enumerate(HALF_WIDTHS):
        same = (i // (2 * w)) == (j // (2 * w))
        lvl[same & ((i & w) != 0) & ((j & w) == 0)] = l
    lvl[i == j] = N_LEVELS
    return np.tile(lvl, (HEADS, 1))


def _const_spec(shape):
    return pl.BlockSpec(shape, lambda *_: (0,) * len(shape))


def kernel(x, meta_tokens, norm_mix_w, w_in, w_gate_up, b_gate, gla_norm_w, conv_w, w_out,
           norm_mlp_w, w_up, w_down, norm_final_w):
    batch, seq, _ = x.shape
    assert seq % MIX_TILE == 0 and (batch * seq) % MLP_TILE == 0
    assert norm_mix_w.shape[0] == 1, "single layer"
    assert conv_w.shape == (1, CONV_TAPS, CONV_CH)

    bg = b_gate[0].reshape(1, QK)
    nw_mix = norm_mix_w[0].reshape(1, D_MODEL)
    gnw = gla_norm_w[0].reshape(1, DV)
    cw = conv_w[0].reshape(1, CONV_TAPS * CONV_CH)
    nw_mlp = norm_mlp_w[0].reshape(1, D_MODEL)
    fw = norm_final_w.reshape(1, D_MODEL)

    cmat = _cumsum_and_ref_matrix()
    cmat2 = jnp.asarray(np.concatenate([cmat, cmat], axis=1), BF16)
    tri_meta = jnp.asarray(_tri2(N_META), BF16)
    tri_wide = jnp.asarray(_tri2(WIDE_CHUNK), BF16)
    lvl = jnp.asarray(_level_matrix())

    w_in_t = jnp.swapaxes(w_in, 1, 2)[0]
    assert GLA_ROWS % PREP_SLAB == 0 and (3 * CONV_CH) % PREP_SLAB == 0 and D_MODEL % PREP_SLAB == 0

    n_tiles = seq // MIX_TILE

    def next_tile(b, t):
        nxt = jnp.minimum(b * n_tiles + t + 1, batch * n_tiles - 1)
        return nxt // n_tiles, nxt % n_tiles, 0

    n_steps = batch * n_tiles
    assert D_MODEL % n_steps == 0 and D_FF % n_steps == 0
    up_slab, down_slab = D_MODEL // n_steps, D_FF // n_steps

    h1, wup, wdown = pl.pallas_call(
        _mixer_kernel,
        grid=(batch, n_tiles),
        in_specs=[
            pl.BlockSpec((1, MIX_TILE, D_MODEL), lambda b, t: (b, t, 0)),
            pl.BlockSpec((1, MIX_TILE, D_MODEL), next_tile),
            _const_spec((N_META, D_MODEL)),
            _const_spec((1, D_MODEL)),
            pl.BlockSpec(memory_space=pl.ANY),
            pl.BlockSpec((pl.Element(RANK), pl.Element(D_MODEL)), lambda b, t: (GLA_ROWS, 0)),
            pl.BlockSpec(memory_space=pl.ANY),
            _const_spec((1, RANK, QK)),
            _const_spec((1, QK)),
            _const_spec((1, DV)),
            _const_spec(cw.shape),
            _const_spec(tri_wide.shape),
            _const_spec(tri_meta.shape),
            _const_spec(cmat2.shape),
            _const_spec(lvl.shape),
            pl.BlockSpec((up_slab, D_FF), lambda b, t: (b * n_tiles + t, 0)),
            pl.BlockSpec((down_slab, D_MODEL), lambda b, t: (b * n_tiles + t, 0)),
        ],
        out_specs=(
            pl.BlockSpec((1, MIX_TILE, D_MODEL), lambda b, t: (b, t, 0)),
            pl.BlockSpec((up_slab, D_FF), lambda b, t: (b * n_tiles + t, 0)),
            pl.BlockSpec((down_slab, D_MODEL), lambda b, t: (b * n_tiles + t, 0)),
        ),
        out_shape=(
            jax.ShapeDtypeStruct((batch, seq, D_MODEL), F32),
            jax.ShapeDtypeStruct((D_MODEL, D_FF), BF16),
            jax.ShapeDtypeStruct((D_FF, D_MODEL), BF16),
        ),
        scratch_shapes=[
            pltpu.VMEM((GLA_ROWS, D_MODEL), BF16),
            pltpu.VMEM((3 * CONV_CH, D_MODEL), BF16),
            pltpu.VMEM((LANES, D_MODEL), BF16),
            pltpu.VMEM((LANES, QK), BF16),
            pltpu.VMEM((D_MODEL, D_MODEL), BF16),
            pltpu.VMEM((DV, QK), F32),
            pltpu.VMEM((SUBLANES, CONV_CH), F32),
            pltpu.VMEM((STAGE_SLOTS, PREP_SLAB, D_MODEL), F32),
            pltpu.SemaphoreType.DMA((STAGE_SLOTS,)),
            pltpu.VMEM((DV, QK), F32),
            pltpu.VMEM((DV, QK), F32),
            pltpu.VMEM((SUBLANES, CONV_CH), F32),
            pltpu.VMEM((MIX_TILE, D_MODEL), BF16),
            pltpu.VMEM((MIX_TILE, D_MODEL), BF16),
            pltpu.VMEM((MIX_TILE, LANES), F32),
            pltpu.VMEM((MIX_TILE, QK), F32),
            pltpu.VMEM((MIX_TILE, QK), F32),
            pltpu.VMEM((MIX_TILE, GW), F32),
            pltpu.VMEM((MIX_TILE, QK), F32),
            pltpu.VMEM((MIX_TILE, QK), F32),
            pltpu.VMEM((MIX_TILE, QK), F32),
            pltpu.VMEM((MIX_TILE, GW), F32),
            pltpu.VMEM((MIX_TILE, GW), F32),
            pltpu.VMEM((MIX_TILE, CONV_CH), BF16),
        ],
        compiler_params=pltpu.CompilerParams(
            vmem_limit_bytes=VMEM_LIMIT, dimension_semantics=("arbitrary", "arbitrary")),
        name="gla_conv_mixer",
    )(x, x, meta_tokens, nw_mix, w_in_t, w_in_t, w_out[0], w_gate_up, bg, gnw, cw, tri_wide, tri_meta, cmat2, lvl,
      w_up[0], w_down[0])

    rows = batch * seq
    out = pl.pallas_call(
        _mlp_kernel,
        grid=(rows // MLP_TILE,),
        in_specs=[
            pl.BlockSpec((MLP_TILE, D_MODEL), lambda i: (i, 0)),
            _const_spec((1, D_MODEL)),
            _const_spec(wup.shape),
            _const_spec(wdown.shape),
            _const_spec((1, D_MODEL)),
        ],
        out_specs=pl.BlockSpec((MLP_TILE, D_MODEL), lambda i: (i, 0)),
        out_shape=jax.ShapeDtypeStruct((rows, D_MODEL), F32),
        compiler_params=pltpu.CompilerParams(
            vmem_limit_bytes=VMEM_LIMIT, dimension_semantics=("arbitrary",)),
        name="relu2_mlp_final_norm",
    )(h1.reshape(rows, D_MODEL), nw_mlp, wup, wdown, fw)
    return out.reshape(batch, seq, D_MODEL)
```

```python
import numpy as np
import jax
import jax.numpy as jnp
from jax import lax
from jax.experimental import pallas as pl
from jax.experimental.pallas import tpu as pltpu

D_MODEL = 1024
N_META = 16
HEADS = 4
DK = 64
DV = 128
QK = HEADS * DK
GW = HEADS * DV
RANK = 16
GATE_NORM = 16.0
CONV_CH = 512
CONV_TAPS = 3
D_FF = 4096
GLA_ROWS = 2 * QK + 2 * GW
PROJ_WIDTH = GLA_ROWS + RANK + 3 * CONV_CH
EPS = 1e-6

LANES = 128
SUBLANES = 8
VMEM_LIMIT = 56 * 1024 * 1024

MIX_TILE = 512
WIDE_CHUNK = 128
CHUNK = 64
HALF_WIDTHS = (32, 16, 8, 4, 2, 1)
N_LEVELS = len(HALF_WIDTHS)
WIDE_KEY_BOUND = 1e18
GATE_TERMS = 6
PREP_SLAB = 512
STAGE_SLOTS = 4
MLP_TILE = 1024
MLP_ROWS = 256
FF_CHUNK = 1024

BF16 = jnp.bfloat16
F32 = jnp.float32


def _rms(x, w):
    ms = jnp.mean(x * x, axis=-1, keepdims=True)
    return x * lax.rsqrt(ms + EPS) * w


def _log_sigmoid(z):
    return jnp.minimum(z, 0.0) - jnp.log(1.0 + jnp.exp(-jnp.abs(z)))


def _silu(g):
    half = 0.5 * g
    return half + half * jnp.tanh(half)


def _dot(a, b):
    return jnp.dot(a, b, preferred_element_type=F32)


def _dot_nt(a, b):
    return lax.dot_general(a, b, (((1,), (1,)), ((), ())), preferred_element_type=F32)


def _split3(x):
    hi = x.astype(BF16)
    r1 = x - hi.astype(F32)
    mid = r1.astype(BF16)
    lo = (r1 - mid.astype(F32)).astype(BF16)
    return hi, mid, lo


def _stack_heads(x):
    lane_head = lax.broadcasted_iota(jnp.int32, x.shape, 1) >> (DK.bit_length() - 1)
    return jnp.concatenate([jnp.where(lane_head == h, x, 0.0) for h in range(HEADS)], axis=0)


def _rows_by_head(v):
    return jnp.concatenate([v[:, h * DV:(h + 1) * DV] for h in range(HEADS)], axis=0)


def _gate_log_decay(gr6, wgu6, bg):
    hi, mid, lo = _split3(gr6)
    group = lax.broadcasted_iota(jnp.int32, gr6.shape, 1) >> (RANK.bit_length() - 1)
    lhs = jnp.where(group < 3, hi, jnp.where(group < 5, mid, lo))
    z = _dot(lhs, wgu6) + bg
    return _log_sigmoid(z) / GATE_NORM


def _cumsum_rows(tri2, gk):
    hi, mid, _ = _split3(gk)
    return _dot(tri2, jnp.concatenate([hi, mid], axis=0))


def _state_increment(kdec, v):
    return _dot(_rows_by_head(v).T.astype(BF16), _stack_heads(kdec).astype(BF16))


def _state_update(st, kdec, v, decay_last):
    return st * decay_last + _state_increment(kdec, v)


def _weight_slabs(win_hbm, wout_hbm, wa_ref, wb_ref, wout_ref):
    conv_row0 = GLA_ROWS + RANK
    slabs = []
    for r in range(0, GLA_ROWS, PREP_SLAB):
        slabs.append((win_hbm.at[pl.ds(r, PREP_SLAB), :], wa_ref.at[pl.ds(r, PREP_SLAB), :]))
    for r in range(0, 3 * CONV_CH, PREP_SLAB):
        slabs.append((win_hbm.at[pl.ds(conv_row0 + r, PREP_SLAB), :], wb_ref.at[pl.ds(r, PREP_SLAB), :]))
    for r in range(0, D_MODEL, PREP_SLAB):
        slabs.append((wout_hbm.at[pl.ds(r, PREP_SLAB), :], wout_ref.at[pl.ds(r, PREP_SLAB), :]))
    return slabs


def _stage_weights(meta_ref, nw_ref, win_hbm, wg_ref, wout_hbm, wgu32_ref, bg_ref, trim_ref,
                   wa_ref, wb_ref, wgr_ref, wgu_ref, wout_ref, st0_ref, tail0_ref, stage_s, sems, first_tile):
    slabs = _weight_slabs(win_hbm, wout_hbm, wa_ref, wb_ref, wout_ref)
    n_slots = stage_s.shape[0]
    copies = [pltpu.make_async_copy(src, stage_s.at[i % n_slots], sems.at[i % n_slots])
              for i, (src, _) in enumerate(slabs)]
    for copy in copies[:n_slots]:
        copy.start()
    hn = _rms(meta_ref[...], nw_ref[...]).astype(BF16)
    g_hi, g_mid, g_lo = _split3(wgu32_ref[0])
    zeros = jnp.zeros((LANES - GATE_TERMS * RANK, QK), BF16)
    wgu_ref[...] = jnp.concatenate([g_hi, g_mid, g_lo, g_hi, g_mid, g_hi, zeros], axis=0)
    gr = wg_ref[...].astype(BF16)
    wgr_ref[...] = jnp.concatenate(
        [gr] * GATE_TERMS + [jnp.zeros((LANES - GATE_TERMS * RANK, D_MODEL), BF16)], axis=0)
    gk = _gate_log_decay(_dot_nt(hn, wgr_ref[...]), wgu_ref[...], bg_ref[...])
    b = _cumsum_rows(trim_ref[...], gk)
    b_last = b[N_META - 1:N_META, :]
    first_tile()
    n_proj = (GLA_ROWS + 3 * CONV_CH) // PREP_SLAB
    proj = []
    for i, (_, dst) in enumerate(slabs):
        copies[i].wait()
        dst[...] = stage_s[i % n_slots].astype(BF16)
        if i + n_slots < len(copies):
            copies[i + n_slots].start()
        if i < n_proj:
            proj.append(_dot_nt(hn, dst[...]))
        if i == n_proj - 1:
            proj = jnp.concatenate(proj, axis=1)
            k = proj[:, QK:2 * QK]
            v = proj[:, 2 * QK:2 * QK + GW]
            cc = proj[:, GLA_ROWS + CONV_CH:GLA_ROWS + 2 * CONV_CH]
            cx = proj[:, GLA_ROWS + 2 * CONV_CH:GLA_ROWS + 3 * CONV_CH]
            st0_ref[...] = _state_update(jnp.zeros((DV, QK), F32), k * jnp.exp(b_last - b), v, jnp.exp(b_last))
            u = cc * cx
            tail0_ref[...] = u[N_META - SUBLANES:N_META, :]


def _wide_chunks():
    return [slice(c * WIDE_CHUNK, (c + 1) * WIDE_CHUNK) for c in range(MIX_TILE // WIDE_CHUNK)]


def _wide_states(v_s, decay_last, kp_s, st_ref):
    chunks = _wide_chunks()
    kv = [_state_increment(kp_s[rows, :] * decay_last[c], v_s[rows, :]) for c, rows in enumerate(chunks)]
    states = [st_ref[...]]
    for c in range(len(chunks)):
        states.append(states[c] * decay_last[c] + kv[c])
    st_ref[...] = states[-1]
    return states[:-1]


def _wide_scores(qp_s, kp_s, states):
    return [_dot_nt(_stack_heads(qp_s[rows, :]).astype(BF16),
                    jnp.concatenate([kp_s[rows, :].astype(BF16), states[c].astype(BF16)], axis=0))
            for c, rows in enumerate(_wide_chunks())]


def _wide_outputs(sc, v_s, o_s):
    row = lax.broadcasted_iota(jnp.int32, (HEADS * WIDE_CHUNK, WIDE_CHUNK), 0) & (WIDE_CHUNK - 1)
    col = lax.broadcasted_iota(jnp.int32, (HEADS * WIDE_CHUNK, WIDE_CHUNK), 1)
    causal = row >= col
    for c, rows in enumerate(_wide_chunks()):
        scores = jnp.where(causal, sc[c][:, 0:WIDE_CHUNK], 0.0).astype(BF16)
        for h in range(HEADS):
            hr = slice(h * WIDE_CHUNK, (h + 1) * WIDE_CHUNK)
            o_h = _dot(scores[hr], v_s[rows, h * DV:(h + 1) * DV].astype(BF16))
            o_s[rows, h * DV:(h + 1) * DV] = o_h + sc[c][hr, WIDE_CHUNK:WIDE_CHUNK + DV]


def _gla_stable(q_s, k_s, v_s, gk_s, o_s, st_ref, cmat_ref, lvl_ref):
    lvl = lvl_ref[...]

    def chunk_body(c, st):
        r0 = pl.multiple_of(c * CHUNK, CHUNK)
        q = q_s[pl.ds(r0, CHUNK), :]
        k = k_s[pl.ds(r0, CHUNK), :]
        v = v_s[pl.ds(r0, CHUNK), :]
        gk = gk_s[pl.ds(r0, CHUNK), :]
        br = _cumsum_rows(cmat_ref[...], gk)
        b = br[0:CHUNK]
        row = lax.broadcasted_iota(jnp.int32, (CHUNK, QK), 0)
        a = jnp.zeros((HEADS * CHUNK, CHUNK), F32)
        for l, w in enumerate(HALF_WIDTHS):
            ref = br[(l + 1) * CHUNK:(l + 2) * CHUNK]
            e = jnp.exp(-jnp.abs(b - ref))
            second = (row & w) != 0
            ql = jnp.where(second, q * e, 0.0)
            kl = jnp.where(second, 0.0, k * e)
            g = _dot_nt(_stack_heads(ql).astype(BF16), kl.astype(BF16))
            a = jnp.where(lvl == l, g, a)
        g = _dot_nt(_stack_heads(q).astype(BF16), k.astype(BF16))
        a = jnp.where(lvl == N_LEVELS, g, a)
        a = a.astype(BF16)
        o_inter = _dot_nt(_stack_heads(q * jnp.exp(b)).astype(BF16), st.astype(BF16))
        for h in range(HEADS):
            o_h = _dot(a[h * CHUNK:(h + 1) * CHUNK], v[:, h * DV:(h + 1) * DV].astype(BF16))
            o_s[pl.ds(r0, CHUNK), h * DV:(h + 1) * DV] = o_h + o_inter[h * CHUNK:(h + 1) * CHUNK]
        b_last = b[CHUNK - 1:CHUNK, :]
        return _state_update(st, k * jnp.exp(b_last - b), v, jnp.exp(b_last))

    st_ref[...] = lax.fori_loop(0, MIX_TILE // CHUNK, chunk_body, st_ref[...])


def _mixer_kernel(x_ref, xnext_ref, meta_ref, nw_ref, win_hbm, wg_ref, wout_hbm, wgu32_ref, bg_ref, gnw_ref,
                  cw_ref, tri_ref, trim_ref, cmat_ref, lvl_ref, nwm_ref, wup32_ref, wdown32_ref,
                  h1_ref, hn2_ref, wup16_ref, wdown16_ref,
                  wa_ref, wb_ref, wgr_ref, wgu_ref, wout_ref, st0_ref, tail0_ref, stage_s, sems,
                  st_ref, stprev_s, tail_ref, hn_s, hnext_s, grnext_s, q_s, k_s, v_s, gk_s, qp_s, kp_s, o_s,
                  gate_s, yconv_s):
    t = pl.program_id(1)

    @pl.when(pl.program_id(0) + t == 0)
    def _():
        def first_tile():
            hnext_s[...] = _rms(x_ref[0], nw_ref[...]).astype(BF16)
            grnext_s[...] = _dot_nt(hnext_s[...], wgr_ref[...])

        _stage_weights(meta_ref, nw_ref, win_hbm, wg_ref, wout_hbm, wgu32_ref, bg_ref, trim_ref,
                       wa_ref, wb_ref, wgr_ref, wgu_ref, wout_ref, st0_ref, tail0_ref, stage_s, sems, first_tile)

    @pl.when(t == 0)
    def _():
        st_ref[...] = st0_ref[...]
        tail_ref[...] = tail0_ref[...]

    wup16_ref[...] = wup32_ref[...].astype(BF16)
    wdown16_ref[...] = wdown32_ref[...].astype(BF16)

    hn_s[...] = hnext_s[...]
    gk = _gate_log_decay(grnext_s[...], wgu_ref[...], bg_ref[...])
    qk = _dot_nt(hn_s[...], wa_ref[0:2 * QK, :])
    q = qk[:, 0:QK] * (DK ** -0.5)
    k = qk[:, QK:2 * QK]
    n_wide = MIX_TILE // WIDE_CHUNK
    gk_cols = jnp.concatenate([gk[c * WIDE_CHUNK:(c + 1) * WIDE_CHUNK] for c in range(n_wide)], axis=1)
    b_cols = _cumsum_rows(tri_ref[...], gk_cols)
    b = jnp.concatenate([b_cols[:, c * QK:(c + 1) * QK] for c in range(n_wide)], axis=0)
    kp = k * jnp.exp(-b)
    decay_last = [jnp.exp(b[rows.stop - 1:rows.stop, :]) for rows in _wide_chunks()]
    qp_s[...] = q * jnp.exp(b)
    kp_s[...] = kp
    n_bad = jnp.sum(jnp.where(jnp.abs(kp) <= WIDE_KEY_BOUND, 0.0, 1.0))
    wide_ok = n_bad == 0.0
    stprev_s[...] = st_ref[...]
    v_s[...] = _dot_nt(hn_s[...], wa_ref[2 * QK:2 * QK + GW, :])
    cc = _dot_nt(hn_s[...], wb_ref[CONV_CH:2 * CONV_CH, :])
    states = _wide_states(v_s, decay_last, kp_s, st_ref)
    cx = _dot_nt(hn_s[...], wb_ref[2 * CONV_CH:3 * CONV_CH, :])
    sc = _wide_scores(qp_s, kp_s, states)
    hnext = _rms(xnext_ref[0], nw_ref[...]).astype(BF16)
    hnext_s[...] = hnext
    grnext_s[...] = _dot_nt(hnext, wgr_ref[...])
    cb = _dot_nt(hn_s[...], wb_ref[0:CONV_CH, :])
    _wide_outputs(sc, v_s, o_s)
    gate_s[...] = _silu(_dot_nt(hn_s[...], wa_ref[2 * QK + GW:2 * QK + 2 * GW, :]))

    u = cc * cx
    first_row = lax.broadcasted_iota(jnp.int32, u.shape, 0) == 0
    u1 = jnp.where(first_row, tail_ref[SUBLANES - 1:SUBLANES, :], pltpu.roll(u, 1, axis=0))
    u2 = jnp.where(first_row, tail_ref[SUBLANES - 2:SUBLANES - 1, :], pltpu.roll(u1, 1, axis=0))
    tail_ref[...] = u[MIX_TILE - SUBLANES:MIX_TILE, :]
    w0, w1, w2 = (cw_ref[:, tap * CONV_CH:(tap + 1) * CONV_CH] for tap in range(CONV_TAPS))
    yconv_s[...] = (cb * (w0 * u2 + w1 * u1 + w2 * u)).astype(BF16)

    def mix_out():
        gnw = gnw_ref[...]
        y_gla = jnp.concatenate(
            [_rms(o_s[:, h * DV:(h + 1) * DV], gnw) for h in range(HEADS)], axis=1) * gate_s[...]
        mixed = (_dot(yconv_s[...], wout_ref[GW:GW + CONV_CH, :])
                 + _dot(y_gla.astype(BF16), wout_ref[0:GW, :]))
        h1 = x_ref[0] + mixed
        h1_ref[0] = h1
        hn2_ref[0] = _rms(h1, nwm_ref[...]).astype(BF16)

    mix_out()

    @pl.when(jnp.logical_not(wide_ok))
    def _():
        st_ref[...] = stprev_s[...]
        qk_again = _dot_nt(hn_s[...], wa_ref[0:2 * QK, :])
        q_s[...] = qk_again[:, 0:QK] * (DK ** -0.5)
        k_s[...] = qk_again[:, QK:2 * QK]
        gk_s[...] = _gate_log_decay(_dot_nt(hn_s[...], wgr_ref[...]), wgu_ref[...], bg_ref[...])
        _gla_stable(q_s, k_s, v_s, gk_s, o_s, st_ref, cmat_ref, lvl_ref)
        mix_out()


def _mlp_kernel(h_ref, hn_ref, wup_ref, wdown_ref, fw_ref, out_ref):
    blocks = [slice(r, r + MLP_ROWS) for r in range(0, MLP_TILE, MLP_ROWS)]
    hn = [hn_ref[rows, :] for rows in blocks]
    acc = [jnp.zeros((MLP_ROWS, D_MODEL), F32) for _ in blocks]
    for f in range(0, D_FF, FF_CHUNK):
        for i in range(len(blocks)):
            up = _dot(hn[i], wup_ref[:, f:f + FF_CHUNK])
            act = jnp.square(jnp.maximum(up, 0.0)).astype(BF16)
            acc[i] = acc[i] + _dot(act, wdown_ref[f:f + FF_CHUNK, :])
    for i, rows in enumerate(blocks):
        out_ref[rows, :] = _rms(h_ref[rows, :] + acc[i], fw_ref[...])


def _tri2(n):
    tri = np.tril(np.ones((n, n), np.float32))
    return np.concatenate([tri, tri], axis=1)


def _cumsum_and_ref_matrix():
    tri = np.tril(np.ones((CHUNK, CHUNK), np.float32))
    blocks = [tri]
    idx = np.arange(CHUNK)
    for w in HALF_WIDTHS:
        ref_row = (idx // (2 * w)) * (2 * w) + w - 1
        blocks.append(tri[ref_row])
    return np.concatenate(blocks, axis=0)


def _level_matrix():
    i = np.arange(CHUNK)[:, None]
    j = np.arange(CHUNK)[None, :]
    lvl = np.full((CHUNK, CHUNK), -1, np.int32)
    for l, w in enumerate(HALF_WIDTHS):
        same = (i // (2 * w)) == (j // (2 * w))
        lvl[same & ((i & w) != 0) & ((j & w) == 0)] = l
    lvl[i == j] = N_LEVELS
    return np.tile(lvl, (HEADS, 1))


def _const_spec(shape):
    return pl.BlockSpec(shape, lambda *_: (0,) * len(shape))


def kernel(x, meta_tokens, norm_mix_w, w_in, w_gate_up, b_gate, gla_norm_w, conv_w, w_out,
           norm_mlp_w, w_up, w_down, norm_final_w):
    batch, seq, _ = x.shape
    assert seq % MIX_TILE == 0 and (batch * seq) % MLP_TILE == 0
    assert norm_mix_w.shape[0] == 1, "single layer"
    assert conv_w.shape == (1, CONV_TAPS, CONV_CH)

    bg = b_gate[0].reshape(1, QK)
    nw_mix = norm_mix_w[0].reshape(1, D_MODEL)
    gnw = gla_norm_w[0].reshape(1, DV)
    cw = conv_w[0].reshape(1, CONV_TAPS * CONV_CH)
    nw_mlp = norm_mlp_w[0].reshape(1, D_MODEL)
    fw = norm_final_w.reshape(1, D_MODEL)

    cmat = _cumsum_and_ref_matrix()
    cmat2 = jnp.asarray(np.concatenate([cmat, cmat], axis=1), BF16)
    tri_meta = jnp.asarray(_tri2(N_META), BF16)
    tri_wide = jnp.asarray(_tri2(WIDE_CHUNK), BF16)
    lvl = jnp.asarray(_level_matrix())

    w_in_t = jnp.swapaxes(w_in, 1, 2)[0]
    assert GLA_ROWS % PREP_SLAB == 0 and (3 * CONV_CH) % PREP_SLAB == 0 and D_MODEL % PREP_SLAB == 0

    n_tiles = seq // MIX_TILE

    def next_tile(b, t):
        nxt = jnp.minimum(b * n_tiles + t + 1, batch * n_tiles - 1)
        return nxt // n_tiles, nxt % n_tiles, 0

    n_steps = batch * n_tiles
    assert D_MODEL % n_steps == 0 and D_FF % n_steps == 0
    up_slab, down_slab = D_MODEL // n_steps, D_FF // n_steps

    h1, hn2, wup, wdown = pl.pallas_call(
        _mixer_kernel,
        grid=(batch, n_tiles),
        in_specs=[
            pl.BlockSpec((1, MIX_TILE, D_MODEL), lambda b, t: (b, t, 0)),
            pl.BlockSpec((1, MIX_TILE, D_MODEL), next_tile),
            _const_spec((N_META, D_MODEL)),
            _const_spec((1, D_MODEL)),
            pl.BlockSpec(memory_space=pl.ANY),
            pl.BlockSpec((pl.Element(RANK), pl.Element(D_MODEL)), lambda b, t: (GLA_ROWS, 0)),
            pl.BlockSpec(memory_space=pl.ANY),
            _const_spec((1, RANK, QK)),
            _const_spec((1, QK)),
            _const_spec((1, DV)),
            _const_spec(cw.shape),
            _const_spec(tri_wide.shape),
            _const_spec(tri_meta.shape),
            _const_spec(cmat2.shape),
            _const_spec(lvl.shape),
            _const_spec((1, D_MODEL)),
            pl.BlockSpec((up_slab, D_FF), lambda b, t: (b * n_tiles + t, 0)),
            pl.BlockSpec((down_slab, D_MODEL), lambda b, t: (b * n_tiles + t, 0)),
        ],
        out_specs=(
            pl.BlockSpec((1, MIX_TILE, D_MODEL), lambda b, t: (b, t, 0)),
            pl.BlockSpec((1, MIX_TILE, D_MODEL), lambda b, t: (b, t, 0)),
            pl.BlockSpec((up_slab, D_FF), lambda b, t: (b * n_tiles + t, 0)),
            pl.BlockSpec((down_slab, D_MODEL), lambda b, t: (b * n_tiles + t, 0)),
        ),
        out_shape=(
            jax.ShapeDtypeStruct((batch, seq, D_MODEL), F32),
            jax.ShapeDtypeStruct((batch, seq, D_MODEL), BF16),
            jax.ShapeDtypeStruct((D_MODEL, D_FF), BF16),
            jax.ShapeDtypeStruct((D_FF, D_MODEL), BF16),
        ),
        scratch_shapes=[
            pltpu.VMEM((GLA_ROWS, D_MODEL), BF16),
            pltpu.VMEM((3 * CONV_CH, D_MODEL), BF16),
            pltpu.VMEM((LANES, D_MODEL), BF16),
            pltpu.VMEM((LANES, QK), BF16),
            pltpu.VMEM((D_MODEL, D_MODEL), BF16),
            pltpu.VMEM((DV, QK), F32),
            pltpu.VMEM((SUBLANES, CONV_CH), F32),
            pltpu.VMEM((STAGE_SLOTS, PREP_SLAB, D_MODEL), F32),
            pltpu.SemaphoreType.DMA((STAGE_SLOTS,)),
            pltpu.VMEM((DV, QK), F32),
            pltpu.VMEM((DV, QK), F32),
            pltpu.VMEM((SUBLANES, CONV_CH), F32),
            pltpu.VMEM((MIX_TILE, D_MODEL), BF16),
            pltpu.VMEM((MIX_TILE, D_MODEL), BF16),
            pltpu.VMEM((MIX_TILE, LANES), F32),
            pltpu.VMEM((MIX_TILE, QK), F32),
            pltpu.VMEM((MIX_TILE, QK), F32),
            pltpu.VMEM((MIX_TILE, GW), F32),
            pltpu.VMEM((MIX_TILE, QK), F32),
            pltpu.VMEM((MIX_TILE, QK), F32),
            pltpu.VMEM((MIX_TILE, QK), F32),
            pltpu.VMEM((MIX_TILE, GW), F32),
            pltpu.VMEM((MIX_TILE, GW), F32),
            pltpu.VMEM((MIX_TILE, CONV_CH), BF16),
        ],
        compiler_params=pltpu.CompilerParams(
            vmem_limit_bytes=VMEM_LIMIT, dimension_semantics=("arbitrary", "arbitrary")),
        name="gla_conv_mixer",
    )(x, x, meta_tokens, nw_mix, w_in_t, w_in_t, w_out[0], w_gate_up, bg, gnw, cw, tri_wide, tri_meta, cmat2, lvl,
      nw_mlp, w_up[0], w_down[0])

    rows = batch * seq
    out = pl.pallas_call(
        _mlp_kernel,
        grid=(rows // MLP_TILE,),
        in_specs=[
            pl.BlockSpec((MLP_TILE, D_MODEL), lambda i: (i, 0)),
            pl.BlockSpec((MLP_TILE, D_MODEL), lambda i: (i, 0)),
            _const_spec(wup.shape),
            _const_spec(wdown.shape),
            _const_spec((1, D_MODEL)),
        ],
        out_specs=pl.BlockSpec((MLP_TILE, D_MODEL), lambda i: (i, 0)),
        out_shape=jax.ShapeDtypeStruct((rows, D_MODEL), F32),
        compiler_params=pltpu.CompilerParams(
            vmem_limit_bytes=VMEM_LIMIT, dimension_semantics=("arbitrary",)),
        name="relu2_mlp_final_norm",
    )(h1.reshape(rows, D_MODEL), hn2.reshape(rows, D_MODEL), wup, wdown, fw)
    return out.reshape(batch, seq, D_MODEL)
```
